```python
import jax, jax.numpy as jnp
from jax import lax
import numpy as np

D_MODEL = 2048
BATCH = 4
SEQ = 4096
DEPTH = 4

CHUNK = 64
N_META = 16
D_MIX = D_MODEL
CONV_W = D_MIX // 2
CONV_K = 3
RWKV_W = D_MIX - CONV_W
HEAD_DIM = 64
RWKV_HEADS = RWKV_W // HEAD_DIM
R_DECAY = 64
R_ICLR = 64
R_VRES = 32
R_GATE = 160
D_FF = 5632
NORM_EPS = 1e-6
LNX_EPS = 64e-5
P_CONV = 3 * CONV_W
P_RWKV = 3 * RWKV_W + R_DECAY + R_ICLR + R_GATE
P_IN = P_CONV + P_RWKV

kernel_name = 'hymba_style_conv_rwkv7_macaron_trunk'


def rms_norm(x, g):
    xf = x.astype(jnp.float32)
    y = xf * lax.rsqrt(jnp.mean(xf * xf, axis=-1, keepdims=True) + NORM_EPS)
    return (y * g.astype(jnp.float32)).astype(x.dtype)


def swiglu(h, w_gu, w_down):
    gate, up = jnp.split(h @ w_gu, 2, axis=-1)
    return (jax.nn.silu(gate) * up) @ w_down


def token_shift(p, mu):
    p_prev = jnp.pad(p, ((0, 0), (1, 0), (0, 0)))[:, :-1]
    return p + mu * (p_prev - p)


def short_conv_group(p, conv_w, conv_g):
    gate_b, gate_c, u = jnp.split(p, 3, axis=-1)
    u = gate_c * u
    L = u.shape[1]
    u_pad = jnp.pad(u, ((0, 0), (CONV_K - 1, 0), (0, 0)))
    conv = conv_w[0] * u_pad[:, 0:L]
    for j in range(1, CONV_K):
        conv = conv + conv_w[j] * u_pad[:, j:j + L]
    return rms_norm(gate_b * conv, conv_g)


def wkv7_scan(r, decay, k, v, a_vec, b_vec):
    B, L, H, N = r.shape

    def step(S, inp):
        r_t, w_t, k_t, v_t, a_t, b_t = inp
        sa = jnp.einsum('bhvk,bhk->bhv', S, a_t)
        S = S * w_t[:, :, None, :] + sa[..., None] * b_t[:, :, None, :] + v_t[..., None] * k_t[:, :, None, :]
        return S, jnp.einsum('bhvk,bhk->bhv', S, r_t)

    def run_chunk(S, seq):
        return lax.scan(step, S, seq)

    seq = tuple(jnp.moveaxis(t, 1, 0) for t in (r, decay, k, v, a_vec, b_vec))
    meta = tuple(t[:N_META] for t in seq)
    frames = tuple(t[N_META:].reshape(-1, CHUNK, B, H, N) for t in seq)
    S0 = jnp.zeros((B, H, N, N), jnp.float32)
    S, y_meta = run_chunk(S0, meta)
    S, y_frames = lax.scan(run_chunk, S, frames)
    y = jnp.concatenate([y_meta, y_frames.reshape(-1, B, H, N)], axis=0)
    return jnp.moveaxis(y, 0, 1)


def rwkv7_group(p, v_first, v_mix, w0, w2, a0, a2, g2, k_k, k_a, r_k, lnx_g, lnx_b):
    B, L, _ = p.shape
    f32 = jnp.float32
    split_at = [RWKV_W, 2 * RWKV_W, 3 * RWKV_W, 3 * RWKV_W + R_DECAY,
                3 * RWKV_W + R_DECAY + R_ICLR, P_RWKV]
    r, k, v, wd, ad, gd, vd = jnp.split(p, split_at, axis=-1)
    if v_mix is None:
        v_first = v
    else:
        v0, v2 = v_mix
        v = v + (v_first - v) * jax.nn.sigmoid(v0 + vd @ v2)
    wlog = -jax.nn.softplus(-(w0 + jnp.tanh(wd) @ w2).astype(f32)) - 0.5
    decay = jnp.exp(-jnp.exp(wlog))
    a = jax.nn.sigmoid((a0 + ad @ a2).astype(f32))
    g = jax.nn.sigmoid(gd) @ g2

    def heads(t):
        return t.astype(f32).reshape(B, L, RWKV_HEADS, HEAD_DIM)

    r_h, k_h, v_h, a_h, w_h = heads(r), heads(k), heads(v), heads(a), heads(decay)
    kk = heads(k * k_k)
    kk = kk / jnp.maximum(jnp.linalg.norm(kk, axis=-1, keepdims=True), 1e-12)
    k_h = k_h * (1.0 + (a_h - 1.0) * k_a.astype(f32).reshape(RWKV_HEADS, HEAD_DIM))
    y = wkv7_scan(r_h, w_h, k_h, v_h, -kk, kk * a_h)
    mean = jnp.mean(y, axis=-1, keepdims=True)
    var = jnp.mean(jnp.square(y - mean), axis=-1, keepdims=True)
    y = ((y - mean) * lax.rsqrt(var + LNX_EPS)).reshape(B, L, RWKV_W) * lnx_g + lnx_b
    bonus = jnp.sum(r_h * k_h * r_k.astype(f32), axis=-1, keepdims=True) * v_h
    y = (y + bonus.reshape(B, L, RWKV_W)) * g.astype(f32)
    return y.astype(p.dtype), v_first


def setup_inputs(seed: int = 0) -> dict:
    key = jax.random.key(seed)
    ks = iter(jax.random.split(key, 40))

    def nrm(shape, scale):
        return scale * jax.random.normal(next(ks), shape, jnp.float32)

    def gain(shape):
        return 1.0 + nrm(shape, 0.02)

    def unif(shape, lo, hi):
        return jax.random.uniform(next(ks), shape, jnp.float32, lo, hi)

    DM1 = DEPTH - 1
    return {
        'x': nrm((BATCH, SEQ, D_MODEL), 1.0),
        'meta_tokens': nrm((N_META, D_MODEL), 1.0),
        'ffn1_pre_g': gain((DEPTH, D_MODEL)),
        'ffn1_w_gu': nrm((DEPTH, D_MODEL, 2 * D_FF), D_MODEL ** -0.5),
        'ffn1_w_down': nrm((DEPTH, D_FF, D_MODEL), D_FF ** -0.5),
        'ffn1_post_g': gain((DEPTH, D_MODEL)),
        'mix_pre_g': gain((DEPTH, D_MODEL)),
        'w_in': nrm((DEPTH, D_MODEL, P_IN), D_MODEL ** -0.5),
        'w_in_vres': nrm((DM1, D_MODEL, R_VRES), D_MODEL ** -0.5),
        'mu_rwkv': unif((DEPTH, P_RWKV), 0.0, 1.0),
        'mu_vres': unif((DM1, R_VRES), 0.0, 1.0),
        'conv_w': nrm((DEPTH, CONV_K, CONV_W), CONV_K ** -0.5),
        'conv_norm_g': gain((DEPTH, CONV_W)),
        'decay_w0': unif((DEPTH, RWKV_W), -6.0, -1.0),
        'decay_w2': nrm((DEPTH, R_DECAY, RWKV_W), 0.1),
        'iclr_a0': nrm((DEPTH, RWKV_W), 0.1),
        'iclr_a2': nrm((DEPTH, R_ICLR, RWKV_W), 0.1),
        'vres_v0': nrm((DM1, RWKV_W), 0.1),
        'vres_v2': nrm((DM1, R_VRES, RWKV_W), 0.1),
        'gate_g2': nrm((DEPTH, R_GATE, RWKV_W), R_GATE ** -0.5),
        'k_k': 0.85 + nrm((DEPTH, RWKV_W), 0.05),
        'k_a': 1.0 + nrm((DEPTH, RWKV_W), 0.05),
        'r_k': nrm((DEPTH, RWKV_HEADS, HEAD_DIM), 0.1),
        'lnx_g': gain((DEPTH, RWKV_W)),
        'lnx_b': nrm((DEPTH, RWKV_W), 0.02),
        'w_out': nrm((DEPTH, D_MIX, D_MODEL), D_MIX ** -0.5),
        'mix_post_g': gain((DEPTH, D_MODEL)),
        'ffn2_pre_g': gain((DEPTH, D_MODEL)),
        'ffn2_w_gu': nrm((DEPTH, D_MODEL, 2 * D_FF), D_MODEL ** -0.5),
        'ffn2_w_down': nrm((DEPTH, D_FF, D_MODEL), D_FF ** -0.5),
        'ffn2_post_g': gain((DEPTH, D_MODEL)),
    }


def reference(x, meta_tokens, ffn1_pre_g, ffn1_w_gu, ffn1_w_down, ffn1_post_g,
              mix_pre_g, w_in, w_in_vres, mu_rwkv, mu_vres, conv_w, conv_norm_g,
              decay_w0, decay_w2, iclr_a0, iclr_a2, vres_v0, vres_v2, gate_g2,
              k_k, k_a, r_k, lnx_g, lnx_b, w_out, mix_post_g,
              ffn2_pre_g, ffn2_w_gu, ffn2_w_down, ffn2_post_g):
    B = x.shape[0]
    meta = jnp.broadcast_to(meta_tokens.astype(x.dtype)[None], (B, N_META, D_MODEL))
    h = jnp.concatenate([meta, x], axis=1)
    v_first = None
    for i in range(DEPTH):
        h = h + 0.5 * rms_norm(swiglu(rms_norm(h, ffn1_pre_g[i]), ffn1_w_gu[i], ffn1_w_down[i]), ffn1_post_g[i])
        u = rms_norm(h, mix_pre_g[i])
        if i == 0:
            w_cat, mu_cat, v_mix = w_in[i], mu_rwkv[i], None
        else:
            w_cat = jnp.concatenate([w_in[i], w_in_vres[i - 1]], axis=1)
            mu_cat = jnp.concatenate([mu_rwkv[i], mu_vres[i - 1]], axis=0)
            v_mix = (vres_v0[i - 1], vres_v2[i - 1])
        proj = u @ w_cat
        y_conv = short_conv_group(proj[..., :P_CONV], conv_w[i], conv_norm_g[i])
        p_rwkv = token_shift(proj[..., P_CONV:], mu_cat)
        y_rwkv, v_first = rwkv7_group(p_rwkv, v_first, v_mix, decay_w0[i], decay_w2[i],
                                      iclr_a0[i], iclr_a2[i], gate_g2[i], k_k[i], k_a[i],
                                      r_k[i], lnx_g[i], lnx_b[i])
        y = jnp.concatenate([y_conv, y_rwkv], axis=-1) @ w_out[i]
        h = h + rms_norm(y, mix_post_g[i])
        h = h + 0.5 * rms_norm(swiglu(rms_norm(h, ffn2_pre_g[i]), ffn2_w_gu[i], ffn2_w_down[i]), ffn2_post_g[i])
    return h[:, N_META:]
```

```python
import functools

import jax
import jax.numpy as jnp
from jax import lax
from jax.experimental import pallas as pl
from jax.experimental.pallas import tpu as pltpu

F32 = jnp.float32
BF16 = jnp.bfloat16
HIGHEST = lax.Precision.HIGHEST

NORM_EPS = 1e-6
LNX_EPS = 64e-5
N_META = 16
HEAD_DIM = 64
WKV_CHUNK = 64
R_DECAY, R_ICLR, R_GATE, R_VRES = 64, 64, 160, 32
LO_WD, LO_AD, LO_GD, LO_VD = 0, 128, 256, 512
LO_GD_W = 256
LO_W = 1024

VMEM_LIMIT_BYTES = 56 * 1024 * 1024


def _cparams(sem):
    return pltpu.CompilerParams(dimension_semantics=sem, vmem_limit_bytes=VMEM_LIMIT_BYTES)


def _rms(x, g):
    ms = jnp.mean(x * x, axis=-1, keepdims=True)
    return x * lax.rsqrt(ms + NORM_EPS) * g


def _ffn_kernel(h_ref, pre_g_ref, wg_ref, wu_ref, wd_ref, post_g_ref, o_ref, xn_ref, acc_ref):
    j = pl.program_id(1)

    @pl.when(j == 0)
    def _():
        xn_ref[...] = _rms(h_ref[...], pre_g_ref[...]).astype(BF16)
        acc_ref[...] = jnp.zeros_like(acc_ref)

    xn = xn_ref[...]
    gate = jnp.dot(xn, wg_ref[...], preferred_element_type=F32)
    up = jnp.dot(xn, wu_ref[...], preferred_element_type=F32)
    act = (gate * jax.nn.sigmoid(gate) * up).astype(BF16)
    acc_ref[...] += jnp.dot(act, wd_ref[...], preferred_element_type=F32)

    @pl.when(j == pl.num_programs(1) - 1)
    def _():
        o_ref[...] = h_ref[...] + 0.5 * _rms(acc_ref[...], post_g_ref[...])


def _ffn(h, pre_g, w_gu, w_down, post_g, *, tm, tf):
    T, D = h.shape
    FF = w_down.shape[0]
    nj = FF // tf
    return pl.pallas_call(
        _ffn_kernel,
        out_shape=jax.ShapeDtypeStruct((T, D), F32),
        grid=(T // tm, nj),
        in_specs=[
            pl.BlockSpec((tm, D), lambda i, j: (i, 0)),
            pl.BlockSpec((1, D), lambda i, j: (0, 0)),
            pl.BlockSpec((D, tf), lambda i, j: (0, j)),
            pl.BlockSpec((D, tf), lambda i, j: (0, j + nj)),
            pl.BlockSpec((tf, D), lambda i, j: (j, 0)),
            pl.BlockSpec((1, D), lambda i, j: (0, 0)),
        ],
        out_specs=pl.BlockSpec((tm, D), lambda i, j: (i, 0)),
        scratch_shapes=[pltpu.VMEM((tm, D), BF16), pltpu.VMEM((tm, D), F32)],
        compiler_params=_cparams(("parallel", "arbitrary")),
        name="ffn",
    )(h, pre_g, w_gu, w_gu, w_down, post_g)


def _proj_in_kernel(h_ref, g_ref, w_ref, o_ref, xn_ref):
    @pl.when(pl.program_id(1) == 0)
    def _():
        xn_ref[...] = _rms(h_ref[...], g_ref[...]).astype(BF16)

    o_ref[...] = jnp.dot(xn_ref[...], w_ref[...], preferred_element_type=F32)


def _proj_in(h, g, w_cat, *, tm, tn):
    T, D = h.shape
    P = w_cat.shape[1]
    return pl.pallas_call(
        _proj_in_kernel,
        out_shape=jax.ShapeDtypeStruct((T, P), F32),
        grid=(T // tm, P // tn),
        in_specs=[
            pl.BlockSpec((tm, D), lambda i, j: (i, 0)),
            pl.BlockSpec((1, D), lambda i, j: (0, 0)),
            pl.BlockSpec((D, tn), lambda i, j: (0, j)),
        ],
        out_specs=pl.BlockSpec((tm, tn), lambda i, j: (i, j)),
        scratch_shapes=[pltpu.VMEM((tm, D), BF16)],
        compiler_params=_cparams(("parallel", "arbitrary")),
        name="proj_in",
    )(h, g, w_cat)


def _conv_kernel(gb_ref, gc_ref, u_ref, cw_ref, cg_ref, o_ref, ext_ref):
    tt = u_ref.shape[0]

    @pl.when(pl.program_id(1) == 0)
    def _():
        ext_ref[0:8, :] = jnp.zeros((8, ext_ref.shape[1]), F32)

    uc = gc_ref[...] * u_ref[...]
    ext_ref[8:8 + tt, :] = uc
    conv = (cw_ref[0:1, :] * ext_ref[6:6 + tt, :] + cw_ref[1:2, :] * ext_ref[7:7 + tt, :]
            + cw_ref[2:3, :] * uc)
    o_ref[...] = _rms(gb_ref[...] * conv, cg_ref[...]).astype(BF16)
    ext_ref[0:8, :] = ext_ref[tt:tt + 8, :]


def _conv_group(proj, conv_w, conv_g, *, batch, tt):
    T = proj.shape[0]
    W = conv_w.shape[1]
    nt = T // batch // tt
    row = lambda b, t: b * nt + t
    return pl.pallas_call(
        _conv_kernel,
        out_shape=jax.ShapeDtypeStruct((T, W), BF16),
        grid=(batch, nt),
        in_specs=[
            pl.BlockSpec((tt, W), lambda b, t: (row(b, t), 0)),
            pl.BlockSpec((tt, W), lambda b, t: (row(b, t), 1)),
            pl.BlockSpec((tt, W), lambda b, t: (row(b, t), 2)),
            pl.BlockSpec(conv_w.shape, lambda b, t: (0, 0)),
            pl.BlockSpec((1, W), lambda b, t: (0, 0)),
        ],
        out_specs=pl.BlockSpec((tt, W), lambda b, t: (row(b, t), 0)),
        scratch_shapes=[pltpu.VMEM((tt + 8, W), F32)],
        compiler_params=_cparams(("parallel", "arbitrary")),
        name="conv_group",
    )(proj, proj, proj, conv_w, conv_g)


PV_W0, PV_A0, PV_V0, PV_KK, PV_KA, PV_RK, PV_LNG, PV_LNB, PV_MUR, PV_MUK, PV_MUV, PV_MULO = range(12)
PV_ROWS = 16


def _dot(x, y, precision=None):
    return jnp.dot(x, y, preferred_element_type=F32, precision=precision)


def _dot_nt(x, y, precision=None):
    return lax.dot_general(x, y, (((1,), (1,)), ((), ())), preferred_element_type=F32,
                           precision=precision)


def _dot_tn(x, y, precision=None):
    return lax.dot_general(x, y, (((0,), (0,)), ((), ())), preferred_element_type=F32,
                           precision=precision)


def _wkv_kernel(has_vres, *refs):
    if has_vres:
        (r_ref, k_ref, v_ref, lo_ref, vf_ref, pv_ref, w2_ref, a2_ref, g2_ref, v2_ref,
         y_ref, state_ref, carry_ref, ybuf_ref) = refs
    else:
        (r_ref, k_ref, v_ref, lo_ref, pv_ref, w2_ref, a2_ref, g2_ref,
         y_ref, vf_out_ref, state_ref, carry_ref, ybuf_ref) = refs
    C, W = r_ref.shape
    n_heads = W // HEAD_DIM

    @pl.when(pl.program_id(1) == 0)
    def _():
        state_ref[...] = jnp.zeros_like(state_ref)
        carry_ref[...] = jnp.zeros_like(carry_ref)

    def pv(i):
        return pv_ref[i:i + 1, :]

    row = lax.broadcasted_iota(jnp.int32, (C, W), 0)

    def token_shift(x_ref, slot, mu):
        x = x_ref[...]
        prev = jnp.where(row == 0, carry_ref[slot:slot + 1, :], pltpu.roll(x, 1, 0))
        carry_ref[slot:slot + 1, :] = x[C - 1:C, :]
        return x + mu * (prev - x)

    r = token_shift(r_ref, 0, pv(PV_MUR))
    k = token_shift(k_ref, 1, pv(PV_MUK))
    v = token_shift(v_ref, 2, pv(PV_MUV))
    lo = token_shift(lo_ref, 3, pv(PV_MULO))

    wd = lo[:, LO_WD:LO_WD + 128]
    ad = lo[:, LO_AD:LO_AD + 128]
    gd = lo[:, LO_GD:LO_GD + LO_GD_W]
    if has_vres:
        vd = lo[:, LO_VD:LO_VD + 128]
        mix = jax.nn.sigmoid(pv(PV_V0) + _dot(vd.astype(BF16), v2_ref[...]))
        v = v + (vf_ref[...] - v) * mix
    else:
        vf_out_ref[...] = v

    z = pv(PV_W0) + _dot(jnp.tanh(wd).astype(BF16), w2_ref[...])
    wlog = -jax.nn.softplus(-z) - 0.5
    logw = -jnp.exp(wlog)
    a = jax.nn.sigmoid(pv(PV_A0) + _dot(ad.astype(BF16), a2_ref[...]))
    g = _dot(jax.nn.sigmoid(gd).astype(BF16), g2_ref[...])
    kk_raw = k * pv(PV_KK)
    k2 = k * (1.0 + (a - 1.0) * pv(PV_KA))
    rk_bonus = r * k2 * pv(PV_RK)

    ti = lax.broadcasted_iota(jnp.int32, (C, C), 0)
    si = lax.broadcasted_iota(jnp.int32, (C, C), 1)
    incl = ti >= si
    strict = ti > si
    cs = _dot(incl.astype(F32), logw, HIGHEST)
    cs_last = cs[C - 1:C, :]
    e_pos = jnp.exp(cs)
    e_neg = jnp.exp(-cs)
    e_prev = jnp.exp(cs - logw)
    e_tail = jnp.exp(cs_last - cs)
    p_last = jnp.exp(cs_last)

    r_t = r * e_pos
    eye = (ti == si).astype(F32)
    lng, lnb = pv(PV_LNG), pv(PV_LNB)

    for h in range(n_heads):
        sl = slice(h * HEAD_DIM, (h + 1) * HEAD_DIM)
        kkh = kk_raw[:, sl]
        nrm = jnp.sqrt(jnp.sum(kkh * kkh, axis=-1, keepdims=True))
        kkh = kkh / jnp.maximum(nrm, 1e-12)
        ah = a[:, sl]
        a_t = -kkh * e_prev[:, sl]
        b_raw = kkh * ah
        b_t = b_raw * e_neg[:, sl]
        k_t = k2[:, sl] * e_neg[:, sl]
        b_p = b_raw * e_tail[:, sl]
        k_p = k2[:, sl] * e_tail[:, sl]
        rh_t = r_t[:, sl]
        vh = v[:, sl]

        a_ab = jnp.where(strict, _dot_nt(a_t, b_t, HIGHEST), 0.0)
        a_ak = jnp.where(strict, _dot_nt(a_t, k_t, HIGHEST), 0.0)
        a_rb = jnp.where(incl, _dot_nt(rh_t, b_t, HIGHEST), 0.0)
        a_rk = jnp.where(incl, _dot_nt(rh_t, k_t, HIGHEST), 0.0)

        inv = eye + a_ab
        apow = a_ab
        for _ in range(5):
            apow = _dot(apow, apow, HIGHEST)
            inv = inv + _dot(inv, apow, HIGHEST)

        s0 = state_ref[h]
        w_in = _dot_nt(a_t, s0, HIGHEST) + _dot(a_ak, vh, HIGHEST)
        u = _dot(inv, w_in, HIGHEST)
        y = _dot_nt(rh_t, s0, HIGHEST) + _dot(a_rb, u, HIGHEST) + _dot(a_rk, vh, HIGHEST)
        state_ref[h] = (s0 * p_last[:, sl] + _dot_tn(u, b_p, HIGHEST)
                        + _dot_tn(vh, k_p, HIGHEST))

        mean = jnp.mean(y, axis=-1, keepdims=True)
        yc = y - mean
        var = jnp.mean(yc * yc, axis=-1, keepdims=True)
        yn = yc * lax.rsqrt(var + LNX_EPS) * lng[:, sl] + lnb[:, sl]
        bonus = jnp.sum(rk_bonus[:, sl], axis=-1, keepdims=True) * vh
        ybuf_ref[:, sl] = yn + bonus

    y_ref[...] = (ybuf_ref[...] * g).astype(BF16)


def _rwkv_group(proj, v_first, pvec, w2, a2, g2, v2, *, batch):
    T = proj.shape[0]
    W = pvec.shape[1]
    C = WKV_CHUNK
    nc = T // batch // C
    has_vres = v_first is not None
    row = lambda b, c: b * nc + c
    tok = lambda col: pl.BlockSpec((C, W), lambda b, c: (row(b, c), col))
    full = lambda arr: pl.BlockSpec(arr.shape, lambda b, c: (0, 0))
    in_specs = [tok(3), tok(4), tok(5), tok(6)]
    args = [proj, proj, proj, proj]
    if has_vres:
        in_specs.append(tok(0))
        args.append(v_first)
    in_specs += [full(pvec), full(w2), full(a2), full(g2)]
    args += [pvec, w2, a2, g2]
    if has_vres:
        in_specs.append(full(v2))
        args.append(v2)
        out_shape = jax.ShapeDtypeStruct((T, W), BF16)
        out_specs = tok(0)
    else:
        out_shape = (jax.ShapeDtypeStruct((T, W), BF16), jax.ShapeDtypeStruct((T, W), F32))
        out_specs = (tok(0), tok(0))
    return pl.pallas_call(
        functools.partial(_wkv_kernel, has_vres),
        out_shape=out_shape,
        grid=(batch, nc),
        in_specs=in_specs,
        out_specs=out_specs,
        scratch_shapes=[
            pltpu.VMEM((W // HEAD_DIM, HEAD_DIM, HEAD_DIM), F32),
            pltpu.VMEM((8, W), F32),
            pltpu.VMEM((C, W), F32),
        ],
        compiler_params=_cparams(("parallel", "arbitrary")),
        name="rwkv_group",
    )(*args)


def _proj_out_kernel(yc_ref, yr_ref, w_ref, g_ref, h_ref, o_ref):
    half = yc_ref.shape[1]
    y = (jnp.dot(yc_ref[...], w_ref[0:half, :], preferred_element_type=F32)
         + jnp.dot(yr_ref[...], w_ref[half:2 * half, :], preferred_element_type=F32))
    o_ref[...] = h_ref[...] + _rms(y, g_ref[...])


def _proj_out(y_conv, y_rwkv, w_out, g, h, *, tm):
    T, D = h.shape
    half = y_conv.shape[1]
    return pl.pallas_call(
        _proj_out_kernel,
        out_shape=jax.ShapeDtypeStruct((T, D), F32),
        grid=(T // tm,),
        in_specs=[
            pl.BlockSpec((tm, half), lambda i: (i, 0)),
            pl.BlockSpec((tm, half), lambda i: (i, 0)),
            pl.BlockSpec(w_out.shape, lambda i: (0, 0)),
            pl.BlockSpec((1, D), lambda i: (0, 0)),
            pl.BlockSpec((tm, D), lambda i: (i, 0)),
        ],
        out_specs=pl.BlockSpec((tm, D), lambda i: (i, 0)),
        compiler_params=_cparams(("parallel",)),
        name="proj_out",
    )(y_conv, y_rwkv, w_out, g, h)


def _pad_rows(w, rows):
    return jnp.pad(w, ((0, rows - w.shape[0]), (0, 0)))


def _lora_cols(wd, ad, gd, vd):
    n = wd.shape[0]
    out = jnp.zeros((n, LO_W), wd.dtype)
    out = out.at[:, LO_WD:LO_WD + R_DECAY].set(wd)
    out = out.at[:, LO_AD:LO_AD + R_ICLR].set(ad)
    out = out.at[:, LO_GD:LO_GD + R_GATE].set(gd)
    if vd is not None:
        out = out.at[:, LO_VD:LO_VD + R_VRES].set(vd)
    return out


def kernel(x, meta_tokens, ffn1_pre_g, ffn1_w_gu, ffn1_w_down, ffn1_post_g, mix_pre_g, w_in, w_in_vres, mu_rwkv, mu_vres, conv_w, conv_norm_g, decay_w0, decay_w2, iclr_a0, iclr_a2, vres_v0, vres_v2, gate_g2, k_k, k_a, r_k, lnx_g, lnx_b, w_out, mix_post_g, ffn2_pre_g, ffn2_w_gu, ffn2_w_down, ffn2_post_g):
    B, S, D = x.shape
    depth = w_in.shape[0]
    conv_width = conv_w.shape[2]
    rw = decay_w0.shape[1]
    L = N_META + S
    Lp = -(-L // WKV_CHUNK) * WKV_CHUNK
    T = B * Lp
    tm = 640 if T % 640 == 0 else WKV_CHUNK
    tf = 512 if ffn1_w_down.shape[1] % 512 == 0 else ffn1_w_down.shape[1]
    tn = 1024
    p_conv = 3 * conv_width

    meta = jnp.broadcast_to(meta_tokens.astype(x.dtype)[None], (B, N_META, D))
    h = jnp.concatenate([meta, x, jnp.zeros((B, Lp - L, D), x.dtype)], axis=1).reshape(T, D)

    row = lambda a: a.reshape(1, -1)
    v_first = None
    for i in range(depth):
        h = _ffn(h, row(ffn1_pre_g[i]), ffn1_w_gu[i].astype(BF16), ffn1_w_down[i].astype(BF16),
                 row(ffn1_post_g[i]), tm=tm, tf=tf)

        wi = w_in[i]
        o = p_conv + 3 * rw
        lo_w = _lora_cols(wi[:, o:o + R_DECAY], wi[:, o + R_DECAY:o + R_DECAY + R_ICLR],
                          wi[:, o + R_DECAY + R_ICLR:], w_in_vres[i - 1] if i > 0 else None)
        w_cat = jnp.concatenate([wi[:, :o], lo_w], axis=1).astype(BF16)
        mu = mu_rwkv[i]
        mu_lo = _lora_cols(row(mu[3 * rw:3 * rw + R_DECAY]),
                           row(mu[3 * rw + R_DECAY:3 * rw + R_DECAY + R_ICLR]),
                           row(mu[3 * rw + R_DECAY + R_ICLR:]),
                           row(mu_vres[i - 1]) if i > 0 else None)
        zero = jnp.zeros((1, rw), F32)
        pvec = jnp.concatenate([
            row(decay_w0[i]), row(iclr_a0[i]), row(vres_v0[i - 1]) if i > 0 else zero,
            row(k_k[i]), row(k_a[i]), row(r_k[i]), row(lnx_g[i]), row(lnx_b[i]),
            row(mu[:rw]), row(mu[rw:2 * rw]), row(mu[2 * rw:3 * rw]), mu_lo,
            jnp.zeros((PV_ROWS - 12, rw), F32)], axis=0)

        proj = _proj_in(h, row(mix_pre_g[i]), w_cat, tm=tm, tn=tn)
        y_conv = _conv_group(proj, conv_w[i], row(conv_norm_g[i]), batch=B, tt=Lp // 5 if Lp % 40 == 0 else WKV_CHUNK)
        rw_args = (_pad_rows(decay_w2[i], 128).astype(BF16), _pad_rows(iclr_a2[i], 128).astype(BF16),
                   _pad_rows(gate_g2[i], LO_GD_W).astype(BF16),
                   _pad_rows(vres_v2[i - 1], 128).astype(BF16) if i > 0 else None)
        if i == 0:
            y_rwkv, v_first = _rwkv_group(proj, None, pvec, *rw_args, batch=B)
        else:
            y_rwkv = _rwkv_group(proj, v_first, pvec, *rw_args, batch=B)
        h = _proj_out(y_conv, y_rwkv, w_out[i].astype(BF16), row(mix_post_g[i]), h, tm=tm)

        h = _ffn(h, row(ffn2_pre_g[i]), ffn2_w_gu[i].astype(BF16), ffn2_w_down[i].astype(BF16),
                 row(ffn2_post_g[i]), tm=tm, tf=tf)
    return h.reshape(B, Lp, D)[:, N_META:L]
```

```python
import functools

import jax
import jax.numpy as jnp
from jax import lax
from jax.experimental import pallas as pl
from jax.experimental.pallas import tpu as pltpu

F32 = jnp.float32
BF16 = jnp.bfloat16

NORM_EPS = 1e-6
LNX_EPS = 64e-5
N_META = 16
HEAD_DIM = 64
WKV_CHUNK = 64
R_DECAY, R_ICLR, R_GATE, R_VRES = 64, 64, 160, 32
LO_WD, LO_AD, LO_GD, LO_VD = 0, 128, 256, 512
LO_GD_W = 256
LO_W = 1024
MXU_TILE = 256
GROUP_HEADS = MXU_TILE // HEAD_DIM

VMEM_LIMIT_BYTES = 56 * 1024 * 1024


def _cparams(sem):
    return pltpu.CompilerParams(dimension_semantics=sem, vmem_limit_bytes=VMEM_LIMIT_BYTES)


def _rms(x, g):
    ms = jnp.mean(x * x, axis=-1, keepdims=True)
    return x * lax.rsqrt(ms + NORM_EPS) * g


def _ffn_kernel(h_ref, pre_g_ref, wg_ref, wu_ref, wd_ref, post_g_ref, o_ref, xn_ref, acc_ref):
    j = pl.program_id(1)

    @pl.when(j == 0)
    def _():
        xn_ref[...] = _rms(h_ref[...], pre_g_ref[...]).astype(BF16)
        acc_ref[...] = jnp.zeros_like(acc_ref)

    xn = xn_ref[...]
    gate = jnp.dot(xn, wg_ref[...], preferred_element_type=F32)
    up = jnp.dot(xn, wu_ref[...], preferred_element_type=F32)
    act = (gate * jax.nn.sigmoid(gate) * up).astype(BF16)
    acc_ref[...] += jnp.dot(act, wd_ref[...], preferred_element_type=F32)

    @pl.when(j == pl.num_programs(1) - 1)
    def _():
        o_ref[...] = h_ref[...] + 0.5 * _rms(acc_ref[...], post_g_ref[...])


def _ffn(h, pre_g, w_gu, w_down, post_g, *, tm, tf):
    T, D = h.shape
    FF = w_down.shape[0]
    nj = FF // tf
    return pl.pallas_call(
        _ffn_kernel,
        out_shape=jax.ShapeDtypeStruct((T, D), F32),
        grid=(T // tm, nj),
        in_specs=[
            pl.BlockSpec((tm, D), lambda i, j: (i, 0)),
            pl.BlockSpec((1, D), lambda i, j: (0, 0)),
            pl.BlockSpec((D, tf), lambda i, j: (0, j)),
            pl.BlockSpec((D, tf), lambda i, j: (0, j + nj)),
            pl.BlockSpec((tf, D), lambda i, j: (j, 0)),
            pl.BlockSpec((1, D), lambda i, j: (0, 0)),
        ],
        out_specs=pl.BlockSpec((tm, D), lambda i, j: (i, 0)),
        scratch_shapes=[pltpu.VMEM((tm, D), BF16), pltpu.VMEM((tm, D), F32)],
        compiler_params=_cparams(("parallel", "arbitrary")),
        name="ffn",
    )(h, pre_g, w_gu, w_gu, w_down, post_g)


def _proj_in_kernel(h_ref, g_ref, w_ref, o_ref, xn_ref):
    @pl.when(pl.program_id(1) == 0)
    def _():
        xn_ref[...] = _rms(h_ref[...], g_ref[...]).astype(BF16)

    o_ref[...] = jnp.dot(xn_ref[...], w_ref[...], preferred_element_type=F32)


def _proj_in(h, g, w_cat, *, tm, tn):
    T, D = h.shape
    P = w_cat.shape[1]
    return pl.pallas_call(
        _proj_in_kernel,
        out_shape=jax.ShapeDtypeStruct((T, P), F32),
        grid=(T // tm, P // tn),
        in_specs=[
            pl.BlockSpec((tm, D), lambda i, j: (i, 0)),
            pl.BlockSpec((1, D), lambda i, j: (0, 0)),
            pl.BlockSpec((D, tn), lambda i, j: (0, j)),
        ],
        out_specs=pl.BlockSpec((tm, tn), lambda i, j: (i, j)),
        scratch_shapes=[pltpu.VMEM((tm, D), BF16)],
        compiler_params=_cparams(("parallel", "arbitrary")),
        name="proj_in",
    )(h, g, w_cat)


def _conv_kernel(gb_ref, gc_ref, u_ref, cw_ref, cg_ref, o_ref, ext_ref):
    tt = u_ref.shape[0]

    @pl.when(pl.program_id(1) == 0)
    def _():
        ext_ref[0:8, :] = jnp.zeros((8, ext_ref.shape[1]), F32)

    uc = gc_ref[...] * u_ref[...]
    ext_ref[8:8 + tt, :] = uc
    conv = (cw_ref[0:1, :] * ext_ref[6:6 + tt, :] + cw_ref[1:2, :] * ext_ref[7:7 + tt, :]
            + cw_ref[2:3, :] * uc)
    o_ref[...] = _rms(gb_ref[...] * conv, cg_ref[...]).astype(BF16)
    ext_ref[0:8, :] = ext_ref[tt:tt + 8, :]


def _conv_group(proj, conv_w, conv_g, *, batch, tt):
    T = proj.shape[0]
    W = conv_w.shape[1]
    nt = T // batch // tt
    row = lambda b, t: b * nt + t
    return pl.pallas_call(
        _conv_kernel,
        out_shape=jax.ShapeDtypeStruct((T, W), BF16),
        grid=(batch, nt),
        in_specs=[
            pl.BlockSpec((tt, W), lambda b, t: (row(b, t), 0)),
            pl.BlockSpec((tt, W), lambda b, t: (row(b, t), 1)),
            pl.BlockSpec((tt, W), lambda b, t: (row(b, t), 2)),
            pl.BlockSpec(conv_w.shape, lambda b, t: (0, 0)),
            pl.BlockSpec((1, W), lambda b, t: (0, 0)),
        ],
        out_specs=pl.BlockSpec((tt, W), lambda b, t: (row(b, t), 0)),
        scratch_shapes=[pltpu.VMEM((tt + 8, W), F32)],
        compiler_params=_cparams(("parallel", "arbitrary")),
        name="conv_group",
    )(proj, proj, proj, conv_w, conv_g)


PV_W0, PV_A0, PV_V0, PV_KK, PV_KA, PV_RK, PV_LNG, PV_LNB, PV_MUR, PV_MUK, PV_MUV, PV_MULO = range(12)
PV_ROWS = 16

_DIMS = {"nn": ((1,), (0,)), "nt": ((1,), (1,)), "tn": ((0,), (0,))}


def _split(x):
    hi = x.astype(BF16)
    return hi, (x - hi.astype(F32)).astype(BF16)


def _mm(x, w, form="nn", x_hp=False, w_hp=False):
    dot = lambda p, q: lax.dot_general(p, q, (_DIMS[form], ((), ())), preferred_element_type=F32)
    xh, xl = x if isinstance(x, tuple) else (_split(x) if x_hp else (x.astype(BF16), None))
    wh, wl = w if isinstance(w, tuple) else (_split(w) if w_hp else (w.astype(BF16), None))
    out = dot(xh, wh)
    if x_hp:
        out = out + dot(xl, wh)
    if w_hp:
        out = out + dot(xh, wl)
    return out


HP_SCORE = (False, False)
HP_INV = (False, False)
HP_APPLY = (False, False)
HP_STATE = (False, False)


def _wkv_kernel(has_vres, *refs):
    if has_vres:
        (r_ref, k_ref, v_ref, lo_ref, vf_ref, pv_ref, w2_ref, a2_ref, g2_ref, v2_ref,
         y_ref, state_ref, carry_ref) = refs
    else:
        (r_ref, k_ref, v_ref, lo_ref, pv_ref, w2_ref, a2_ref, g2_ref,
         y_ref, vf_out_ref, state_ref, carry_ref) = refs
    C, W = r_ref.shape
    G = MXU_TILE
    n_groups = W // G

    @pl.when(pl.program_id(1) == 0)
    def _():
        state_ref[...] = jnp.zeros_like(state_ref)
        carry_ref[...] = jnp.zeros_like(carry_ref)

    def pv(i):
        return pv_ref[i:i + 1, :]

    row = lax.broadcasted_iota(jnp.int32, (C, W), 0)

    def token_shift(x_ref, slot, mu):
        x = x_ref[...]
        prev = jnp.where(row == 0, carry_ref[slot:slot + 1, :], pltpu.roll(x, 1, 0))
        carry_ref[slot:slot + 1, :] = x[C - 1:C, :]
        return x + mu * (prev - x)

    r = token_shift(r_ref, 0, pv(PV_MUR))
    k = token_shift(k_ref, 1, pv(PV_MUK))
    v = token_shift(v_ref, 2, pv(PV_MUV))
    lo = token_shift(lo_ref, 3, pv(PV_MULO))

    wd = lo[:, LO_WD:LO_WD + 128]
    ad = lo[:, LO_AD:LO_AD + 128]
    gd = lo[:, LO_GD:LO_GD + LO_GD_W]
    if has_vres:
        vd = lo[:, LO_VD:LO_VD + 128]
        mix = jax.nn.sigmoid(pv(PV_V0) + _mm(vd, v2_ref[...]))
        v = v + (vf_ref[...] - v) * mix
    else:
        vf_out_ref[...] = v

    z = pv(PV_W0) + _mm(jnp.tanh(wd), w2_ref[...])
    wlog = -jax.nn.softplus(-z) - 0.5
    logw = -jnp.exp(wlog)
    a = jax.nn.sigmoid(pv(PV_A0) + _mm(ad, a2_ref[...]))
    g = _mm(jax.nn.sigmoid(gd), g2_ref[...])
    kk_raw = k * pv(PV_KK)
    k2 = k * (1.0 + (a - 1.0) * pv(PV_KA))

    gi = lax.broadcasted_iota(jnp.int32, (G, G), 0)
    gj = lax.broadcasted_iota(jnp.int32, (G, G), 1)
    same_head = (gi // HEAD_DIM) == (gj // HEAD_DIM)
    bd_mask = same_head.astype(F32).astype(BF16)
    ti = lax.broadcasted_iota(jnp.int32, (C, G), 0)
    si = lax.broadcasted_iota(jnp.int32, (C, G), 1) % HEAD_DIM
    incl_p, strict_p = ti >= si, ti > si
    eye_p = (ti == si).astype(F32)
    half_pair = []
    blk = 2
    while blk <= C:
        half_pair.append(((ti // blk) == (si // blk)) & ((ti // (blk // 2)) != (si // (blk // 2))))
        blk *= 2

    def bd(y):
        def expand(p):
            return jnp.concatenate([p] * GROUP_HEADS, axis=0) * bd_mask
        hi, lo = _split(y)
        return expand(hi), expand(lo)

    def group_cols(x):
        return [x[:, i * G:(i + 1) * G] for i in range(n_groups)]

    def head_sum(x):
        stacked = jnp.concatenate(group_cols(x), axis=0)
        s = _mm(stacked, bd_mask, x_hp=True)
        return jnp.concatenate([s[i * C:(i + 1) * C] for i in range(n_groups)], axis=1)

    kk = kk_raw / jnp.maximum(jnp.sqrt(head_sum(kk_raw * kk_raw)), 1e-12)
    bonus = head_sum(r * k2 * pv(PV_RK)) * v

    tc = lax.broadcasted_iota(jnp.int32, (C, C), 0)
    sc = lax.broadcasted_iota(jnp.int32, (C, C), 1)
    tri = (tc >= sc).astype(F32).astype(BF16)
    l_hi = logw.astype(BF16)
    l_rest = logw - l_hi.astype(F32)
    l_mid = l_rest.astype(BF16)
    l_lo = (l_rest - l_mid.astype(F32)).astype(BF16)
    cs = jnp.dot(jnp.concatenate([tri, tri, tri], axis=1),
                 jnp.concatenate([l_hi, l_mid, l_lo], axis=0), preferred_element_type=F32)
    cs_last = cs[C - 1:C, :]
    e_neg = jnp.exp(-cs)
    e_tail = jnp.exp(cs_last - cs)
    p_last = jnp.exp(cs_last)

    r_t = r * jnp.exp(cs)
    a_t = -kk * jnp.exp(cs - logw)
    b_raw = kk * a
    b_t = b_raw * e_neg
    k_t = k2 * e_neg
    b_p = b_raw * e_tail
    k_p = k2 * e_tail

    rg, ag, bg, kg, vg, bpg, kpg = map(group_cols, (r_t, a_t, b_t, k_t, v, b_p, k_p))
    groups = range(n_groups)
    stack = lambda p, q: jnp.concatenate([p, q], axis=0)
    lhs = [stack(ag[i], rg[i]) for i in groups]
    sc_b = [_mm(lhs[i], bd(bg[i]), "nt", *HP_SCORE) for i in groups]
    sc_k = [_mm(lhs[i], bd(kg[i]), "nt", *HP_SCORE) for i in groups]
    a_ab = [jnp.where(strict_p, sc_b[i][:C], 0.0) for i in groups]
    a_rb = [jnp.where(incl_p, sc_b[i][C:], 0.0) for i in groups]
    a_ak = [jnp.where(strict_p, sc_k[i][:C], 0.0) for i in groups]
    a_rk = [jnp.where(incl_p, sc_k[i][C:], 0.0) for i in groups]

    inv = [eye_p + jnp.where(half_pair[0], a_ab[i], 0.0) for i in groups]
    for lvl in range(1, len(half_pair)):
        n_inv = [_mm(jnp.where(half_pair[lvl], a_ab[i], 0.0), bd(inv[i]), "nn", *HP_INV)
                 for i in groups]
        inv = [inv[i] + _mm(inv[i], bd(n_inv[i]), "nn", *HP_INV) for i in groups]

    av_yv = [_mm(stack(a_ak[i], a_rk[i]), bd(vg[i]), "nn", *HP_APPLY) for i in groups]
    a_hat = [_mm(inv[i], bd(ag[i]), "nn", *HP_APPLY) for i in groups]
    u_v = [_mm(inv[i], bd(av_yv[i][:C]), "nn", *HP_APPLY) for i in groups]

    s0 = [state_ref[i] for i in groups]
    res = [_mm(stack(a_hat[i], rg[i]), s0[i], "nt", *HP_STATE) for i in groups]
    u = [res[i][:C] + u_v[i] for i in groups]
    upd = [_mm(stack(u[i], vg[i]), stack(bpg[i], kpg[i]), "tn", *HP_STATE) for i in groups]
    for i in groups:
        state_ref[i] = s0[i] * p_last[:, i * G:(i + 1) * G] + jnp.where(same_head, upd[i], 0.0)
    y_groups = [res[i][C:] + _mm(a_rb[i], bd(u[i]), "nn", *HP_APPLY) + av_yv[i][C:]
                for i in groups]

    y = jnp.concatenate(y_groups, axis=1)
    yc = y - head_sum(y) * (1.0 / HEAD_DIM)
    var = head_sum(yc * yc) * (1.0 / HEAD_DIM)
    yn = yc * lax.rsqrt(var + LNX_EPS) * pv(PV_LNG) + pv(PV_LNB)
    y_ref[...] = ((yn + bonus) * g).astype(BF16)


def _rwkv_group(proj, v_first, pvec, w2, a2, g2, v2, *, batch):
    T = proj.shape[0]
    W = pvec.shape[1]
    C = WKV_CHUNK
    nc = T // batch // C
    has_vres = v_first is not None
    row = lambda b, c: b * nc + c
    tok = lambda col: pl.BlockSpec((C, W), lambda b, c: (row(b, c), col))
    full = lambda arr: pl.BlockSpec(arr.shape, lambda b, c: (0, 0))
    in_specs = [tok(3), tok(4), tok(5), tok(6)]
    args = [proj, proj, proj, proj]
    if has_vres:
        in_specs.append(tok(0))
        args.append(v_first)
    in_specs += [full(pvec), full(w2), full(a2), full(g2)]
    args += [pvec, w2, a2, g2]
    if has_vres:
        in_specs.append(full(v2))
        args.append(v2)
        out_shape = jax.ShapeDtypeStruct((T, W), BF16)
        out_specs = tok(0)
    else:
        out_shape = (jax.ShapeDtypeStruct((T, W), BF16), jax.ShapeDtypeStruct((T, W), F32))
        out_specs = (tok(0), tok(0))
    return pl.pallas_call(
        functools.partial(_wkv_kernel, has_vres),
        out_shape=out_shape,
        grid=(batch, nc),
        in_specs=in_specs,
        out_specs=out_specs,
        scratch_shapes=[
            pltpu.VMEM((W // MXU_TILE, MXU_TILE, MXU_TILE), F32),
            pltpu.VMEM((8, W), F32),
        ],
        compiler_params=_cparams(("parallel", "arbitrary")),
        name="rwkv_group",
    )(*args)


def _proj_out_kernel(yc_ref, yr_ref, w_ref, g_ref, h_ref, o_ref):
    half = yc_ref.shape[1]
    y = (jnp.dot(yc_ref[...], w_ref[0:half, :], preferred_element_type=F32)
         + jnp.dot(yr_ref[...], w_ref[half:2 * half, :], preferred_element_type=F32))
    o_ref[...] = h_ref[...] + _rms(y, g_ref[...])


def _proj_out(y_conv, y_rwkv, w_out, g, h, *, tm):
    T, D = h.shape
    half = y_conv.shape[1]
    return pl.pallas_call(
        _proj_out_kernel,
        out_shape=jax.ShapeDtypeStruct((T, D), F32),
        grid=(T // tm,),
        in_specs=[
            pl.BlockSpec((tm, half), lambda i: (i, 0)),
            pl.BlockSpec((tm, half), lambda i: (i, 0)),
            pl.BlockSpec(w_out.shape, lambda i: (0, 0)),
            pl.BlockSpec((1, D), lambda i: (0, 0)),
            pl.BlockSpec((tm, D), lambda i: (i, 0)),
        ],
        out_specs=pl.BlockSpec((tm, D), lambda i: (i, 0)),
        compiler_params=_cparams(("parallel",)),
        name="proj_out",
    )(y_conv, y_rwkv, w_out, g, h)


def _pad_rows(w, rows):
    return jnp.pad(w, ((0, rows - w.shape[0]), (0, 0)))


def _lora_cols(wd, ad, gd, vd):
    n = wd.shape[0]
    out = jnp.zeros((n, LO_W), wd.dtype)
    out = out.at[:, LO_WD:LO_WD + R_DECAY].set(wd)
    out = out.at[:, LO_AD:LO_AD + R_ICLR].set(ad)
    out = out.at[:, LO_GD:LO_GD + R_GATE].set(gd)
    if vd is not None:
        out = out.at[:, LO_VD:LO_VD + R_VRES].set(vd)
    return out


def kernel(x, meta_tokens, ffn1_pre_g, ffn1_w_gu, ffn1_w_down, ffn1_post_g, mix_pre_g, w_in, w_in_vres, mu_rwkv, mu_vres, conv_w, conv_norm_g, decay_w0, decay_w2, iclr_a0, iclr_a2, vres_v0, vres_v2, gate_g2, k_k, k_a, r_k, lnx_g, lnx_b, w_out, mix_post_g, ffn2_pre_g, ffn2_w_gu, ffn2_w_down, ffn2_post_g):
    B, S, D = x.shape
    depth = w_in.shape[0]
    conv_width = conv_w.shape[2]
    rw = decay_w0.shape[1]
    L = N_META + S
    Lp = -(-L // WKV_CHUNK) * WKV_CHUNK
    T = B * Lp
    tm = 640 if T % 640 == 0 else WKV_CHUNK
    tf = 512 if ffn1_w_down.shape[1] % 512 == 0 else ffn1_w_down.shape[1]
    tn = 1024
    p_conv = 3 * conv_width

    meta = jnp.broadcast_to(meta_tokens.astype(x.dtype)[None], (B, N_META, D))
    h = jnp.concatenate([meta, x, jnp.zeros((B, Lp - L, D), x.dtype)], axis=1).reshape(T, D)

    row = lambda a: a.reshape(1, -1)
    v_first = None
    for i in range(depth):
        h = _ffn(h, row(ffn1_pre_g[i]), ffn1_w_gu[i].astype(BF16), ffn1_w_down[i].astype(BF16),
                 row(ffn1_post_g[i]), tm=tm, tf=tf)

        wi = w_in[i]
        o = p_conv + 3 * rw
        lo_w = _lora_cols(wi[:, o:o + R_DECAY], wi[:, o + R_DECAY:o + R_DECAY + R_ICLR],
                          wi[:, o + R_DECAY + R_ICLR:], w_in_vres[i - 1] if i > 0 else None)
        w_cat = jnp.concatenate([wi[:, :o], lo_w], axis=1).astype(BF16)
        mu = mu_rwkv[i]
        mu_lo = _lora_cols(row(mu[3 * rw:3 * rw + R_DECAY]),
                           row(mu[3 * rw + R_DECAY:3 * rw + R_DECAY + R_ICLR]),
                           row(mu[3 * rw + R_DECAY + R_ICLR:]),
                           row(mu_vres[i - 1]) if i > 0 else None)
        zero = jnp.zeros((1, rw), F32)
        pvec = jnp.concatenate([
            row(decay_w0[i]), row(iclr_a0[i]), row(vres_v0[i - 1]) if i > 0 else zero,
            row(k_k[i]), row(k_a[i]), row(r_k[i]), row(lnx_g[i]), row(lnx_b[i]),
            row(mu[:rw]), row(mu[rw:2 * rw]), row(mu[2 * rw:3 * rw]), mu_lo,
            jnp.zeros((PV_ROWS - 12, rw), F32)], axis=0)

        proj = _proj_in(h, row(mix_pre_g[i]), w_cat, tm=tm, tn=tn)
        y_conv = _conv_group(proj, conv_w[i], row(conv_norm_g[i]), batch=B, tt=Lp // 5 if Lp % 40 == 0 else WKV_CHUNK)
        rw_args = (_pad_rows(decay_w2[i], 128).astype(BF16), _pad_rows(iclr_a2[i], 128).astype(BF16),
                   _pad_rows(gate_g2[i], LO_GD_W).astype(BF16),
                   _pad_rows(vres_v2[i - 1], 128).astype(BF16) if i > 0 else None)
        if i == 0:
            y_rwkv, v_first = _rwkv_group(proj, None, pvec, *rw_args, batch=B)
        else:
            y_rwkv = _rwkv_group(proj, v_first, pvec, *rw_args, batch=B)
        h = _proj_out(y_conv, y_rwkv, w_out[i].astype(BF16), row(mix_post_g[i]), h, tm=tm)

        h = _ffn(h, row(ffn2_pre_g[i]), ffn2_w_gu[i].astype(BF16), ffn2_w_down[i].astype(BF16),
                 row(ffn2_post_g[i]), tm=tm, tf=tf)
    return h.reshape(B, Lp, D)[:, N_META:L]
```

```python
import functools

import jax
import jax.numpy as jnp
from jax import lax
from jax.experimental import pallas as pl
from jax.experimental.pallas import tpu as pltpu

F32 = jnp.float32
BF16 = jnp.bfloat16

NORM_EPS = 1e-6
LNX_EPS = 64e-5
N_META = 16
HEAD_DIM = 64
WKV_CHUNK = 64
R_DECAY, R_ICLR, R_GATE, R_VRES = 64, 64, 160, 32
LO_WD, LO_AD, LO_GD, LO_VD = 0, 128, 256, 512
LO_GD_W = 256
LO_W = 1024
MXU_TILE = 256
GROUP_HEADS = MXU_TILE // HEAD_DIM
SUBLANES = 8

VMEM_LIMIT_BYTES = 56 * 1024 * 1024


def _cparams(sem):
    return pltpu.CompilerParams(dimension_semantics=sem, vmem_limit_bytes=VMEM_LIMIT_BYTES)


def _rms(x, g):
    ms = jnp.mean(x * x, axis=-1, keepdims=True)
    return x * lax.rsqrt(ms + NORM_EPS) * g


def _layer_vec(layer, width):
    return pl.BlockSpec((None, 1, width), lambda *_: (layer, 0, 0))


def _ffn_kernel(h_ref, pre_g_ref, wg_ref, wu_ref, wd_ref, post_g_ref, o_ref, xn_ref, acc_ref):
    j = pl.program_id(1)

    @pl.when(j == 0)
    def _():
        xn_ref[...] = _rms(h_ref[...], pre_g_ref[...]).astype(BF16)
        acc_ref[...] = jnp.zeros_like(acc_ref)

    xn = xn_ref[...]
    gate = jnp.dot(xn, wg_ref[...], preferred_element_type=F32)
    up = jnp.dot(xn, wu_ref[...], preferred_element_type=F32)
    act = (gate * jax.nn.sigmoid(gate) * up).astype(BF16)
    acc_ref[...] += jnp.dot(act, wd_ref[...], preferred_element_type=F32)

    @pl.when(j == pl.num_programs(1) - 1)
    def _():
        o_ref[...] = h_ref[...] + 0.5 * _rms(acc_ref[...], post_g_ref[...])


def _ffn(h, pre_g, w_gu, w_down, post_g, layer, *, tm, tf):
    T, D = h.shape
    FF = w_down.shape[1]
    nj = FF // tf
    return pl.pallas_call(
        _ffn_kernel,
        out_shape=jax.ShapeDtypeStruct((T, D), F32),
        grid=(T // tm, nj),
        in_specs=[
            pl.BlockSpec((tm, D), lambda i, j: (i, 0)),
            _layer_vec(layer, D),
            pl.BlockSpec((None, D, tf), lambda i, j: (layer, 0, j)),
            pl.BlockSpec((None, D, tf), lambda i, j: (layer, 0, j + nj)),
            pl.BlockSpec((None, tf, D), lambda i, j: (layer, j, 0)),
            _layer_vec(layer, D),
        ],
        out_specs=pl.BlockSpec((tm, D), lambda i, j: (i, 0)),
        scratch_shapes=[pltpu.VMEM((tm, D), BF16), pltpu.VMEM((tm, D), F32)],
        compiler_params=_cparams(("parallel", "arbitrary")),
        name="ffn",
    )(h, pre_g, w_gu, w_gu, w_down, post_g)


def _proj_in_kernel(h_ref, g_ref, w_ref, o_ref, xn_ref):
    @pl.when(pl.program_id(1) == 0)
    def _():
        xn_ref[...] = _rms(h_ref[...], g_ref[...]).astype(BF16)

    o_ref[...] = jnp.dot(xn_ref[...], w_ref[...], preferred_element_type=F32)


def _proj_in(h, g, w_cat, layer, *, tm, tn):
    T, D = h.shape
    P = w_cat.shape[2]
    return pl.pallas_call(
        _proj_in_kernel,
        out_shape=jax.ShapeDtypeStruct((T, P), F32),
        grid=(T // tm, P // tn),
        in_specs=[
            pl.BlockSpec((tm, D), lambda i, j: (i, 0)),
            _layer_vec(layer, D),
            pl.BlockSpec((None, D, tn), lambda i, j: (layer, 0, j)),
        ],
        out_specs=pl.BlockSpec((tm, tn), lambda i, j: (i, j)),
        scratch_shapes=[pltpu.VMEM((tm, D), BF16)],
        compiler_params=_cparams(("parallel", "arbitrary")),
        name="proj_in",
    )(h, g, w_cat)


def _conv_kernel(gb_ref, gc_ref, u_ref, cw_ref, cg_ref, o_ref, ext_ref):
    tt = u_ref.shape[0]

    @pl.when(pl.program_id(1) == 0)
    def _():
        ext_ref[0:8, :] = jnp.zeros((8, ext_ref.shape[1]), F32)

    uc = gc_ref[...] * u_ref[...]
    ext_ref[8:8 + tt, :] = uc
    conv = (cw_ref[0:1, :] * ext_ref[6:6 + tt, :] + cw_ref[1:2, :] * ext_ref[7:7 + tt, :]
            + cw_ref[2:3, :] * uc)
    o_ref[...] = _rms(gb_ref[...] * conv, cg_ref[...]).astype(BF16)
    ext_ref[0:8, :] = ext_ref[tt:tt + 8, :]


def _conv_group(proj, conv_w, conv_g, layer, *, batch, tt):
    T = proj.shape[0]
    K, W = conv_w.shape[1:]
    nt = T // batch // tt
    row = lambda b, t: b * nt + t
    return pl.pallas_call(
        _conv_kernel,
        out_shape=jax.ShapeDtypeStruct((T, W), BF16),
        grid=(batch, nt),
        in_specs=[
            pl.BlockSpec((tt, W), lambda b, t: (row(b, t), 0)),
            pl.BlockSpec((tt, W), lambda b, t: (row(b, t), 1)),
            pl.BlockSpec((tt, W), lambda b, t: (row(b, t), 2)),
            pl.BlockSpec((None, K, W), lambda b, t: (layer, 0, 0)),
            _layer_vec(layer, W),
        ],
        out_specs=pl.BlockSpec((tt, W), lambda b, t: (row(b, t), 0)),
        scratch_shapes=[pltpu.VMEM((tt + 8, W), F32)],
        compiler_params=_cparams(("parallel", "arbitrary")),
        name="conv_group",
    )(proj, proj, proj, conv_w, conv_g)


PV_W0, PV_A0, PV_V0, PV_KK, PV_KA, PV_RK, PV_LNG, PV_LNB, PV_MUR, PV_MUK, PV_MUV, PV_MULO = range(12)
PV_ROWS = 16

_DIMS = {"nn": ((1,), (0,)), "nt": ((1,), (1,)), "tn": ((0,), (0,))}


def _split(x):
    hi = x.astype(BF16)
    return hi, (x - hi.astype(F32)).astype(BF16)


def _mm(x, w, form="nn", x_hp=False):
    dot = lambda p, q: lax.dot_general(p, q, (_DIMS[form], ((), ())), preferred_element_type=F32)
    w = w.astype(BF16)
    if not x_hp:
        return dot(x.astype(BF16), w)
    hi, lo = _split(x)
    return dot(hi, w) + dot(lo, w)


def _wkv_kernel(has_vres, *refs):
    if has_vres:
        (r_ref, k_ref, v_ref, lo_ref, vf_ref, pv_ref, w2_ref, a2_ref, g2_ref, v2_ref,
         y_ref, state_ref, carry_ref) = refs
    else:
        (r_ref, k_ref, v_ref, lo_ref, pv_ref, w2_ref, a2_ref, g2_ref,
         y_ref, vf_out_ref, state_ref, carry_ref) = refs
    NB, C, W = r_ref.shape
    R = NB * C
    G = MXU_TILE
    n_groups = W // G

    @pl.when(pl.program_id(1) == 0)
    def _():
        state_ref[...] = jnp.zeros_like(state_ref)
        carry_ref[...] = jnp.zeros_like(carry_ref)

    def pv(i):
        return pv_ref[i:i + 1, :]

    row = lax.broadcasted_iota(jnp.int32, (R, W), 0)

    def token_shift(x_ref, slot, mu):
        x = x_ref[...].reshape(R, W)
        prev = pltpu.roll(x, 1, 0)
        for b in range(NB):
            crow = b * SUBLANES + slot
            prev = jnp.where(row == b * C, carry_ref[crow:crow + 1, :], prev)
            carry_ref[crow:crow + 1, :] = x[(b + 1) * C - 1:(b + 1) * C, :]
        return x + mu * (prev - x)

    r = token_shift(r_ref, 0, pv(PV_MUR))
    k = token_shift(k_ref, 1, pv(PV_MUK))
    v = token_shift(v_ref, 2, pv(PV_MUV))
    lo = token_shift(lo_ref, 3, pv(PV_MULO))

    wd = lo[:, LO_WD:LO_WD + 128]
    ad = lo[:, LO_AD:LO_AD + 128]
    gd = lo[:, LO_GD:LO_GD + LO_GD_W]
    if has_vres:
        vd = lo[:, LO_VD:LO_VD + 128]
        mix = jax.nn.sigmoid(pv(PV_V0) + _mm(vd, v2_ref[...]))
        v = v + (vf_ref[...].reshape(R, W) - v) * mix
    else:
        vf_out_ref[...] = v.reshape(NB, C, W)

    z = pv(PV_W0) + _mm(jnp.tanh(wd), w2_ref[...])
    wlog = -jax.nn.softplus(-z) - 0.5
    logw = -jnp.exp(wlog)
    a = jax.nn.sigmoid(pv(PV_A0) + _mm(ad, a2_ref[...]))
    g = _mm(jax.nn.sigmoid(gd), g2_ref[...])
    kk_raw = k * pv(PV_KK)
    k2 = k * (1.0 + (a - 1.0) * pv(PV_KA))

    gi = lax.broadcasted_iota(jnp.int32, (G, G), 0)
    gj = lax.broadcasted_iota(jnp.int32, (G, G), 1)
    same_head = (gi // HEAD_DIM) == (gj // HEAD_DIM)
    bd_mask = same_head.astype(F32).astype(BF16)
    ti = lax.broadcasted_iota(jnp.int32, (C, G), 0)
    si = lax.broadcasted_iota(jnp.int32, (C, G), 1) % HEAD_DIM
    incl_p, strict_p = ti >= si, ti > si
    eye_p = (ti == si).astype(F32)
    half_pair = []
    blk = 2
    while blk <= C:
        half_pair.append(((ti // blk) == (si // blk)) & ((ti // (blk // 2)) != (si // (blk // 2))))
        blk *= 2

    def bd(y):
        return jnp.concatenate([y.astype(BF16)] * GROUP_HEADS, axis=0) * bd_mask

    def group_cols(x):
        return [x[:, i * G:(i + 1) * G] for i in range(n_groups)]

    def head_sum(x):
        s = _mm(jnp.concatenate(group_cols(x), axis=0), bd_mask, x_hp=True)
        return jnp.concatenate([s[i * R:(i + 1) * R] for i in range(n_groups)], axis=1)

    kk = kk_raw / jnp.maximum(jnp.sqrt(head_sum(kk_raw * kk_raw)), 1e-12)
    bonus = head_sum(r * k2 * pv(PV_RK)) * v

    tc = lax.broadcasted_iota(jnp.int32, (C, C), 0)
    sc = lax.broadcasted_iota(jnp.int32, (C, C), 1)
    tri = (tc >= sc).astype(F32).astype(BF16)
    tri3 = jnp.concatenate([tri, tri, tri], axis=1)
    l_hi = logw.astype(BF16)
    l_rest = logw - l_hi.astype(F32)
    l_mid = l_rest.astype(BF16)
    l_lo = (l_rest - l_mid.astype(F32)).astype(BF16)
    seq_rows = lambda x, b: x[b * C:(b + 1) * C]
    cs_b = [jnp.dot(tri3, jnp.concatenate([seq_rows(l_hi, b), seq_rows(l_mid, b), seq_rows(l_lo, b)],
                                          axis=0), preferred_element_type=F32) for b in range(NB)]
    cs = jnp.concatenate(cs_b, axis=0)
    cs_last = [c[C - 1:C, :] for c in cs_b]
    cs_end = jnp.concatenate([jnp.broadcast_to(c, (C, W)) for c in cs_last], axis=0)
    e_neg = jnp.exp(-cs)
    e_tail = jnp.exp(cs_end - cs)
    p_last = [jnp.exp(c) for c in cs_last]

    r_t = r * jnp.exp(cs)
    a_t = -kk * jnp.exp(cs - logw)
    b_raw = kk * a
    b_t = b_raw * e_neg
    k_t = k2 * e_neg
    b_p = b_raw * e_tail
    k_p = k2 * e_tail

    def units_of(x):
        return [x[b * C:(b + 1) * C, i * G:(i + 1) * G] for b in range(NB) for i in range(n_groups)]

    rg, ag, bg, kg, vg, bpg, kpg = map(units_of, (r_t, a_t, b_t, k_t, v, b_p, k_p))
    units = range(NB * n_groups)
    stack = lambda p, q: jnp.concatenate([p, q], axis=0)
    lhs = [stack(ag[i], rg[i]) for i in units]
    sc_b = [_mm(lhs[i], bd(bg[i]), "nt") for i in units]
    sc_k = [_mm(lhs[i], bd(kg[i]), "nt") for i in units]
    a_ab = [jnp.where(strict_p, sc_b[i][:C], 0.0) for i in units]
    a_rb = [jnp.where(incl_p, sc_b[i][C:], 0.0) for i in units]
    a_ak = [jnp.where(strict_p, sc_k[i][:C], 0.0) for i in units]
    a_rk = [jnp.where(incl_p, sc_k[i][C:], 0.0) for i in units]

    inv = [eye_p + jnp.where(half_pair[0], a_ab[i], 0.0) for i in units]
    for lvl in range(1, len(half_pair)):
        n_inv = [_mm(jnp.where(half_pair[lvl], a_ab[i], 0.0), bd(inv[i])) for i in units]
        inv = [inv[i] + _mm(inv[i], bd(n_inv[i])) for i in units]

    av_yv = [_mm(stack(a_ak[i], a_rk[i]), bd(vg[i])) for i in units]
    a_hat = [_mm(inv[i], bd(ag[i])) for i in units]
    u_v = [_mm(inv[i], bd(av_yv[i][:C])) for i in units]

    s0 = [state_ref[i] for i in units]
    res = [_mm(stack(a_hat[i], rg[i]), s0[i], "nt") for i in units]
    u = [res[i][:C] + u_v[i] for i in units]
    upd = [_mm(stack(u[i], vg[i]), stack(bpg[i], kpg[i]), "tn") for i in units]
    for i in units:
        b, gidx = divmod(i, n_groups)
        state_ref[i] = (s0[i] * p_last[b][:, gidx * G:(gidx + 1) * G]
                        + jnp.where(same_head, upd[i], 0.0))
    y_units = [res[i][C:] + _mm(a_rb[i], bd(u[i])) + av_yv[i][C:] for i in units]

    y = jnp.concatenate([jnp.concatenate(y_units[b * n_groups:(b + 1) * n_groups], axis=1)
                         for b in range(NB)], axis=0)
    yc = y - head_sum(y) * (1.0 / HEAD_DIM)
    var = head_sum(yc * yc) * (1.0 / HEAD_DIM)
    yn = yc * lax.rsqrt(var + LNX_EPS) * pv(PV_LNG) + pv(PV_LNB)
    y_ref[...] = ((yn + bonus) * g).astype(BF16).reshape(NB, C, W)


def _rwkv_group(proj, v_first, pvec, w2, a2, g2, v2, layer, *, batch, nb):
    Lp = proj.shape[1]
    W = pvec.shape[2]
    C = WKV_CHUNK
    has_vres = v_first is not None
    tok = lambda col: pl.BlockSpec((nb, C, W), lambda bp, c: (bp, c, col))
    par = lambda arr: pl.BlockSpec((None,) + arr.shape[1:], lambda bp, c: (layer, 0, 0))
    in_specs = [tok(3), tok(4), tok(5), tok(6)]
    args = [proj, proj, proj, proj]
    if has_vres:
        in_specs.append(tok(0))
        args.append(v_first)
    in_specs += [par(pvec), par(w2), par(a2), par(g2)]
    args += [pvec, w2, a2, g2]
    if has_vres:
        in_specs.append(par(v2))
        args.append(v2)
        out_shape = jax.ShapeDtypeStruct((batch, Lp, W), BF16)
        out_specs = tok(0)
    else:
        out_shape = (jax.ShapeDtypeStruct((batch, Lp, W), BF16),
                     jax.ShapeDtypeStruct((batch, Lp, W), F32))
        out_specs = (tok(0), tok(0))
    return pl.pallas_call(
        functools.partial(_wkv_kernel, has_vres),
        out_shape=out_shape,
        grid=(batch // nb, Lp // C),
        in_specs=in_specs,
        out_specs=out_specs,
        scratch_shapes=[
            pltpu.VMEM((nb * (W // MXU_TILE), MXU_TILE, MXU_TILE), F32),
            pltpu.VMEM((nb * SUBLANES, W), F32),
        ],
        compiler_params=_cparams(("parallel", "arbitrary")),
        name="rwkv_group",
    )(*args)


def _proj_out_kernel(yc_ref, yr_ref, w_ref, g_ref, h_ref, o_ref):
    half = yc_ref.shape[1]
    y = (jnp.dot(yc_ref[...], w_ref[0:half, :], preferred_element_type=F32)
         + jnp.dot(yr_ref[...], w_ref[half:2 * half, :], preferred_element_type=F32))
    o_ref[...] = h_ref[...] + _rms(y, g_ref[...])


def _proj_out(y_conv, y_rwkv, w_out, g, h, layer, *, tm):
    T, D = h.shape
    half = y_conv.shape[1]
    return pl.pallas_call(
        _proj_out_kernel,
        out_shape=jax.ShapeDtypeStruct((T, D), F32),
        grid=(T // tm,),
        in_specs=[
            pl.BlockSpec((tm, half), lambda i: (i, 0)),
            pl.BlockSpec((tm, half), lambda i: (i, 0)),
            pl.BlockSpec((None,) + w_out.shape[1:], lambda i: (layer, 0, 0)),
            _layer_vec(layer, D),
            pl.BlockSpec((tm, D), lambda i: (i, 0)),
        ],
        out_specs=pl.BlockSpec((tm, D), lambda i: (i, 0)),
        compiler_params=_cparams(("parallel",)),
        name="proj_out",
    )(y_conv, y_rwkv, w_out, g, h)


def _pad_axis(w, axis, size):
    pad = [(0, 0)] * w.ndim
    pad[axis] = (0, size - w.shape[axis])
    return jnp.pad(w, pad)


def _lora_cols(wd, ad, gd, vd):
    out = jnp.zeros(wd.shape[:-1] + (LO_W,), wd.dtype)
    out = out.at[..., LO_WD:LO_WD + R_DECAY].set(wd)
    out = out.at[..., LO_AD:LO_AD + R_ICLR].set(ad)
    out = out.at[..., LO_GD:LO_GD + R_GATE].set(gd)
    return out.at[..., LO_VD:LO_VD + R_VRES].set(vd)


def _first_layer_zero(w):
    return jnp.concatenate([jnp.zeros_like(w[:1]), w], axis=0)


def kernel(x, meta_tokens, ffn1_pre_g, ffn1_w_gu, ffn1_w_down, ffn1_post_g, mix_pre_g, w_in, w_in_vres, mu_rwkv, mu_vres, conv_w, conv_norm_g, decay_w0, decay_w2, iclr_a0, iclr_a2, vres_v0, vres_v2, gate_g2, k_k, k_a, r_k, lnx_g, lnx_b, w_out, mix_post_g, ffn2_pre_g, ffn2_w_gu, ffn2_w_down, ffn2_post_g):
    B, S, D = x.shape
    depth = w_in.shape[0]
    conv_width = conv_w.shape[2]
    rw = decay_w0.shape[1]
    L = N_META + S
    Lp = -(-L // WKV_CHUNK) * WKV_CHUNK
    T = B * Lp
    pick = lambda n, cands: next(c for c in cands if n % c == 0)
    tm = pick(T, (640, 512, 256, WKV_CHUNK))
    tm_proj = pick(T, (1280, 640, 512, 256, WKV_CHUNK))
    tf = pick(ffn1_w_down.shape[1], (512, 256, 128))
    tt_conv = pick(Lp, (832, 512, 256, WKV_CHUNK))
    nb = pick(B, (2, 1))
    tn = 1024
    o = 3 * conv_width + 3 * rw

    meta = jnp.broadcast_to(meta_tokens.astype(x.dtype)[None], (B, N_META, D))
    h = jnp.concatenate([meta, x, jnp.zeros((B, Lp - L, D), x.dtype)], axis=1).reshape(T, D)

    vec = lambda a: a.reshape(depth, 1, -1)
    ffn1 = (vec(ffn1_pre_g), ffn1_w_gu.astype(BF16), ffn1_w_down.astype(BF16), vec(ffn1_post_g))
    ffn2 = (vec(ffn2_pre_g), ffn2_w_gu.astype(BF16), ffn2_w_down.astype(BF16), vec(ffn2_post_g))
    lo_w = _lora_cols(w_in[..., o:o + R_DECAY], w_in[..., o + R_DECAY:o + R_DECAY + R_ICLR],
                      w_in[..., o + R_DECAY + R_ICLR:], _first_layer_zero(w_in_vres))
    w_cat = jnp.concatenate([w_in[..., :o], lo_w], axis=-1).astype(BF16)
    m = 3 * rw
    mu_lo = _lora_cols(mu_rwkv[:, m:m + R_DECAY], mu_rwkv[:, m + R_DECAY:m + R_DECAY + R_ICLR],
                       mu_rwkv[:, m + R_DECAY + R_ICLR:], _first_layer_zero(mu_vres))
    pvec = jnp.stack([decay_w0, iclr_a0, _first_layer_zero(vres_v0), k_k, k_a, r_k.reshape(depth, rw),
                      lnx_g, lnx_b, mu_rwkv[:, :rw], mu_rwkv[:, rw:2 * rw], mu_rwkv[:, 2 * rw:m], mu_lo]
                     + [jnp.zeros((depth, rw), F32)] * (PV_ROWS - 12), axis=1)
    w2 = _pad_axis(decay_w2, 1, 128).astype(BF16)
    a2 = _pad_axis(iclr_a2, 1, 128).astype(BF16)
    g2 = _pad_axis(gate_g2, 1, LO_GD_W).astype(BF16)
    v2 = _pad_axis(_first_layer_zero(vres_v2), 1, 128).astype(BF16)
    w_out_b = w_out.astype(BF16)
    mix_pre, mix_post, conv_g = vec(mix_pre_g), vec(mix_post_g), vec(conv_norm_g)

    v_first = None
    for i in range(depth):
        h = _ffn(h, *ffn1, i, tm=tm, tf=tf)
        proj = _proj_in(h, mix_pre, w_cat, i, tm=tm_proj, tn=tn)
        y_conv = _conv_group(proj, conv_w, conv_g, i, batch=B, tt=tt_conv)
        proj3 = proj.reshape(B, Lp, proj.shape[1])
        if i == 0:
            y_rwkv, v_first = _rwkv_group(proj3, None, pvec, w2, a2, g2, None, i, batch=B, nb=nb)
        else:
            y_rwkv = _rwkv_group(proj3, v_first, pvec, w2, a2, g2, v2, i, batch=B, nb=nb)
        h = _proj_out(y_conv, y_rwkv.reshape(T, rw), w_out_b, mix_post, h, i, tm=tm)
        h = _ffn(h, *ffn2, i, tm=tm, tf=tf)
    return h.reshape(B, Lp, D)[:, N_META:L]
```

```python
import functools
import math

import jax
import jax.numpy as jnp
from jax import lax
from jax.experimental import pallas as pl
from jax.experimental.pallas import tpu as pltpu

F32 = jnp.float32
BF16 = jnp.bfloat16

NORM_EPS = 1e-6
LNX_EPS = 64e-5
N_META = 16
HEAD_DIM = 64
WKV_CHUNK = 64
R_DECAY, R_ICLR, R_GATE, R_VRES = 64, 64, 160, 32
LO_WD, LO_AD, LO_GD, LO_VD = 0, 64, 128, 384
LO_GD_W = 256
LO_VD_W = 128
LO_W = 512
LANES = 128
GROUP_HEADS = LANES // HEAD_DIM
SUBLANES = 8

VMEM_LIMIT_BYTES = 56 * 1024 * 1024


def _cparams(sem):
    return pltpu.CompilerParams(dimension_semantics=sem, vmem_limit_bytes=VMEM_LIMIT_BYTES)


def _rms(x, g):
    ms = jnp.mean(x * x, axis=-1, keepdims=True)
    return x * lax.rsqrt(ms + NORM_EPS) * g


def _layer_vec(layer, width):
    return pl.BlockSpec((None, 1, width), lambda *_: (layer, 0, 0))


def _ffn_kernel(h_ref, pre_g_ref, wg_ref, wu_ref, wd_ref, post_g_ref, o_ref, xn_ref, acc_ref):
    j = pl.program_id(1)

    @pl.when(j == 0)
    def _():
        xn_ref[...] = _rms(h_ref[...], pre_g_ref[...]).astype(BF16)
        acc_ref[...] = jnp.zeros_like(acc_ref)

    xn = xn_ref[...]
    gate = jnp.dot(xn, wg_ref[...], preferred_element_type=F32)
    up = jnp.dot(xn, wu_ref[...], preferred_element_type=F32)
    act = (gate * jax.nn.sigmoid(gate) * up).astype(BF16)
    acc_ref[...] += jnp.dot(act, wd_ref[...], preferred_element_type=F32)

    @pl.when(j == pl.num_programs(1) - 1)
    def _():
        o_ref[...] = h_ref[...] + 0.5 * _rms(acc_ref[...], post_g_ref[...])


def _ffn(h, pre_g, w_gu, w_down, post_g, layer, *, tm, tf):
    T, D = h.shape
    FF = w_down.shape[1]
    nj = FF // tf
    return pl.pallas_call(
        _ffn_kernel,
        out_shape=jax.ShapeDtypeStruct((T, D), F32),
        grid=(T // tm, nj),
        in_specs=[
            pl.BlockSpec((tm, D), lambda i, j: (i, 0)),
            _layer_vec(layer, D),
            pl.BlockSpec((None, D, tf), lambda i, j: (layer, 0, j)),
            pl.BlockSpec((None, D, tf), lambda i, j: (layer, 0, j + nj)),
            pl.BlockSpec((None, tf, D), lambda i, j: (layer, j, 0)),
            _layer_vec(layer, D),
        ],
        out_specs=pl.BlockSpec((tm, D), lambda i, j: (i, 0)),
        scratch_shapes=[pltpu.VMEM((tm, D), BF16), pltpu.VMEM((tm, D), F32)],
        compiler_params=_cparams(("parallel", "arbitrary")),
        name="ffn",
    )(h, pre_g, w_gu, w_gu, w_down, post_g)


def _proj_in_kernel(h_ref, g_ref, w_ref, o_ref, xn_ref):
    @pl.when(pl.program_id(1) == 0)
    def _():
        xn_ref[...] = _rms(h_ref[...], g_ref[...]).astype(BF16)

    o_ref[...] = jnp.dot(xn_ref[...], w_ref[...], preferred_element_type=F32)


def _proj_in(h, g, w_cat, layer, *, tm, tn):
    T, D = h.shape
    P = w_cat.shape[2]
    return pl.pallas_call(
        _proj_in_kernel,
        out_shape=jax.ShapeDtypeStruct((T, P), F32),
        grid=(T // tm, P // tn),
        in_specs=[
            pl.BlockSpec((tm, D), lambda i, j: (i, 0)),
            _layer_vec(layer, D),
            pl.BlockSpec((None, D, tn), lambda i, j: (layer, 0, j)),
        ],
        out_specs=pl.BlockSpec((tm, tn), lambda i, j: (i, j)),
        scratch_shapes=[pltpu.VMEM((tm, D), BF16)],
        compiler_params=_cparams(("parallel", "arbitrary")),
        name="proj_in",
    )(h, g, w_cat)


def _conv_kernel(gb_ref, gc_ref, u_ref, cw_ref, cg_ref, o_ref, ext_ref):
    tt = u_ref.shape[0]

    @pl.when(pl.program_id(1) == 0)
    def _():
        ext_ref[0:8, :] = jnp.zeros((8, ext_ref.shape[1]), F32)

    uc = gc_ref[...] * u_ref[...]
    ext_ref[8:8 + tt, :] = uc
    conv = (cw_ref[0:1, :] * ext_ref[6:6 + tt, :] + cw_ref[1:2, :] * ext_ref[7:7 + tt, :]
            + cw_ref[2:3, :] * uc)
    o_ref[...] = _rms(gb_ref[...] * conv, cg_ref[...]).astype(BF16)
    ext_ref[0:8, :] = ext_ref[tt:tt + 8, :]


def _conv_group(proj, conv_w, conv_g, layer, *, batch, tt):
    T = proj.shape[0]
    K, W = conv_w.shape[1:]
    nt = T // batch // tt
    row = lambda b, t: b * nt + t
    return pl.pallas_call(
        _conv_kernel,
        out_shape=jax.ShapeDtypeStruct((T, W), BF16),
        grid=(batch, nt),
        in_specs=[
            pl.BlockSpec((tt, W), lambda b, t: (row(b, t), 0)),
            pl.BlockSpec((tt, W), lambda b, t: (row(b, t), 1)),
            pl.BlockSpec((tt, W), lambda b, t: (row(b, t), 2)),
            pl.BlockSpec((None, K, W), lambda b, t: (layer, 0, 0)),
            _layer_vec(layer, W),
        ],
        out_specs=pl.BlockSpec((tt, W), lambda b, t: (row(b, t), 0)),
        scratch_shapes=[pltpu.VMEM((tt + 8, W), F32)],
        compiler_params=_cparams(("parallel", "arbitrary")),
        name="conv_group",
    )(proj, proj, proj, conv_w, conv_g)


PV_W0, PV_A0, PV_V0, PV_KK, PV_KA, PV_RK, PV_LNG, PV_LNB, PV_MUR, PV_MUK, PV_MUV, PV_MULO = range(12)
PV_ROWS = 16

_DIMS = {"nn": ((1,), (0,)), "nt": ((1,), (1,)), "tn": ((0,), (0,))}


def _split(x):
    hi = x.astype(BF16)
    return hi, (x - hi.astype(F32)).astype(BF16)


def _mm(x, w, form="nn", x_hp=False):
    dot = lambda p, q: lax.dot_general(p, q, (_DIMS[form], ((), ())), preferred_element_type=F32)
    w = w.astype(BF16)
    if not x_hp:
        return dot(x.astype(BF16), w)
    hi, lo = _split(x)
    return dot(hi, w) + dot(lo, w)


def _wkv_kernel(has_vres, *refs):
    if has_vres:
        (r_ref, k_ref, v_ref, lo_ref, vf_ref, pv_ref, wa2_ref, g2_ref, v2_ref,
         y_ref, state_ref, carry_ref) = refs
    else:
        (r_ref, k_ref, v_ref, lo_ref, pv_ref, wa2_ref, g2_ref,
         y_ref, vf_out_ref, state_ref, carry_ref) = refs
    NB, C, W = r_ref.shape
    R = NB * C
    G = LANES
    n_groups = W // G

    @pl.when(pl.program_id(1) == 0)
    def _():
        state_ref[...] = jnp.zeros_like(state_ref)
        carry_ref[...] = jnp.zeros_like(carry_ref)

    def pv(i):
        return pv_ref[i:i + 1, :]

    def token_shift(x_ref, slot, mu):
        width = x_ref.shape[-1]
        x = x_ref[...].reshape(R, width)
        row = lax.broadcasted_iota(jnp.int32, (R, width), 0)
        prev = pltpu.roll(x, 1, 0)
        for b in range(NB):
            crow = b * SUBLANES + slot
            prev = jnp.where(row == b * C, carry_ref[crow:crow + 1, :width], prev)
            carry_ref[crow:crow + 1, :width] = x[(b + 1) * C - 1:(b + 1) * C, :]
        return x + mu[:, :width] * (prev - x)

    r = token_shift(r_ref, 0, pv(PV_MUR))
    k = token_shift(k_ref, 1, pv(PV_MUK))
    v = token_shift(v_ref, 2, pv(PV_MUV))
    lo = token_shift(lo_ref, 3, pv(PV_MULO))

    wa = lo[:, LO_WD:LO_WD + LANES]
    wa = jnp.where(lax.broadcasted_iota(jnp.int32, wa.shape, 1) < LO_AD, jnp.tanh(wa), wa)
    gd = lo[:, LO_GD:LO_GD + LO_GD_W]
    if has_vres:
        vd = lo[:, LO_VD:LO_VD + LO_VD_W]
        mix = jax.nn.sigmoid(pv(PV_V0) + _mm(vd, v2_ref[...]))
        v = v + (vf_ref[...].reshape(R, W) - v) * mix
    else:
        vf_out_ref[...] = v.reshape(NB, C, W)

    za = _mm(wa, wa2_ref[...])
    logw = -math.exp(-0.5) * jax.nn.sigmoid(pv(PV_W0) + za[:, :W])
    a = jax.nn.sigmoid(pv(PV_A0) + za[:, W:])
    g = _mm(jax.nn.sigmoid(gd), g2_ref[...])
    kk_raw = k * pv(PV_KK)
    k2 = k * (1.0 + (a - 1.0) * pv(PV_KA))

    gi = lax.broadcasted_iota(jnp.int32, (G, G), 0)
    gj = lax.broadcasted_iota(jnp.int32, (G, G), 1)
    same_head = (gi // HEAD_DIM) == (gj // HEAD_DIM)
    bd_mask = same_head.astype(F32).astype(BF16)
    ti = lax.broadcasted_iota(jnp.int32, (C, G), 0)
    si = lax.broadcasted_iota(jnp.int32, (C, G), 1) % HEAD_DIM
    incl_p, strict_p = ti >= si, ti > si
    eye_p = (ti == si).astype(F32)
    half_pair = []
    blk = 2
    while blk <= C:
        half_pair.append(((ti // blk) == (si // blk)) & ((ti // (blk // 2)) != (si // (blk // 2))))
        blk *= 2

    def bd(y):
        return jnp.concatenate([y] * GROUP_HEADS, axis=0) * bd_mask

    def group_cols(x):
        return [x[:, i * G:(i + 1) * G] for i in range(n_groups)]

    def head_sum(x, x_hp):
        s = _mm(jnp.concatenate(group_cols(x), axis=0), bd_mask, x_hp=x_hp)
        return jnp.concatenate([s[i * R:(i + 1) * R] for i in range(n_groups)], axis=1)

    kk = kk_raw * jnp.minimum(lax.rsqrt(head_sum(kk_raw * kk_raw, False)), 1e12)
    bonus = head_sum(r * k2 * pv(PV_RK), True) * v

    tc = lax.broadcasted_iota(jnp.int32, (C, C), 0)
    sc = lax.broadcasted_iota(jnp.int32, (C, C), 1)
    tri = (tc >= sc).astype(F32).astype(BF16)
    tri3 = jnp.concatenate([tri, tri, tri], axis=1)
    l_hi = logw.astype(BF16)
    l_rest = logw - l_hi.astype(F32)
    l_mid = l_rest.astype(BF16)
    l_lo = (l_rest - l_mid.astype(F32)).astype(BF16)
    seq_rows = lambda x, b: x[b * C:(b + 1) * C]
    cs_b = [jnp.dot(tri3, jnp.concatenate([seq_rows(l_hi, b), seq_rows(l_mid, b), seq_rows(l_lo, b)],
                                          axis=0), preferred_element_type=F32) for b in range(NB)]
    cs = jnp.concatenate(cs_b, axis=0)
    cs_last = [c[C - 1:C, :] for c in cs_b]
    cs_end = jnp.concatenate([jnp.broadcast_to(c, (C, W)) for c in cs_last], axis=0)
    e_neg = jnp.exp(-cs)
    e_tail = jnp.exp(cs_end - cs)
    p_last = [jnp.exp(c) for c in cs_last]

    b_raw = kk * a
    r_t = (r * jnp.exp(cs)).astype(BF16)
    a_t = (-kk * jnp.exp(cs - logw)).astype(BF16)
    b_t = (b_raw * e_neg).astype(BF16)
    k_t = (k2 * e_neg).astype(BF16)
    b_p = (b_raw * e_tail).astype(BF16)
    k_p = (k2 * e_tail).astype(BF16)
    v_b = v.astype(BF16)

    def units_of(x):
        return [x[b * C:(b + 1) * C, i * G:(i + 1) * G] for b in range(NB) for i in range(n_groups)]

    rg, ag, bg, kg, vg, bpg, kpg = map(units_of, (r_t, a_t, b_t, k_t, v_b, b_p, k_p))
    units = range(NB * n_groups)
    stack = lambda p, q: jnp.concatenate([p, q], axis=0)
    lhs = [stack(ag[i], rg[i]) for i in units]
    sc_b = [_mm(lhs[i], bd(bg[i]), "nt") for i in units]
    sc_k = [_mm(lhs[i], bd(kg[i]), "nt") for i in units]
    a_ab = [jnp.where(strict_p, sc_b[i][:C], 0.0) for i in units]
    a_rb = [jnp.where(incl_p, sc_b[i][C:], 0.0) for i in units]
    a_ak = [jnp.where(strict_p, sc_k[i][:C], 0.0) for i in units]
    a_rk = [jnp.where(incl_p, sc_k[i][C:], 0.0) for i in units]
    kv = [_mm(vg[i], kpg[i], "tn") for i in units]

    inv = [eye_p + jnp.where(half_pair[0], a_ab[i], 0.0) for i in units]
    inv_b = [x.astype(BF16) for x in inv]
    for lvl in range(1, len(half_pair)):
        n_inv = [_mm(jnp.where(half_pair[lvl], a_ab[i], 0.0), bd(inv_b[i])) for i in units]
        inv = [inv[i] + _mm(inv_b[i], bd(n_inv[i].astype(BF16))) for i in units]
        inv_b = [x.astype(BF16) for x in inv]

    av_yv = [_mm(stack(a_ak[i], a_rk[i]), bd(vg[i])) for i in units]
    a_hat = [_mm(inv_b[i], bd(ag[i])) for i in units]
    u_v = [_mm(inv_b[i], bd(av_yv[i][:C].astype(BF16))) for i in units]

    s0 = [state_ref[i] for i in units]
    res = [_mm(stack(a_hat[i].astype(BF16), rg[i]), s0[i], "nt") for i in units]
    u = [(res[i][:C] + u_v[i]).astype(BF16) for i in units]
    upd = [_mm(u[i], bpg[i], "tn") + kv[i] for i in units]
    for i in units:
        b, gidx = divmod(i, n_groups)
        state_ref[i] = (s0[i] * p_last[b][:, gidx * G:(gidx + 1) * G]
                        + jnp.where(same_head, upd[i], 0.0))
    y_units = [res[i][C:] + _mm(a_rb[i], bd(u[i])) + av_yv[i][C:] for i in units]

    y = jnp.concatenate([jnp.concatenate(y_units[b * n_groups:(b + 1) * n_groups], axis=1)
                         for b in range(NB)], axis=0)
    yc = y - head_sum(y, True) * (1.0 / HEAD_DIM)
    var = head_sum(yc * yc, False) * (1.0 / HEAD_DIM)
    yn = yc * lax.rsqrt(var + LNX_EPS) * pv(PV_LNG) + pv(PV_LNB)
    y_ref[...] = ((yn + bonus) * g).astype(BF16).reshape(NB, C, W)


def _rwkv_group(proj, v_first, pvec, wa2, g2, v2, layer, *, batch, nb):
    Lp = proj.shape[1]
    W = pvec.shape[2]
    C = WKV_CHUNK
    has_vres = v_first is not None
    tok = lambda col: pl.BlockSpec((nb, C, W), lambda bp, c: (bp, c, col))
    par = lambda arr: pl.BlockSpec((None,) + arr.shape[1:], lambda bp, c: (layer, 0, 0))
    lo_spec = pl.BlockSpec((nb, C, LO_W), lambda bp, c: (bp, c, 6 * W // LO_W))
    in_specs = [tok(3), tok(4), tok(5), lo_spec]
    args = [proj, proj, proj, proj]
    if has_vres:
        in_specs.append(tok(0))
        args.append(v_first)
    in_specs += [par(pvec), par(wa2), par(g2)]
    args += [pvec, wa2, g2]
    if has_vres:
        in_specs.append(par(v2))
        args.append(v2)
        out_shape = jax.ShapeDtypeStruct((batch, Lp, W), BF16)
        out_specs = tok(0)
    else:
        out_shape = (jax.ShapeDtypeStruct((batch, Lp, W), BF16),
                     jax.ShapeDtypeStruct((batch, Lp, W), F32))
        out_specs = (tok(0), tok(0))
    return pl.pallas_call(
        functools.partial(_wkv_kernel, has_vres),
        out_shape=out_shape,
        grid=(batch // nb, Lp // C),
        in_specs=in_specs,
        out_specs=out_specs,
        scratch_shapes=[
            pltpu.VMEM((nb * (W // LANES), LANES, LANES), F32),
            pltpu.VMEM((nb * SUBLANES, W), F32),
        ],
        compiler_params=_cparams(("parallel", "arbitrary")),
        name="rwkv_group",
    )(*args)


def _proj_out_kernel(yc_ref, yr_ref, w_ref, g_ref, h_ref, o_ref):
    half = yc_ref.shape[1]
    y = (jnp.dot(yc_ref[...], w_ref[0:half, :], preferred_element_type=F32)
         + jnp.dot(yr_ref[...], w_ref[half:2 * half, :], preferred_element_type=F32))
    o_ref[...] = h_ref[...] + _rms(y, g_ref[...])


def _proj_out(y_conv, y_rwkv, w_out, g, h, layer, *, tm):
    T, D = h.shape
    half = y_conv.shape[1]
    return pl.pallas_call(
        _proj_out_kernel,
        out_shape=jax.ShapeDtypeStruct((T, D), F32),
        grid=(T // tm,),
        in_specs=[
            pl.BlockSpec((tm, half), lambda i: (i, 0)),
            pl.BlockSpec((tm, half), lambda i: (i, 0)),
            pl.BlockSpec((None,) + w_out.shape[1:], lambda i: (layer, 0, 0)),
            _layer_vec(layer, D),
            pl.BlockSpec((tm, D), lambda i: (i, 0)),
        ],
        out_specs=pl.BlockSpec((tm, D), lambda i: (i, 0)),
        compiler_params=_cparams(("parallel",)),
        name="proj_out",
    )(y_conv, y_rwkv, w_out, g, h)


def _pad_axis(w, axis, size):
    pad = [(0, 0)] * w.ndim
    pad[axis] = (0, size - w.shape[axis])
    return jnp.pad(w, pad)


def _lora_cols(wd, ad, gd, vd):
    out = jnp.zeros(wd.shape[:-1] + (LO_W,), wd.dtype)
    out = out.at[..., LO_WD:LO_WD + R_DECAY].set(wd)
    out = out.at[..., LO_AD:LO_AD + R_ICLR].set(ad)
    out = out.at[..., LO_GD:LO_GD + R_GATE].set(gd)
    return out.at[..., LO_VD:LO_VD + R_VRES].set(vd)


def _first_layer_zero(w):
    return jnp.concatenate([jnp.zeros_like(w[:1]), w], axis=0)


def kernel(x, meta_tokens, ffn1_pre_g, ffn1_w_gu, ffn1_w_down, ffn1_post_g, mix_pre_g, w_in, w_in_vres, mu_rwkv, mu_vres, conv_w, conv_norm_g, decay_w0, decay_w2, iclr_a0, iclr_a2, vres_v0, vres_v2, gate_g2, k_k, k_a, r_k, lnx_g, lnx_b, w_out, mix_post_g, ffn2_pre_g, ffn2_w_gu, ffn2_w_down, ffn2_post_g):
    B, S, D = x.shape
    depth = w_in.shape[0]
    conv_width = conv_w.shape[2]
    rw = decay_w0.shape[1]
    L = N_META + S
    Lp = -(-L // WKV_CHUNK) * WKV_CHUNK
    T = B * Lp
    pick = lambda n, cands: next(c for c in cands if n % c == 0)
    tm = pick(T, (640, 512, 256, WKV_CHUNK))
    tm_proj = pick(T, (1280, 640, 512, 256, WKV_CHUNK))
    tf = pick(ffn1_w_down.shape[1], (512, 256, 128))
    tt_conv = pick(Lp, (832, 512, 256, WKV_CHUNK))
    nb = pick(B, (2, 1))
    tn = LO_W
    o = 3 * conv_width + 3 * rw

    meta = jnp.broadcast_to(meta_tokens.astype(x.dtype)[None], (B, N_META, D))
    h = jnp.concatenate([meta, x, jnp.zeros((B, Lp - L, D), x.dtype)], axis=1).reshape(T, D)

    vec = lambda a: a.reshape(depth, 1, -1)
    ffn1 = (vec(ffn1_pre_g), ffn1_w_gu.astype(BF16), ffn1_w_down.astype(BF16), vec(ffn1_post_g))
    ffn2 = (vec(ffn2_pre_g), ffn2_w_gu.astype(BF16), ffn2_w_down.astype(BF16), vec(ffn2_post_g))
    lo_w = _lora_cols(w_in[..., o:o + R_DECAY], w_in[..., o + R_DECAY:o + R_DECAY + R_ICLR],
                      w_in[..., o + R_DECAY + R_ICLR:], _first_layer_zero(w_in_vres))
    w_cat = jnp.concatenate([w_in[..., :o], lo_w], axis=-1).astype(BF16)
    m = 3 * rw
    mu_lo = _lora_cols(mu_rwkv[:, m:m + R_DECAY], mu_rwkv[:, m + R_DECAY:m + R_DECAY + R_ICLR],
                       mu_rwkv[:, m + R_DECAY + R_ICLR:], _first_layer_zero(mu_vres))
    pvec = jnp.stack([decay_w0, iclr_a0, _first_layer_zero(vres_v0), k_k, k_a, r_k.reshape(depth, rw),
                      lnx_g, lnx_b, mu_rwkv[:, :rw], mu_rwkv[:, rw:2 * rw], mu_rwkv[:, 2 * rw:m],
                      _pad_axis(mu_lo, 1, rw)]
                     + [jnp.zeros((depth, rw), F32)] * (PV_ROWS - 12), axis=1)
    wa2 = jnp.zeros((depth, LANES, 2 * rw), F32)
    wa2 = wa2.at[:, LO_WD:LO_WD + R_DECAY, :rw].set(decay_w2)
    wa2 = wa2.at[:, LO_AD:LO_AD + R_ICLR, rw:].set(iclr_a2).astype(BF16)
    g2 = _pad_axis(gate_g2, 1, LO_GD_W).astype(BF16)
    v2 = _pad_axis(_first_layer_zero(vres_v2), 1, LO_VD_W).astype(BF16)
    w_out_b = w_out.astype(BF16)
    mix_pre, mix_post, conv_g = vec(mix_pre_g), vec(mix_post_g), vec(conv_norm_g)

    v_first = None
    for i in range(depth):
        h = _ffn(h, *ffn1, i, tm=tm, tf=tf)
        proj = _proj_in(h, mix_pre, w_cat, i, tm=tm_proj, tn=tn)
        y_conv = _conv_group(proj, conv_w, conv_g, i, batch=B, tt=tt_conv)
        proj3 = proj.reshape(B, Lp, proj.shape[1])
        if i == 0:
            y_rwkv, v_first = _rwkv_group(proj3, None, pvec, wa2, g2, None, i, batch=B, nb=nb)
        else:
            y_rwkv = _rwkv_group(proj3, v_first, pvec, wa2, g2, v2, i, batch=B, nb=nb)
        h = _proj_out(y_conv, y_rwkv.reshape(T, rw), w_out_b, mix_post, h, i, tm=tm)
        h = _ffn(h, *ffn2, i, tm=tm, tf=tf)
    return h.reshape(B, Lp, D)[:, N_META:L]
```

```python
import functools
import math

import jax
import jax.numpy as jnp
from jax import lax
from jax.experimental import pallas as pl
from jax.experimental.pallas import tpu as pltpu

F32 = jnp.float32
BF16 = jnp.bfloat16

NORM_EPS = 1e-6
LNX_EPS = 64e-5
N_META = 16
HEAD_DIM = 64
WKV_CHUNK = 64
R_DECAY, R_ICLR, R_GATE, R_VRES = 64, 64, 160, 32
LO_WD, LO_AD, LO_GD, LO_VD = 0, 64, 128, 384
LO_GD_W = 256
LO_VD_W = 128
LO_W = 512
LANES = 128
GROUP_HEADS = LANES // HEAD_DIM
SUBLANES = 8

VMEM_LIMIT_BYTES = 56 * 1024 * 1024


def _cparams(sem):
    return pltpu.CompilerParams(dimension_semantics=sem, vmem_limit_bytes=VMEM_LIMIT_BYTES)


def _rms(x, g):
    ms = jnp.mean(x * x, axis=-1, keepdims=True)
    return x * lax.rsqrt(ms + NORM_EPS) * g


def _layer_vec(layer, width):
    return pl.BlockSpec((None, 1, width), lambda *_: (layer, 0, 0))


def _ffn_kernel(h_ref, pre_g_ref, wg_ref, wu_ref, wd_ref, post_g_ref, o_ref, xn_ref):
    j = pl.program_id(1)

    @pl.when(j == 0)
    def _():
        xn_ref[...] = _rms(h_ref[...], pre_g_ref[...]).astype(BF16)
        o_ref[...] = jnp.zeros_like(o_ref)

    xn = xn_ref[...]
    gate = jnp.dot(xn, wg_ref[...], preferred_element_type=F32)
    up = jnp.dot(xn, wu_ref[...], preferred_element_type=F32)
    act = (gate * jax.nn.sigmoid(gate) * up).astype(BF16)
    o_ref[...] += jnp.dot(act, wd_ref[...], preferred_element_type=F32)

    @pl.when(j == pl.num_programs(1) - 1)
    def _():
        o_ref[...] = h_ref[...] + 0.5 * _rms(o_ref[...], post_g_ref[...])


def _ffn(h, pre_g, w_gu, w_down, post_g, layer, *, tm, tf):
    T, D = h.shape
    FF = w_down.shape[1]
    nj = FF // tf
    return pl.pallas_call(
        _ffn_kernel,
        out_shape=jax.ShapeDtypeStruct((T, D), F32),
        grid=(T // tm, nj),
        in_specs=[
            pl.BlockSpec((tm, D), lambda i, j: (i, 0)),
            _layer_vec(layer, D),
            pl.BlockSpec((None, D, tf), lambda i, j: (layer, 0, j)),
            pl.BlockSpec((None, D, tf), lambda i, j: (layer, 0, j + nj)),
            pl.BlockSpec((None, tf, D), lambda i, j: (layer, j, 0)),
            _layer_vec(layer, D),
        ],
        out_specs=pl.BlockSpec((tm, D), lambda i, j: (i, 0)),
        scratch_shapes=[pltpu.VMEM((tm, D), BF16)],
        compiler_params=_cparams(("parallel", "arbitrary")),
        name="ffn",
    )(h, pre_g, w_gu, w_gu, w_down, post_g)


def _proj_in_kernel(h_ref, g_ref, w_ref, o_ref, xn_ref):
    @pl.when(pl.program_id(1) == 0)
    def _():
        xn_ref[...] = _rms(h_ref[...], g_ref[...]).astype(BF16)

    o_ref[...] = jnp.dot(xn_ref[...], w_ref[...], preferred_element_type=F32)


def _proj_in(h, g, w_cat, layer, *, tm, tn):
    T, D = h.shape
    P = w_cat.shape[2]
    return pl.pallas_call(
        _proj_in_kernel,
        out_shape=jax.ShapeDtypeStruct((T, P), F32),
        grid=(T // tm, P // tn),
        in_specs=[
            pl.BlockSpec((tm, D), lambda i, j: (i, 0)),
            _layer_vec(layer, D),
            pl.BlockSpec((None, D, tn), lambda i, j: (layer, 0, j)),
        ],
        out_specs=pl.BlockSpec((tm, tn), lambda i, j: (i, j)),
        scratch_shapes=[pltpu.VMEM((tm, D), BF16)],
        compiler_params=_cparams(("parallel", "arbitrary")),
        name="proj_in",
    )(h, g, w_cat)


def _conv_kernel(gb_ref, gc_ref, u_ref, cw_ref, cg_ref, o_ref, ext_ref):
    tt = u_ref.shape[0]

    @pl.when(pl.program_id(1) == 0)
    def _():
        ext_ref[0:8, :] = jnp.zeros((8, ext_ref.shape[1]), F32)

    uc = gc_ref[...] * u_ref[...]
    ext_ref[8:8 + tt, :] = uc
    conv = (cw_ref[0:1, :] * ext_ref[6:6 + tt, :] + cw_ref[1:2, :] * ext_ref[7:7 + tt, :]
            + cw_ref[2:3, :] * uc)
    o_ref[...] = _rms(gb_ref[...] * conv, cg_ref[...]).astype(BF16)
    ext_ref[0:8, :] = ext_ref[tt:tt + 8, :]


def _conv_group(proj, conv_w, conv_g, layer, *, batch, tt):
    T = proj.shape[0]
    K, W = conv_w.shape[1:]
    nt = T // batch // tt
    row = lambda b, t: b * nt + t
    return pl.pallas_call(
        _conv_kernel,
        out_shape=jax.ShapeDtypeStruct((T, W), BF16),
        grid=(batch, nt),
        in_specs=[
            pl.BlockSpec((tt, W), lambda b, t: (row(b, t), 0)),
            pl.BlockSpec((tt, W), lambda b, t: (row(b, t), 1)),
            pl.BlockSpec((tt, W), lambda b, t: (row(b, t), 2)),
            pl.BlockSpec((None, K, W), lambda b, t: (layer, 0, 0)),
            _layer_vec(layer, W),
        ],
        out_specs=pl.BlockSpec((tt, W), lambda b, t: (row(b, t), 0)),
        scratch_shapes=[pltpu.VMEM((tt + 8, W), F32)],
        compiler_params=_cparams(("parallel", "arbitrary")),
        name="conv_group",
    )(proj, proj, proj, conv_w, conv_g)


PV_W0, PV_A0, PV_V0, PV_KK, PV_KA, PV_RK, PV_LNG, PV_LNB, PV_MUR, PV_MUK, PV_MUV, PV_MULO = range(12)
PV_ROWS = 16

_DIMS = {"nn": ((1,), (0,)), "nt": ((1,), (1,)), "tn": ((0,), (0,))}


def _split(x):
    hi = x.astype(BF16)
    return hi, (x - hi.astype(F32)).astype(BF16)


def _mm(x, w, form="nn", x_hp=False):
    dot = lambda p, q: lax.dot_general(p, q, (_DIMS[form], ((), ())), preferred_element_type=F32)
    w = w.astype(BF16)
    if not x_hp:
        return dot(x.astype(BF16), w)
    hi, lo = _split(x)
    return dot(hi, w) + dot(lo, w)


OP_A, OP_R, OP_B, OP_K, OP_V, OP_BP, OP_KP = range(7)
N_OPS = 7
PF_BONUS, PF_GATE, PF_V = range(3)


def _interleave(*stages):
    live = list(stages)
    while live:
        live = [s for s in live if next(s, True) is None]


def _wkv_kernel(has_vres, *refs):
    if has_vres:
        (r_ref, k_ref, v_ref, lo_ref, vf_ref, pv_ref, wa2_ref, g2_ref, v2_ref,
         y_ref, state_ref, carry_ref, ops_ref, pf_ref, plast_ref) = refs
    else:
        (r_ref, k_ref, v_ref, lo_ref, pv_ref, wa2_ref, g2_ref,
         y_ref, vf_out_ref, state_ref, carry_ref, ops_ref, pf_ref, plast_ref) = refs
    NB, C, W = r_ref.shape
    R = NB * C
    G = LANES
    n_groups = W // G
    step = pl.program_id(1)

    @pl.when(step == 0)
    def _():
        state_ref[...] = jnp.zeros_like(state_ref)
        carry_ref[...] = jnp.zeros_like(carry_ref)
        ops_ref[...] = jnp.zeros_like(ops_ref)
        pf_ref[...] = jnp.zeros_like(pf_ref)
        plast_ref[...] = jnp.zeros_like(plast_ref)

    def pv(i):
        return pv_ref[i:i + 1, :]

    gi = lax.broadcasted_iota(jnp.int32, (G, G), 0)
    gj = lax.broadcasted_iota(jnp.int32, (G, G), 1)
    same_head = (gi // HEAD_DIM) == (gj // HEAD_DIM)
    bd_mask = same_head.astype(F32).astype(BF16)

    def group_cols(x):
        return [x[:, i * G:(i + 1) * G] for i in range(n_groups)]

    def head_sum(x, x_hp):
        s = _mm(jnp.concatenate(group_cols(x), axis=0), bd_mask, x_hp=x_hp)
        return jnp.concatenate([s[i * R:(i + 1) * R] for i in range(n_groups)], axis=1)

    def prepare(slot):
        def token_shift(x_ref, cslot, mu):
            width = x_ref.shape[-1]
            x = x_ref[...].reshape(R, width)
            row = lax.broadcasted_iota(jnp.int32, (R, width), 0)
            prev = pltpu.roll(x, 1, 0)
            for b in range(NB):
                crow = b * SUBLANES + cslot
                prev = jnp.where(row == b * C, carry_ref[crow:crow + 1, :width], prev)
                carry_ref[crow:crow + 1, :width] = x[(b + 1) * C - 1:(b + 1) * C, :]
            return x + mu[:, :width] * (prev - x)

        lo = token_shift(lo_ref, 3, pv(PV_MULO))
        yield
        wa = lo[:, LO_WD:LO_WD + LANES]
        wa = jnp.where(lax.broadcasted_iota(jnp.int32, wa.shape, 1) < LO_AD, jnp.tanh(wa), wa)
        za = _mm(wa, wa2_ref[...])
        yield
        pf_ref[slot, PF_GATE] = _mm(jax.nn.sigmoid(lo[:, LO_GD:LO_GD + LO_GD_W]), g2_ref[...])
        yield
        v = token_shift(v_ref, 2, pv(PV_MUV))
        yield
        if has_vres:
            vd = lo[:, LO_VD:LO_VD + LO_VD_W]
            mix = jax.nn.sigmoid(pv(PV_V0) + _mm(vd, v2_ref[...]))
            v = v + (vf_ref[...].reshape(R, W) - v) * mix
        else:
            pf_ref[slot, PF_V] = v
        yield
        logw = -math.exp(-0.5) * jax.nn.sigmoid(pv(PV_W0) + za[:, :W])
        yield
        a = jax.nn.sigmoid(pv(PV_A0) + za[:, W:])
        yield
        k = token_shift(k_ref, 1, pv(PV_MUK))
        yield
        kk_raw = k * pv(PV_KK)
        k2 = k * (1.0 + (a - 1.0) * pv(PV_KA))
        yield
        kk = kk_raw * jnp.minimum(lax.rsqrt(head_sum(kk_raw * kk_raw, False)), 1e12)
        yield
        r = token_shift(r_ref, 0, pv(PV_MUR))
        yield
        pf_ref[slot, PF_BONUS] = head_sum(r * k2 * pv(PV_RK), True) * v
        yield

        tc = lax.broadcasted_iota(jnp.int32, (C, C), 0)
        sc = lax.broadcasted_iota(jnp.int32, (C, C), 1)
        tri = (tc >= sc).astype(F32).astype(BF16)
        tri3 = jnp.concatenate([tri, tri, tri], axis=1)
        l_hi = logw.astype(BF16)
        l_rest = logw - l_hi.astype(F32)
        l_mid = l_rest.astype(BF16)
        l_lo = (l_rest - l_mid.astype(F32)).astype(BF16)
        yield
        seq = lambda x, b: x[b * C:(b + 1) * C]
        cs_b = [jnp.dot(tri3, jnp.concatenate([seq(l_hi, b), seq(l_mid, b), seq(l_lo, b)], axis=0),
                        preferred_element_type=F32) for b in range(NB)]
        cs = jnp.concatenate(cs_b, axis=0)
        yield
        cs_last = [c[C - 1:C, :] for c in cs_b]
        cs_end = jnp.concatenate([jnp.broadcast_to(c, (C, W)) for c in cs_last], axis=0)
        for b in range(NB):
            plast_ref[slot, b * SUBLANES:b * SUBLANES + 1, :] = jnp.exp(cs_last[b])

        def emit(op, x):
            xb = x.astype(BF16)
            for b in range(NB):
                ops_ref[slot, b, op * C:(op + 1) * C, :] = seq(xb, b)

        emit(OP_V, v)
        yield
        emit(OP_R, r * jnp.exp(cs))
        yield
        emit(OP_A, -kk * jnp.exp(cs - logw))
        yield
        b_raw = kk * a
        e_neg = jnp.exp(-cs)
        yield
        emit(OP_B, b_raw * e_neg)
        yield
        emit(OP_K, k2 * e_neg)
        yield
        e_tail = jnp.exp(cs_end - cs)
        yield
        emit(OP_BP, b_raw * e_tail)
        yield
        emit(OP_KP, k2 * e_tail)

    def scan(slot):
        ti = lax.broadcasted_iota(jnp.int32, (C, G), 0)
        si = lax.broadcasted_iota(jnp.int32, (C, G), 1) % HEAD_DIM
        incl_p, strict_p = ti >= si, ti > si
        eye_p = (ti == si).astype(F32)
        half_pair = []
        blk = 2
        while blk <= C:
            half_pair.append(((ti // blk) == (si // blk)) & ((ti // (blk // 2)) != (si // (blk // 2))))
            blk *= 2

        def bd(y):
            return jnp.concatenate([y] * GROUP_HEADS, axis=0) * bd_mask

        unit_ids = [(b, i) for b in range(NB) for i in range(n_groups)]
        units = range(len(unit_ids))

        def operand(op, n_ops=1):
            return [ops_ref[slot, b, op * C:(op + n_ops) * C, i * G:(i + 1) * G] for b, i in unit_ids]

        lhs = operand(OP_A, 2)
        ag, rg, bg, kg, vg, bpg, kpg = map(operand, (OP_A, OP_R, OP_B, OP_K, OP_V, OP_BP, OP_KP))
        sc_b = [_mm(lhs[i], bd(bg[i]), "nt") for i in units]
        yield
        sc_k = [_mm(lhs[i], bd(kg[i]), "nt") for i in units]
        yield
        a_ab = [jnp.where(strict_p, sc_b[i][:C], 0.0) for i in units]
        a_rb = [jnp.where(incl_p, sc_b[i][C:], 0.0) for i in units]
        a_ak = [jnp.where(strict_p, sc_k[i][:C], 0.0) for i in units]
        a_rk = [jnp.where(incl_p, sc_k[i][C:], 0.0) for i in units]
        yield
        kv = [_mm(vg[i], kpg[i], "tn") for i in units]
        yield

        inv = [eye_p + jnp.where(half_pair[0], a_ab[i], 0.0) for i in units]
        inv_b = [x.astype(BF16) for x in inv]
        for lvl in range(1, len(half_pair)):
            n_inv = [_mm(jnp.where(half_pair[lvl], a_ab[i], 0.0), bd(inv_b[i])) for i in units]
            yield
            inv = [inv[i] + _mm(inv_b[i], bd(n_inv[i].astype(BF16))) for i in units]
            inv_b = [x.astype(BF16) for x in inv]
            yield

        stack = lambda p, q: jnp.concatenate([p, q], axis=0)
        av_yv = [_mm(stack(a_ak[i], a_rk[i]), bd(vg[i])) for i in units]
        yield
        a_hat = [_mm(inv_b[i], bd(ag[i])) for i in units]
        yield
        u_v = [_mm(inv_b[i], bd(av_yv[i][:C].astype(BF16))) for i in units]
        yield

        s0 = [state_ref[i] for i in units]
        res = [_mm(stack(a_hat[i].astype(BF16), rg[i]), s0[i], "nt") for i in units]
        yield
        u = [(res[i][:C] + u_v[i]).astype(BF16) for i in units]
        upd = [_mm(u[i], bpg[i], "tn") + kv[i] for i in units]
        yield
        for n, (b, i) in enumerate(unit_ids):
            p_last = plast_ref[slot, b * SUBLANES:b * SUBLANES + 1, i * G:(i + 1) * G]
            state_ref[n] = s0[n] * p_last + jnp.where(same_head, upd[n], 0.0)
        yield
        y_units = [res[i][C:] + _mm(a_rb[i], bd(u[i])) + av_yv[i][C:] for i in units]
        yield

        y = jnp.concatenate([jnp.concatenate(y_units[b * n_groups:(b + 1) * n_groups], axis=1)
                             for b in range(NB)], axis=0)
        yc = y - head_sum(y, True) * (1.0 / HEAD_DIM)
        yield
        var = head_sum(yc * yc, False) * (1.0 / HEAD_DIM)
        yield
        yn = yc * lax.rsqrt(var + LNX_EPS) * pv(PV_LNG) + pv(PV_LNB)
        out = (yn + pf_ref[slot, PF_BONUS]) * pf_ref[slot, PF_GATE]
        y_ref[...] = out.astype(BF16).reshape(NB, C, W)
        if not has_vres:
            vf_out_ref[...] = pf_ref[slot, PF_V].reshape(NB, C, W)

    for parity in range(2):
        @pl.when(step % 2 == parity)
        def _():
            _interleave(scan(1 - parity), prepare(parity))


def _rwkv_group(proj, v_first, pvec, wa2, g2, v2, layer, *, batch, nb):
    Lp = proj.shape[1]
    W = pvec.shape[2]
    C = WKV_CHUNK
    nc = Lp // C
    has_vres = v_first is not None
    tok_in = lambda col: pl.BlockSpec((nb, C, W), lambda bp, c: (bp, jnp.minimum(c, nc - 1), col))
    tok_out = pl.BlockSpec((nb, C, W), lambda bp, c: (bp, jnp.maximum(c - 1, 0), 0))
    par = lambda arr: pl.BlockSpec((None,) + arr.shape[1:], lambda bp, c: (layer, 0, 0))
    lo_spec = pl.BlockSpec((nb, C, LO_W), lambda bp, c: (bp, jnp.minimum(c, nc - 1), 6 * W // LO_W))
    in_specs = [tok_in(3), tok_in(4), tok_in(5), lo_spec]
    args = [proj, proj, proj, proj]
    if has_vres:
        in_specs.append(tok_in(0))
        args.append(v_first)
    in_specs += [par(pvec), par(wa2), par(g2)]
    args += [pvec, wa2, g2]
    if has_vres:
        in_specs.append(par(v2))
        args.append(v2)
        out_shape = jax.ShapeDtypeStruct((batch, Lp, W), BF16)
        out_specs = tok_out
    else:
        out_shape = (jax.ShapeDtypeStruct((batch, Lp, W), BF16),
                     jax.ShapeDtypeStruct((batch, Lp, W), F32))
        out_specs = (tok_out, tok_out)
    return pl.pallas_call(
        functools.partial(_wkv_kernel, has_vres),
        out_shape=out_shape,
        grid=(batch // nb, nc + 1),
        in_specs=in_specs,
        out_specs=out_specs,
        scratch_shapes=[
            pltpu.VMEM((nb * (W // LANES), LANES, LANES), F32),
            pltpu.VMEM((nb * SUBLANES, W), F32),
            pltpu.VMEM((2, nb, N_OPS * C, W), BF16),
            pltpu.VMEM((2, 3, nb * C, W), F32),
            pltpu.VMEM((2, nb * SUBLANES, W), F32),
        ],
        compiler_params=_cparams(("parallel", "arbitrary")),
        name="rwkv_group",
    )(*args)


def _proj_out_kernel(yc_ref, yr_ref, w_ref, g_ref, h_ref, o_ref):
    half = yc_ref.shape[1]
    y = (jnp.dot(yc_ref[...], w_ref[0:half, :], preferred_element_type=F32)
         + jnp.dot(yr_ref[...], w_ref[half:2 * half, :], preferred_element_type=F32))
    o_ref[...] = h_ref[...] + _rms(y, g_ref[...])


def _proj_out(y_conv, y_rwkv, w_out, g, h, layer, *, tm):
    T, D = h.shape
    half = y_conv.shape[1]
    return pl.pallas_call(
        _proj_out_kernel,
        out_shape=jax.ShapeDtypeStruct((T, D), F32),
        grid=(T // tm,),
        in_specs=[
            pl.BlockSpec((tm, half), lambda i: (i, 0)),
            pl.BlockSpec((tm, half), lambda i: (i, 0)),
            pl.BlockSpec((None,) + w_out.shape[1:], lambda i: (layer, 0, 0)),
            _layer_vec(layer, D),
            pl.BlockSpec((tm, D), lambda i: (i, 0)),
        ],
        out_specs=pl.BlockSpec((tm, D), lambda i: (i, 0)),
        compiler_params=_cparams(("parallel",)),
        name="proj_out",
    )(y_conv, y_rwkv, w_out, g, h)


def _pad_axis(w, axis, size):
    pad = [(0, 0)] * w.ndim
    pad[axis] = (0, size - w.shape[axis])
    return jnp.pad(w, pad)


def _lora_cols(wd, ad, gd, vd):
    out = jnp.zeros(wd.shape[:-1] + (LO_W,), wd.dtype)
    out = out.at[..., LO_WD:LO_WD + R_DECAY].set(wd)
    out = out.at[..., LO_AD:LO_AD + R_ICLR].set(ad)
    out = out.at[..., LO_GD:LO_GD + R_GATE].set(gd)
    return out.at[..., LO_VD:LO_VD + R_VRES].set(vd)


def _first_layer_zero(w):
    return jnp.concatenate([jnp.zeros_like(w[:1]), w], axis=0)


def kernel(x, meta_tokens, ffn1_pre_g, ffn1_w_gu, ffn1_w_down, ffn1_post_g, mix_pre_g, w_in, w_in_vres, mu_rwkv, mu_vres, conv_w, conv_norm_g, decay_w0, decay_w2, iclr_a0, iclr_a2, vres_v0, vres_v2, gate_g2, k_k, k_a, r_k, lnx_g, lnx_b, w_out, mix_post_g, ffn2_pre_g, ffn2_w_gu, ffn2_w_down, ffn2_post_g):
    B, S, D = x.shape
    depth = w_in.shape[0]
    conv_width = conv_w.shape[2]
    rw = decay_w0.shape[1]
    L = N_META + S
    Lp = -(-L // WKV_CHUNK) * WKV_CHUNK
    T = B * Lp
    pick = lambda n, cands: next(c for c in cands if n % c == 0)
    tm = pick(T, (640, 512, 256, WKV_CHUNK))
    tm_ffn = pick(T, (832, 640, 512, 256, WKV_CHUNK))
    tm_proj = pick(T, (1280, 640, 512, 256, WKV_CHUNK))
    tf = pick(ffn1_w_down.shape[1], (512, 256, 128))
    tt_conv = pick(Lp, (832, 512, 256, WKV_CHUNK))
    nb = pick(B, (2, 1))
    tn = LO_W
    o = 3 * conv_width + 3 * rw

    meta = jnp.broadcast_to(meta_tokens.astype(x.dtype)[None], (B, N_META, D))
    h = jnp.concatenate([meta, x, jnp.zeros((B, Lp - L, D), x.dtype)], axis=1).reshape(T, D)

    vec = lambda a: a.reshape(depth, 1, -1)
    ffn1 = (vec(ffn1_pre_g), ffn1_w_gu.astype(BF16), ffn1_w_down.astype(BF16), vec(ffn1_post_g))
    ffn2 = (vec(ffn2_pre_g), ffn2_w_gu.astype(BF16), ffn2_w_down.astype(BF16), vec(ffn2_post_g))
    lo_w = _lora_cols(w_in[..., o:o + R_DECAY], w_in[..., o + R_DECAY:o + R_DECAY + R_ICLR],
                      w_in[..., o + R_DECAY + R_ICLR:], _first_layer_zero(w_in_vres))
    w_cat = jnp.concatenate([w_in[..., :o], lo_w], axis=-1).astype(BF16)
    m = 3 * rw
    mu_lo = _lora_cols(mu_rwkv[:, m:m + R_DECAY], mu_rwkv[:, m + R_DECAY:m + R_DECAY + R_ICLR],
                       mu_rwkv[:, m + R_DECAY + R_ICLR:], _first_layer_zero(mu_vres))
    pvec = jnp.stack([decay_w0, iclr_a0, _first_layer_zero(vres_v0), k_k, k_a, r_k.reshape(depth, rw),
                      lnx_g, lnx_b, mu_rwkv[:, :rw], mu_rwkv[:, rw:2 * rw], mu_rwkv[:, 2 * rw:m],
                      _pad_axis(mu_lo, 1, rw)]
                     + [jnp.zeros((depth, rw), F32)] * (PV_ROWS - 12), axis=1)
    wa2 = jnp.zeros((depth, LANES, 2 * rw), F32)
    wa2 = wa2.at[:, LO_WD:LO_WD + R_DECAY, :rw].set(decay_w2)
    wa2 = wa2.at[:, LO_AD:LO_AD + R_ICLR, rw:].set(iclr_a2).astype(BF16)
    g2 = _pad_axis(gate_g2, 1, LO_GD_W).astype(BF16)
    v2 = _pad_axis(_first_layer_zero(vres_v2), 1, LO_VD_W).astype(BF16)
    w_out_b = w_out.astype(BF16)
    mix_pre, mix_post, conv_g = vec(mix_pre_g), vec(mix_post_g), vec(conv_norm_g)

    v_first = None
    for i in range(depth):
        h = _ffn(h, *ffn1, i, tm=tm_ffn, tf=tf)
        proj = _proj_in(h, mix_pre, w_cat, i, tm=tm_proj, tn=tn)
        y_conv = _conv_group(proj, conv_w, conv_g, i, batch=B, tt=tt_conv)
        proj3 = proj.reshape(B, Lp, proj.shape[1])
        if i == 0:
            y_rwkv, v_first = _rwkv_group(proj3, None, pvec, wa2, g2, None, i, batch=B, nb=nb)
        else:
            y_rwkv = _rwkv_group(proj3, v_first, pvec, wa2, g2, v2, i, batch=B, nb=nb)
        h = _proj_out(y_conv, y_rwkv.reshape(T, rw), w_out_b, mix_post, h, i, tm=tm)
        h = _ffn(h, *ffn2, i, tm=tm_ffn, tf=tf)
    return h.reshape(B, Lp, D)[:, N_META:L]
```

```python
import functools
import math

import jax
import jax.numpy as jnp
from jax import lax
from jax.experimental import pallas as pl
from jax.experimental.pallas import tpu as pltpu

F32 = jnp.float32
BF16 = jnp.bfloat16

NORM_EPS = 1e-6
LNX_EPS = 64e-5
N_META = 16
HEAD_DIM = 64
WKV_CHUNK = 64
R_DECAY, R_ICLR, R_GATE, R_VRES = 64, 64, 160, 32
LO_WD, LO_AD, LO_GD, LO_VD = 0, 64, 128, 384
LO_GD_W = 256
LO_VD_W = 128
LO_W = 512
LANES = 128
GROUP_HEADS = LANES // HEAD_DIM
SUBLANES = 8

VMEM_LIMIT_BYTES = 56 * 1024 * 1024


def _cparams(sem):
    return pltpu.CompilerParams(dimension_semantics=sem, vmem_limit_bytes=VMEM_LIMIT_BYTES)


def _rms(x, g):
    ms = jnp.mean(x * x, axis=-1, keepdims=True)
    return x * lax.rsqrt(ms + NORM_EPS) * g


def _layer_vec(layer, width):
    return pl.BlockSpec((None, 1, width), lambda *_: (layer, 0, 0))


def _ffn_kernel(h_ref, pre_g_ref, wg_ref, wu_ref, wd_ref, half_post_g_ref, o_ref, xn_ref):
    j = pl.program_id(1)

    @pl.when(j == 0)
    def _():
        xn_ref[...] = _rms(h_ref[...], pre_g_ref[...]).astype(BF16)
        o_ref[...] = jnp.zeros_like(o_ref)

    xn = xn_ref[...]
    gate = jnp.dot(xn, wg_ref[...], preferred_element_type=F32)
    up = jnp.dot(xn, wu_ref[...], preferred_element_type=F32)
    act = (gate * jax.nn.sigmoid(gate) * up).astype(BF16)
    o_ref[...] += jnp.dot(act, wd_ref[...], preferred_element_type=F32)

    @pl.when(j == pl.num_programs(1) - 1)
    def _():
        o_ref[...] = h_ref[...] + _rms(o_ref[...], half_post_g_ref[...])


def _ffn(h, pre_g, w_gu, w_down, half_post_g, layer, *, tm, tf):
    T, D = h.shape
    FF = w_down.shape[1]
    nj = FF // tf
    return pl.pallas_call(
        _ffn_kernel,
        out_shape=jax.ShapeDtypeStruct((T, D), F32),
        grid=(T // tm, nj),
        in_specs=[
            pl.BlockSpec((tm, D), lambda i, j: (i, 0)),
            _layer_vec(layer, D),
            pl.BlockSpec((None, D, tf), lambda i, j: (layer, 0, j)),
            pl.BlockSpec((None, D, tf), lambda i, j: (layer, 0, j + nj)),
            pl.BlockSpec((None, tf, D), lambda i, j: (layer, j, 0)),
            _layer_vec(layer, D),
        ],
        out_specs=pl.BlockSpec((tm, D), lambda i, j: (i, 0)),
        scratch_shapes=[pltpu.VMEM((tm, D), BF16)],
        compiler_params=_cparams(("parallel", "arbitrary")),
        name="ffn",
    )(h, pre_g, w_gu, w_gu, w_down, half_post_g)


def _proj_in_kernel(h_ref, g_ref, w_ref, w_lo_ref, o_ref, xn_ref):
    j = pl.program_id(1)
    last = pl.num_programs(1) - 1

    @pl.when(j == 0)
    def _():
        xn_ref[...] = _rms(h_ref[...], g_ref[...]).astype(BF16)

    @pl.when(j < last)
    def _():
        o_ref[...] = jnp.dot(xn_ref[...], w_ref[...], preferred_element_type=F32)

    @pl.when(j == last)
    def _():
        o_ref[...] = jnp.dot(xn_ref[...], w_lo_ref[...], preferred_element_type=F32)


def _proj_in(h, g, w_main, n_main, w_lo, layer, *, tm):
    T, D = h.shape
    tn = w_lo.shape[2]
    n_tiles = n_main // tn
    return pl.pallas_call(
        _proj_in_kernel,
        out_shape=jax.ShapeDtypeStruct((T, n_main + tn), F32),
        grid=(T // tm, n_tiles + 1),
        in_specs=[
            pl.BlockSpec((tm, D), lambda i, j: (i, 0)),
            _layer_vec(layer, D),
            pl.BlockSpec((None, D, tn), lambda i, j: (layer, 0, jnp.minimum(j, n_tiles - 1))),
            pl.BlockSpec((None, D, tn), lambda i, j: (layer, 0, 0)),
        ],
        out_specs=pl.BlockSpec((tm, tn), lambda i, j: (i, j)),
        scratch_shapes=[pltpu.VMEM((tm, D), BF16)],
        compiler_params=_cparams(("parallel", "arbitrary")),
        name="proj_in",
    )(h, g, w_main, w_lo)


def _conv_kernel(gb_ref, gc_ref, u_ref, cw_ref, cg_ref, o_ref, ext_ref):
    tt = u_ref.shape[0]

    @pl.when(pl.program_id(1) == 0)
    def _():
        ext_ref[0:8, :] = jnp.zeros((8, ext_ref.shape[1]), F32)

    uc = gc_ref[...] * u_ref[...]
    ext_ref[8:8 + tt, :] = uc
    conv = (cw_ref[0:1, :] * ext_ref[6:6 + tt, :] + cw_ref[1:2, :] * ext_ref[7:7 + tt, :]
            + cw_ref[2:3, :] * uc)
    o_ref[...] = _rms(gb_ref[...] * conv, cg_ref[...]).astype(BF16)
    ext_ref[0:8, :] = ext_ref[tt:tt + 8, :]


def _conv_group(proj, conv_w, conv_g, layer, *, batch, tt):
    T = proj.shape[0]
    K, W = conv_w.shape[1:]
    nt = T // batch // tt
    row = lambda b, t: b * nt + t
    return pl.pallas_call(
        _conv_kernel,
        out_shape=jax.ShapeDtypeStruct((T, W), BF16),
        grid=(batch, nt),
        in_specs=[
            pl.BlockSpec((tt, W), lambda b, t: (row(b, t), 0)),
            pl.BlockSpec((tt, W), lambda b, t: (row(b, t), 1)),
            pl.BlockSpec((tt, W), lambda b, t: (row(b, t), 2)),
            pl.BlockSpec((None, K, W), lambda b, t: (layer, 0, 0)),
            _layer_vec(layer, W),
        ],
        out_specs=pl.BlockSpec((tt, W), lambda b, t: (row(b, t), 0)),
        scratch_shapes=[pltpu.VMEM((tt + 8, W), F32)],
        compiler_params=_cparams(("parallel", "arbitrary")),
        name="conv_group",
    )(proj, proj, proj, conv_w, conv_g)


PV_W0, PV_A0, PV_V0, PV_KK, PV_KA, PV_RK, PV_LNG, PV_LNB, PV_MUR, PV_MUK, PV_MUV, PV_MULO = range(12)
PV_ROWS = 16

_DIMS = {"nn": ((1,), (0,)), "nt": ((1,), (1,)), "tn": ((0,), (0,))}


def _split(x):
    hi = x.astype(BF16)
    return hi, (x - hi.astype(F32)).astype(BF16)


def _mm(x, w, form="nn", x_hp=False):
    dot = lambda p, q: lax.dot_general(p, q, (_DIMS[form], ((), ())), preferred_element_type=F32)
    w = w.astype(BF16)
    if not x_hp:
        return dot(x.astype(BF16), w)
    hi, lo = _split(x)
    return dot(hi, w) + dot(lo, w)


def _wkv_kernel(has_vres, *refs):
    if has_vres:
        (r_ref, k_ref, v_ref, lo_ref, vf_ref, pv_ref, wa2_ref, g2_ref, v2_ref,
         y_ref, state_ref, carry_ref) = refs
    else:
        (r_ref, k_ref, v_ref, lo_ref, pv_ref, wa2_ref, g2_ref,
         y_ref, vf_out_ref, state_ref, carry_ref) = refs
    NB, C, W = r_ref.shape
    R = NB * C
    G = LANES
    n_groups = W // G

    @pl.when(pl.program_id(1) == 0)
    def _():
        state_ref[...] = jnp.zeros_like(state_ref)
        carry_ref[...] = jnp.zeros_like(carry_ref)

    def pv(i):
        return pv_ref[i:i + 1, :]

    def token_shift(x_ref, slot, mu):
        width = x_ref.shape[-1]
        x = x_ref[...].reshape(R, width)
        row = lax.broadcasted_iota(jnp.int32, (R, width), 0)
        prev = pltpu.roll(x, 1, 0)
        for b in range(NB):
            crow = b * SUBLANES + slot
            prev = jnp.where(row == b * C, carry_ref[crow:crow + 1, :width], prev)
            carry_ref[crow:crow + 1, :width] = x[(b + 1) * C - 1:(b + 1) * C, :]
        return x + mu[:, :width] * (prev - x)

    r = token_shift(r_ref, 0, pv(PV_MUR))
    k = token_shift(k_ref, 1, pv(PV_MUK))
    v = token_shift(v_ref, 2, pv(PV_MUV))
    lo = token_shift(lo_ref, 3, pv(PV_MULO))

    wa = lo[:, LO_WD:LO_WD + LANES]
    wa = jnp.where(lax.broadcasted_iota(jnp.int32, wa.shape, 1) < LO_AD, jnp.tanh(wa), wa)
    gd = lo[:, LO_GD:LO_GD + LO_GD_W]
    if has_vres:
        vd = lo[:, LO_VD:LO_VD + LO_VD_W]
        mix = jax.nn.sigmoid(pv(PV_V0) + _mm(vd, v2_ref[...]))
        v = v + (vf_ref[...].reshape(R, W) - v) * mix
    else:
        vf_out_ref[...] = v.reshape(NB, C, W)

    za = _mm(wa, wa2_ref[...])
    logw = -math.exp(-0.5) * jax.nn.sigmoid(pv(PV_W0) + za[:, :W])
    a = jax.nn.sigmoid(pv(PV_A0) + za[:, W:])
    g = _mm(jax.nn.sigmoid(gd), g2_ref[...])
    kk_raw = k * pv(PV_KK)
    k2 = k * (1.0 + (a - 1.0) * pv(PV_KA))

    gi = lax.broadcasted_iota(jnp.int32, (G, G), 0)
    gj = lax.broadcasted_iota(jnp.int32, (G, G), 1)
    same_head = (gi // HEAD_DIM) == (gj // HEAD_DIM)
    bd_mask = same_head.astype(F32).astype(BF16)
    ti = lax.broadcasted_iota(jnp.int32, (C, G), 0)
    si = lax.broadcasted_iota(jnp.int32, (C, G), 1) % HEAD_DIM
    incl_p, strict_p = ti >= si, ti > si
    eye_p = (ti == si).astype(F32)
    half_pair = []
    blk = 2
    while blk <= C:
        half_pair.append(((ti // blk) == (si // blk)) & ((ti // (blk // 2)) != (si // (blk // 2))))
        blk *= 2

    def bd(y):
        return jnp.concatenate([y] * GROUP_HEADS, axis=0) * bd_mask

    def group_cols(x):
        return [x[:, i * G:(i + 1) * G] for i in range(n_groups)]

    def head_sum(x, x_hp):
        s = _mm(jnp.concatenate(group_cols(x), axis=0), bd_mask, x_hp=x_hp)
        return jnp.concatenate([s[i * R:(i + 1) * R] for i in range(n_groups)], axis=1)

    kk = kk_raw * jnp.minimum(lax.rsqrt(head_sum(kk_raw * kk_raw, False)), 1e12)
    bonus = head_sum(r * k2 * pv(PV_RK), True) * v

    tc = lax.broadcasted_iota(jnp.int32, (C, C), 0)
    sc = lax.broadcasted_iota(jnp.int32, (C, C), 1)
    tri = (tc >= sc).astype(F32).astype(BF16)
    tri3 = jnp.concatenate([tri, tri, tri], axis=1)
    l_hi = logw.astype(BF16)
    l_rest = logw - l_hi.astype(F32)
    l_mid = l_rest.astype(BF16)
    l_lo = (l_rest - l_mid.astype(F32)).astype(BF16)
    seq_rows = lambda x, b: x[b * C:(b + 1) * C]
    cs_b = [jnp.dot(tri3, jnp.concatenate([seq_rows(l_hi, b), seq_rows(l_mid, b), seq_rows(l_lo, b)],
                                          axis=0), preferred_element_type=F32) for b in range(NB)]
    cs = jnp.concatenate(cs_b, axis=0)
    cs_last = [c[C - 1:C, :] for c in cs_b]
    cs_end = jnp.concatenate([jnp.broadcast_to(c, (C, W)) for c in cs_last], axis=0)
    e_neg = jnp.exp(-cs)
    e_tail = jnp.exp(cs_end - cs)
    p_last = [jnp.exp(c) for c in cs_last]

    b_raw = kk * a
    r_t = (r * jnp.exp(cs)).astype(BF16)
    a_t = (-kk * jnp.exp(cs - logw)).astype(BF16)
    b_t = (b_raw * e_neg).astype(BF16)
    k_t = (k2 * e_neg).astype(BF16)
    b_p = (b_raw * e_tail).astype(BF16)
    k_p = (k2 * e_tail).astype(BF16)
    v_b = v.astype(BF16)

    def units_of(x):
        return [x[b * C:(b + 1) * C, i * G:(i + 1) * G] for b in range(NB) for i in range(n_groups)]

    rg, ag, bg, kg, vg, bpg, kpg = map(units_of, (r_t, a_t, b_t, k_t, v_b, b_p, k_p))
    units = range(NB * n_groups)
    stack = lambda p, q: jnp.concatenate([p, q], axis=0)
    lhs = [stack(ag[i], rg[i]) for i in units]
    sc = [_mm(lhs[i], stack(bd(bg[i]), bd(kg[i])), "nt") for i in units]
    a_ab = [jnp.where(strict_p, sc[i][:C, :G], 0.0) for i in units]
    a_rb = [jnp.where(incl_p, sc[i][C:, :G], 0.0) for i in units]
    a_ak = [jnp.where(strict_p, sc[i][:C, G:], 0.0) for i in units]
    a_rk = [jnp.where(incl_p, sc[i][C:, G:], 0.0) for i in units]

    inv = [eye_p + jnp.where(half_pair[0], a_ab[i], 0.0) for i in units]
    inv_b = [x.astype(BF16) for x in inv]
    for lvl in range(1, len(half_pair)):
        n_inv = [_mm(jnp.where(half_pair[lvl], a_ab[i], 0.0), bd(inv_b[i])) for i in units]
        inv = [inv[i] + _mm(inv_b[i], bd(n_inv[i].astype(BF16))) for i in units]
        inv_b = [x.astype(BF16) for x in inv]

    av_yv = [_mm(stack(a_ak[i], a_rk[i]), bd(vg[i])) for i in units]
    hat = [_mm(inv_b[i], jnp.concatenate([bd(ag[i]), bd(av_yv[i][:C].astype(BF16))], axis=1))
           for i in units]

    s0 = [state_ref[i] for i in units]
    res = [_mm(stack(hat[i][:, :G].astype(BF16), rg[i]), s0[i], "nt") for i in units]
    u = [(res[i][:C] + hat[i][:, G:]).astype(BF16) for i in units]
    upd = [_mm(stack(u[i], vg[i]), stack(bpg[i], kpg[i]), "tn") for i in units]
    for i in units:
        b, gidx = divmod(i, n_groups)
        state_ref[i] = (s0[i] * p_last[b][:, gidx * G:(gidx + 1) * G]
                        + jnp.where(same_head, upd[i], 0.0))
    y_units = [res[i][C:] + _mm(a_rb[i], bd(u[i])) + av_yv[i][C:] for i in units]

    y = jnp.concatenate([jnp.concatenate(y_units[b * n_groups:(b + 1) * n_groups], axis=1)
                         for b in range(NB)], axis=0)
    yc = y - head_sum(y, True) * (1.0 / HEAD_DIM)
    var = head_sum(yc * yc, False) * (1.0 / HEAD_DIM)
    yn = yc * lax.rsqrt(var + LNX_EPS) * pv(PV_LNG) + pv(PV_LNB)
    y_ref[...] = ((yn + bonus) * g).astype(BF16).reshape(NB, C, W)


def _rwkv_group(proj, v_first, pvec, wa2, g2, v2, layer, *, batch, nb):
    Lp = proj.shape[1]
    W = pvec.shape[2]
    C = WKV_CHUNK
    has_vres = v_first is not None
    tok = lambda col: pl.BlockSpec((nb, C, W), lambda bp, c: (bp, c, col))
    par = lambda arr: pl.BlockSpec((None,) + arr.shape[1:], lambda bp, c: (layer, 0, 0))
    lo_spec = pl.BlockSpec((nb, C, LO_W), lambda bp, c: (bp, c, 6 * W // LO_W))
    in_specs = [tok(3), tok(4), tok(5), lo_spec]
    args = [proj, proj, proj, proj]
    if has_vres:
        in_specs.append(tok(0))
        args.append(v_first)
    in_specs += [par(pvec), par(wa2), par(g2)]
    args += [pvec, wa2, g2]
    if has_vres:
        in_specs.append(par(v2))
        args.append(v2)
        out_shape = jax.ShapeDtypeStruct((batch, Lp, W), BF16)
        out_specs = tok(0)
    else:
        out_shape = (jax.ShapeDtypeStruct((batch, Lp, W), BF16),
                     jax.ShapeDtypeStruct((batch, Lp, W), F32))
        out_specs = (tok(0), tok(0))
    return pl.pallas_call(
        functools.partial(_wkv_kernel, has_vres),
        out_shape=out_shape,
        grid=(batch // nb, Lp // C),
        in_specs=in_specs,
        out_specs=out_specs,
        scratch_shapes=[
            pltpu.VMEM((nb * (W // LANES), LANES, LANES), F32),
            pltpu.VMEM((nb * SUBLANES, W), F32),
        ],
        compiler_params=_cparams(("parallel", "arbitrary")),
        name="rwkv_group",
    )(*args)


def _proj_out_kernel(yc_ref, yr_ref, w_ref, g_ref, h_ref, o_ref):
    half = yc_ref.shape[1]
    y = (jnp.dot(yc_ref[...], w_ref[0:half, :], preferred_element_type=F32)
         + jnp.dot(yr_ref[...], w_ref[half:2 * half, :], preferred_element_type=F32))
    o_ref[...] = h_ref[...] + _rms(y, g_ref[...])


def _proj_out(y_conv, y_rwkv, w_out, g, h, layer, *, tm):
    T, D = h.shape
    half = y_conv.shape[1]
    return pl.pallas_call(
        _proj_out_kernel,
        out_shape=jax.ShapeDtypeStruct((T, D), F32),
        grid=(T // tm,),
        in_specs=[
            pl.BlockSpec((tm, half), lambda i: (i, 0)),
            pl.BlockSpec((tm, half), lambda i: (i, 0)),
            pl.BlockSpec((None,) + w_out.shape[1:], lambda i: (layer, 0, 0)),
            _layer_vec(layer, D),
            pl.BlockSpec((tm, D), lambda i: (i, 0)),
        ],
        out_specs=pl.BlockSpec((tm, D), lambda i: (i, 0)),
        compiler_params=_cparams(("parallel",)),
        name="proj_out",
    )(y_conv, y_rwkv, w_out, g, h)


def _pad_axis(w, axis, size):
    pad = [(0, 0)] * w.ndim
    pad[axis] = (0, size - w.shape[axis])
    return jnp.pad(w, pad)


def _lora_cols(wd, ad, gd, vd):
    assert LO_AD == wd.shape[-1] and LO_GD == LO_AD + ad.shape[-1]
    return jnp.concatenate([wd, ad, _pad_axis(gd, -1, LO_GD_W), _pad_axis(vd, -1, LO_VD_W)], axis=-1)


def _first_layer_zero(w):
    return jnp.concatenate([jnp.zeros_like(w[:1]), w], axis=0)


def kernel(x, meta_tokens, ffn1_pre_g, ffn1_w_gu, ffn1_w_down, ffn1_post_g, mix_pre_g, w_in, w_in_vres, mu_rwkv, mu_vres, conv_w, conv_norm_g, decay_w0, decay_w2, iclr_a0, iclr_a2, vres_v0, vres_v2, gate_g2, k_k, k_a, r_k, lnx_g, lnx_b, w_out, mix_post_g, ffn2_pre_g, ffn2_w_gu, ffn2_w_down, ffn2_post_g):
    B, S, D = x.shape
    depth = w_in.shape[0]
    conv_width = conv_w.shape[2]
    rw = decay_w0.shape[1]
    L = N_META + S
    Lp = -(-L // WKV_CHUNK) * WKV_CHUNK
    T = B * Lp
    pick = lambda n, cands: next(c for c in cands if n % c == 0)
    tm = pick(T, (640, 512, 256, WKV_CHUNK))
    tm_proj = pick(T, (1280, 640, 512, 256, WKV_CHUNK))
    tf = pick(ffn1_w_down.shape[1], (512, 256, 128))
    tt_conv = pick(Lp, (832, 512, 256, WKV_CHUNK))
    nb = pick(B, (2, 1))
    o = 3 * conv_width + 3 * rw

    h = jnp.pad(x, ((0, 0), (N_META, Lp - L), (0, 0)))
    meta = jnp.broadcast_to(meta_tokens.astype(x.dtype)[None], (B, N_META, D))
    h = lax.dynamic_update_slice(h, meta, (0, 0, 0)).reshape(T, D)

    vec = lambda a: a.reshape(depth, 1, -1)
    ffn1 = (vec(ffn1_pre_g), ffn1_w_gu.astype(BF16), ffn1_w_down.astype(BF16), vec(0.5 * ffn1_post_g))
    ffn2 = (vec(ffn2_pre_g), ffn2_w_gu.astype(BF16), ffn2_w_down.astype(BF16), vec(0.5 * ffn2_post_g))
    w_main = w_in.astype(BF16)
    w_lo = _lora_cols(w_in[..., o:o + R_DECAY], w_in[..., o + R_DECAY:o + R_DECAY + R_ICLR],
                      w_in[..., o + R_DECAY + R_ICLR:], _first_layer_zero(w_in_vres)).astype(BF16)
    m = 3 * rw
    mu_lo = _lora_cols(mu_rwkv[:, m:m + R_DECAY], mu_rwkv[:, m + R_DECAY:m + R_DECAY + R_ICLR],
                       mu_rwkv[:, m + R_DECAY + R_ICLR:], _first_layer_zero(mu_vres))
    pvec = jnp.stack([decay_w0, iclr_a0, _first_layer_zero(vres_v0), k_k, k_a, r_k.reshape(depth, rw),
                      lnx_g, lnx_b, mu_rwkv[:, :rw], mu_rwkv[:, rw:2 * rw], mu_rwkv[:, 2 * rw:m],
                      _pad_axis(mu_lo, 1, rw)]
                     + [jnp.zeros((depth, rw), F32)] * (PV_ROWS - 12), axis=1)
    wa2 = jnp.concatenate([_pad_axis(decay_w2, 2, 2 * rw),
                           jnp.pad(iclr_a2, ((0, 0), (0, 0), (rw, 0)))], axis=1).astype(BF16)
    g2 = _pad_axis(gate_g2, 1, LO_GD_W).astype(BF16)
    v2 = _pad_axis(_first_layer_zero(vres_v2), 1, LO_VD_W).astype(BF16)
    w_out_b = w_out.astype(BF16)
    mix_pre, mix_post, conv_g = vec(mix_pre_g), vec(mix_post_g), vec(conv_norm_g)

    v_first = None
    for i in range(depth):
        h = _ffn(h, *ffn1, i, tm=tm, tf=tf)
        proj = _proj_in(h, mix_pre, w_main, o, w_lo, i, tm=tm_proj)
        y_conv = _conv_group(proj, conv_w, conv_g, i, batch=B, tt=tt_conv)
        proj3 = proj.reshape(B, Lp, proj.shape[1])
        if i == 0:
            y_rwkv, v_first = _rwkv_group(proj3, None, pvec, wa2, g2, None, i, batch=B, nb=nb)
        else:
            y_rwkv = _rwkv_group(proj3, v_first, pvec, wa2, g2, v2, i, batch=B, nb=nb)
        h = _proj_out(y_conv, y_rwkv.reshape(T, rw), w_out_b, mix_post, h, i, tm=tm)
        h = _ffn(h, *ffn2, i, tm=tm, tf=tf)
    return h.reshape(B, Lp, D)[:, N_META:L]
```

```python
import functools
import math

import jax
import jax.numpy as jnp
from jax import lax
from jax.experimental import pallas as pl
from jax.experimental.pallas import tpu as pltpu

F32 = jnp.float32
BF16 = jnp.bfloat16

NORM_EPS = 1e-6
LNX_EPS = 64e-5
N_META = 16
HEAD_DIM = 64
WKV_CHUNK = 64
R_DECAY, R_ICLR, R_GATE, R_VRES = 64, 64, 160, 32
LO_WD, LO_AD, LO_GD, LO_VD = 0, 64, 128, 384
LO_GD_W = 256
LO_VD_W = 128
LO_W = 512
LANES = 128
GROUP_HEADS = LANES // HEAD_DIM
SUBLANES = 8

VMEM_LIMIT_BYTES = 56 * 1024 * 1024


def _cparams(sem):
    return pltpu.CompilerParams(dimension_semantics=sem, vmem_limit_bytes=VMEM_LIMIT_BYTES)


def _rms(x, g):
    ms = jnp.mean(x * x, axis=-1, keepdims=True)
    return x * lax.rsqrt(ms + NORM_EPS) * g


def _layer_vec(layer, width):
    return pl.BlockSpec((None, 1, width), lambda *_: (layer, 0, 0))


def _ffn_kernel(h_ref, pre_g_ref, wg_ref, wu_ref, wd_ref, half_post_g_ref, o_ref, xn_ref):
    j = pl.program_id(2)

    @pl.when(j == 0)
    def _():
        xn_ref[...] = _rms(h_ref[...], pre_g_ref[...]).astype(BF16)
        o_ref[...] = jnp.zeros_like(o_ref)

    xn = xn_ref[...]
    gate = jnp.dot(xn, wg_ref[...], preferred_element_type=F32)
    up = jnp.dot(xn, wu_ref[...], preferred_element_type=F32)
    act = (gate * jax.nn.sigmoid(gate) * up).astype(BF16)
    o_ref[...] += jnp.dot(act, wd_ref[...], preferred_element_type=F32)

    @pl.when(j == pl.num_programs(2) - 1)
    def _():
        o_ref[...] = h_ref[...] + _rms(o_ref[...], half_post_g_ref[...])


def _ffn(h, pre_g, w_gu, w_down, half_post_g, layer, *, tm, tf, rows=None):
    T, D = h.shape
    FF = w_down.shape[1]
    nj = FF // tf
    period, start, count = rows or (T, 0, T)
    n_seq, per = T // period, count // tm
    if rows is None:
        h_spec = pl.BlockSpec((tm, D), lambda s, i, j: (i, 0))
    else:
        assert period % SUBLANES == 0 and start % SUBLANES == 0 and tm % SUBLANES == 0
        h_spec = pl.BlockSpec((pl.Element(tm), pl.Element(D)),
                              lambda s, i, j: (pl.multiple_of(s * period + start + i * tm, SUBLANES), 0))
    return pl.pallas_call(
        _ffn_kernel,
        out_shape=jax.ShapeDtypeStruct((n_seq * count, D), F32),
        grid=(n_seq, per, nj),
        in_specs=[
            h_spec,
            _layer_vec(layer, D),
            pl.BlockSpec((None, D, tf), lambda s, i, j: (layer, 0, j)),
            pl.BlockSpec((None, D, tf), lambda s, i, j: (layer, 0, j + nj)),
            pl.BlockSpec((None, tf, D), lambda s, i, j: (layer, j, 0)),
            _layer_vec(layer, D),
        ],
        out_specs=pl.BlockSpec((tm, D), lambda s, i, j: (s * per + i, 0)),
        scratch_shapes=[pltpu.VMEM((tm, D), BF16)],
        compiler_params=_cparams(("parallel", "parallel", "arbitrary")),
        name="ffn",
    )(h, pre_g, w_gu, w_gu, w_down, half_post_g)


def _proj_in_kernel(h_ref, g_ref, w_ref, w_lo_ref, o_ref, o_lo_ref, xn_ref):
    j = pl.program_id(1)
    last = pl.num_programs(1) - 1

    @pl.when(j == 0)
    def _():
        xn_ref[...] = _rms(h_ref[...], g_ref[...]).astype(BF16)

    @pl.when(j < last)
    def _():
        o_ref[...] = jnp.dot(xn_ref[...], w_ref[...], preferred_element_type=F32)

    @pl.when(j == last)
    def _():
        o_lo_ref[...] = jnp.dot(xn_ref[...], w_lo_ref[...], preferred_element_type=F32)


def _proj_in(h, g, w_main, n_main, w_lo, layer, *, tm, tn):
    T, D = h.shape
    n_lo = w_lo.shape[2]
    n_tiles = n_main // tn
    main_col = lambda j: jnp.minimum(j, n_tiles - 1)
    return pl.pallas_call(
        _proj_in_kernel,
        out_shape=(jax.ShapeDtypeStruct((T, n_main), F32), jax.ShapeDtypeStruct((T, n_lo), F32)),
        grid=(T // tm, n_tiles + 1),
        in_specs=[
            pl.BlockSpec((tm, D), lambda i, j: (i, 0)),
            _layer_vec(layer, D),
            pl.BlockSpec((None, D, tn), lambda i, j: (layer, 0, main_col(j))),
            pl.BlockSpec((None, D, n_lo), lambda i, j: (layer, 0, 0)),
        ],
        out_specs=(pl.BlockSpec((tm, tn), lambda i, j: (i, main_col(j))),
                   pl.BlockSpec((tm, n_lo), lambda i, j: (i, 0))),
        scratch_shapes=[pltpu.VMEM((tm, D), BF16)],
        compiler_params=_cparams(("parallel", "arbitrary")),
        name="proj_in",
    )(h, g, w_main, w_lo)


def _conv_kernel(gb_ref, gc_ref, u_ref, cw_ref, cg_ref, o_ref, ext_ref):
    tt = u_ref.shape[0]

    @pl.when(pl.program_id(1) == 0)
    def _():
        ext_ref[0:8, :] = jnp.zeros((8, ext_ref.shape[1]), F32)

    uc = gc_ref[...] * u_ref[...]
    ext_ref[8:8 + tt, :] = uc
    conv = (cw_ref[0:1, :] * ext_ref[6:6 + tt, :] + cw_ref[1:2, :] * ext_ref[7:7 + tt, :]
            + cw_ref[2:3, :] * uc)
    o_ref[...] = _rms(gb_ref[...] * conv, cg_ref[...]).astype(BF16)
    ext_ref[0:8, :] = ext_ref[tt:tt + 8, :]


def _conv_group(proj, conv_w, conv_g, layer, *, batch, tt):
    T = proj.shape[0]
    K, W = conv_w.shape[1:]
    nt = T // batch // tt
    row = lambda b, t: b * nt + t
    return pl.pallas_call(
        _conv_kernel,
        out_shape=jax.ShapeDtypeStruct((T, W), BF16),
        grid=(batch, nt),
        in_specs=[
            pl.BlockSpec((tt, W), lambda b, t: (row(b, t), 0)),
            pl.BlockSpec((tt, W), lambda b, t: (row(b, t), 1)),
            pl.BlockSpec((tt, W), lambda b, t: (row(b, t), 2)),
            pl.BlockSpec((None, K, W), lambda b, t: (layer, 0, 0)),
            _layer_vec(layer, W),
        ],
        out_specs=pl.BlockSpec((tt, W), lambda b, t: (row(b, t), 0)),
        scratch_shapes=[pltpu.VMEM((tt + 8, W), F32)],
        compiler_params=_cparams(("parallel", "arbitrary")),
        name="conv_group",
    )(proj, proj, proj, conv_w, conv_g)


PV_W0, PV_A0, PV_V0, PV_KK, PV_KA, PV_RK, PV_LNG, PV_LNB, PV_MUR, PV_MUK, PV_MUV, PV_MULO = range(12)
PV_ROWS = 16

_DIMS = {"nn": ((1,), (0,)), "nt": ((1,), (1,)), "tn": ((0,), (0,))}


def _split(x):
    hi = x.astype(BF16)
    return hi, (x - hi.astype(F32)).astype(BF16)


def _mm(x, w, form="nn", x_hp=False):
    dot = lambda p, q: lax.dot_general(p, q, (_DIMS[form], ((), ())), preferred_element_type=F32)
    w = w.astype(BF16)
    if not x_hp:
        return dot(x.astype(BF16), w)
    hi, lo = _split(x)
    return dot(hi, w) + dot(lo, w)


def _wkv_kernel(has_vres, *refs):
    if has_vres:
        (r_ref, k_ref, v_ref, lo_ref, vf_ref, pv_ref, wa2_ref, g2_ref, v2_ref,
         y_ref, state_ref, carry_ref) = refs
    else:
        (r_ref, k_ref, v_ref, lo_ref, pv_ref, wa2_ref, g2_ref,
         y_ref, vf_out_ref, state_ref, carry_ref) = refs
    NB, C, W = r_ref.shape
    R = NB * C
    G = LANES
    n_groups = W // G

    @pl.when(pl.program_id(1) == 0)
    def _():
        state_ref[...] = jnp.zeros_like(state_ref)
        carry_ref[...] = jnp.zeros_like(carry_ref)

    def pv(i):
        return pv_ref[i:i + 1, :]

    def token_shift(x_ref, slot, mu):
        width = x_ref.shape[-1]
        first = lax.broadcasted_iota(jnp.int32, (C, width), 0) == 0
        parts = []
        for b in range(NB):
            x = x_ref[b]
            crow = b * SUBLANES + slot
            prev = jnp.where(first, carry_ref[crow:crow + 1, :width], pltpu.roll(x, 1, 0))
            carry_ref[crow:crow + 1, :width] = x[C - 1:C, :]
            parts.append(x + mu[:, :width] * (prev - x))
        return jnp.concatenate(parts, axis=0)

    r = token_shift(r_ref, 0, pv(PV_MUR))
    k = token_shift(k_ref, 1, pv(PV_MUK))
    v = token_shift(v_ref, 2, pv(PV_MUV))
    lo = token_shift(lo_ref, 3, pv(PV_MULO))

    wa = lo[:, LO_WD:LO_WD + LANES]
    wa = jnp.where(lax.broadcasted_iota(jnp.int32, wa.shape, 1) < LO_AD, jnp.tanh(wa), wa)
    gd = lo[:, LO_GD:LO_GD + LO_GD_W]
    if has_vres:
        vd = lo[:, LO_VD:LO_VD + LO_VD_W]
        mix = jax.nn.sigmoid(pv(PV_V0) + _mm(vd, v2_ref[...]))
        v = v + (vf_ref[...].reshape(R, W) - v) * mix
    else:
        vf_out_ref[...] = v.reshape(NB, C, W)

    za = _mm(wa, wa2_ref[...])
    logw = -math.exp(-0.5) * jax.nn.sigmoid(pv(PV_W0) + za[:, :W])
    a = jax.nn.sigmoid(pv(PV_A0) + za[:, W:])
    g = _mm(jax.nn.sigmoid(gd), g2_ref[...])
    kk_raw = k * pv(PV_KK)
    k2 = k * (1.0 + (a - 1.0) * pv(PV_KA))

    gi = lax.broadcasted_iota(jnp.int32, (G, G), 0)
    gj = lax.broadcasted_iota(jnp.int32, (G, G), 1)
    same_head = (gi // HEAD_DIM) == (gj // HEAD_DIM)
    bd_mask = same_head.astype(F32).astype(BF16)
    ti = lax.broadcasted_iota(jnp.int32, (C, G), 0)
    si = lax.broadcasted_iota(jnp.int32, (C, G), 1) % HEAD_DIM
    incl_p, strict_p = ti >= si, ti > si
    eye_p = (ti == si).astype(F32)
    half_pair = []
    blk = 2
    while blk <= C:
        half_pair.append(((ti // blk) == (si // blk)) & ((ti // (blk // 2)) != (si // (blk // 2))))
        blk *= 2

    def bd(y):
        return jnp.concatenate([y] * GROUP_HEADS, axis=0) * bd_mask

    def group_cols(x):
        return [x[:, i * G:(i + 1) * G] for i in range(n_groups)]

    def head_sum(x, x_hp):
        s = _mm(jnp.concatenate(group_cols(x), axis=0), bd_mask, x_hp=x_hp)
        return jnp.concatenate([s[i * R:(i + 1) * R] for i in range(n_groups)], axis=1)

    kk = kk_raw * jnp.minimum(lax.rsqrt(head_sum(kk_raw * kk_raw, False)), 1e12)
    bonus = head_sum(r * k2 * pv(PV_RK), True) * v

    tc = lax.broadcasted_iota(jnp.int32, (C, C), 0)
    sc = lax.broadcasted_iota(jnp.int32, (C, C), 1)
    tri = (tc >= sc).astype(F32).astype(BF16)
    tri3 = jnp.concatenate([tri, tri, tri], axis=1)
    l_hi = logw.astype(BF16)
    l_rest = logw - l_hi.astype(F32)
    l_mid = l_rest.astype(BF16)
    l_lo = (l_rest - l_mid.astype(F32)).astype(BF16)
    seq_rows = lambda x, b: x[b * C:(b + 1) * C]
    cs_b = [jnp.dot(tri3, jnp.concatenate([seq_rows(l_hi, b), seq_rows(l_mid, b), seq_rows(l_lo, b)],
                                          axis=0), preferred_element_type=F32) for b in range(NB)]
    cs = jnp.concatenate(cs_b, axis=0)
    cs_last = [c[C - 1:C, :] for c in cs_b]
    cs_end = jnp.concatenate([jnp.broadcast_to(c, (C, W)) for c in cs_last], axis=0)
    e_neg = jnp.exp(-cs)
    e_tail = jnp.exp(cs_end - cs)
    p_last = [jnp.exp(c) for c in cs_last]

    b_raw = kk * a
    r_t = (r * jnp.exp(cs)).astype(BF16)
    a_t = (-kk * jnp.exp(cs - logw)).astype(BF16)
    b_t = (b_raw * e_neg).astype(BF16)
    k_t = (k2 * e_neg).astype(BF16)
    b_p = (b_raw * e_tail).astype(BF16)
    k_p = (k2 * e_tail).astype(BF16)
    v_b = v.astype(BF16)

    def units_of(x):
        return [x[b * C:(b + 1) * C, i * G:(i + 1) * G] for b in range(NB) for i in range(n_groups)]

    rg, ag, bg, kg, vg, bpg, kpg = map(units_of, (r_t, a_t, b_t, k_t, v_b, b_p, k_p))
    units = range(NB * n_groups)
    stack = lambda p, q: jnp.concatenate([p, q], axis=0)
    lhs = [stack(ag[i], rg[i]) for i in units]
    sc = [_mm(lhs[i], stack(bd(bg[i]), bd(kg[i])), "nt") for i in units]
    a_ab = [jnp.where(strict_p, sc[i][:C, :G], 0.0) for i in units]
    a_rb = [jnp.where(incl_p, sc[i][C:, :G], 0.0) for i in units]
    a_ak = [jnp.where(strict_p, sc[i][:C, G:], 0.0) for i in units]
    a_rk = [jnp.where(incl_p, sc[i][C:, G:], 0.0) for i in units]

    inv = [eye_p + jnp.where(half_pair[0], a_ab[i], 0.0) for i in units]
    inv_b = [x.astype(BF16) for x in inv]
    for lvl in range(1, len(half_pair)):
        n_inv = [_mm(jnp.where(half_pair[lvl], a_ab[i], 0.0), bd(inv_b[i])) for i in units]
        inv = [inv[i] + _mm(inv_b[i], bd(n_inv[i].astype(BF16))) for i in units]
        inv_b = [x.astype(BF16) for x in inv]

    av_yv = [_mm(stack(a_ak[i], a_rk[i]), bd(vg[i])) for i in units]
    hat = [_mm(inv_b[i], jnp.concatenate([bd(ag[i]), bd(av_yv[i][:C].astype(BF16))], axis=1))
           for i in units]

    s0 = [state_ref[i] for i in units]
    res = [_mm(stack(hat[i][:, :G].astype(BF16), rg[i]), s0[i], "nt") for i in units]
    u = [(res[i][:C] + hat[i][:, G:]).astype(BF16) for i in units]
    upd = [_mm(stack(u[i], vg[i]), stack(bpg[i], kpg[i]), "tn") for i in units]
    for i in units:
        b, gidx = divmod(i, n_groups)
        state_ref[i] = (s0[i] * p_last[b][:, gidx * G:(gidx + 1) * G]
                        + jnp.where(same_head, upd[i], 0.0))
    y_units = [res[i][C:] + _mm(a_rb[i], bd(u[i])) + av_yv[i][C:] for i in units]

    y = jnp.concatenate([jnp.concatenate(y_units[b * n_groups:(b + 1) * n_groups], axis=1)
                         for b in range(NB)], axis=0)
    yc = y - head_sum(y, True) * (1.0 / HEAD_DIM)
    var = head_sum(yc * yc, False) * (1.0 / HEAD_DIM)
    yn = yc * lax.rsqrt(var + LNX_EPS) * pv(PV_LNG) + pv(PV_LNB)
    y_ref[...] = ((yn + bonus) * g).astype(BF16).reshape(NB, C, W)


def _rwkv_group(proj, proj_lo, v_first, pvec, wa2, g2, v2, layer, *, batch, nb):
    Lp = proj.shape[1]
    W = pvec.shape[2]
    C = WKV_CHUNK
    has_vres = v_first is not None
    tok = lambda col: pl.BlockSpec((nb, C, W), lambda bp, c: (bp, c, col))
    par = lambda arr: pl.BlockSpec((None,) + arr.shape[1:], lambda bp, c: (layer, 0, 0))
    in_specs = [tok(3), tok(4), tok(5), pl.BlockSpec((nb, C, LO_W), lambda bp, c: (bp, c, 0))]
    args = [proj, proj, proj, proj_lo]
    if has_vres:
        in_specs.append(tok(0))
        args.append(v_first)
    in_specs += [par(pvec), par(wa2), par(g2)]
    args += [pvec, wa2, g2]
    if has_vres:
        in_specs.append(par(v2))
        args.append(v2)
        out_shape = jax.ShapeDtypeStruct((batch, Lp, W), BF16)
        out_specs = tok(0)
    else:
        out_shape = (jax.ShapeDtypeStruct((batch, Lp, W), BF16),
                     jax.ShapeDtypeStruct((batch, Lp, W), F32))
        out_specs = (tok(0), tok(0))
    return pl.pallas_call(
        functools.partial(_wkv_kernel, has_vres),
        out_shape=out_shape,
        grid=(batch // nb, Lp // C),
        in_specs=in_specs,
        out_specs=out_specs,
        scratch_shapes=[
            pltpu.VMEM((nb * (W // LANES), LANES, LANES), F32),
            pltpu.VMEM((nb * SUBLANES, W), F32),
        ],
        compiler_params=_cparams(("parallel", "arbitrary")),
        name="rwkv_group",
    )(*args)


def _proj_out_kernel(yc_ref, yr_ref, w_ref, g_ref, h_ref, o_ref):
    half = yc_ref.shape[1]
    y = (jnp.dot(yc_ref[...], w_ref[0:half, :], preferred_element_type=F32)
         + jnp.dot(yr_ref[...], w_ref[half:2 * half, :], preferred_element_type=F32))
    o_ref[...] = h_ref[...] + _rms(y, g_ref[...])


def _proj_out(y_conv, y_rwkv, w_out, g, h, layer, *, tm):
    T, D = h.shape
    half = y_conv.shape[1]
    return pl.pallas_call(
        _proj_out_kernel,
        out_shape=jax.ShapeDtypeStruct((T, D), F32),
        grid=(T // tm,),
        in_specs=[
            pl.BlockSpec((tm, half), lambda i: (i, 0)),
            pl.BlockSpec((tm, half), lambda i: (i, 0)),
            pl.BlockSpec((None,) + w_out.shape[1:], lambda i: (layer, 0, 0)),
            _layer_vec(layer, D),
            pl.BlockSpec((tm, D), lambda i: (i, 0)),
        ],
        out_specs=pl.BlockSpec((tm, D), lambda i: (i, 0)),
        compiler_params=_cparams(("parallel",)),
        name="proj_out",
    )(y_conv, y_rwkv, w_out, g, h)


def _pad_axis(w, axis, size):
    pad = [(0, 0)] * w.ndim
    pad[axis] = (0, size - w.shape[axis])
    return jnp.pad(w, pad)


def _lora_cols(wd, ad, gd, vd):
    assert LO_AD == wd.shape[-1] and LO_GD == LO_AD + ad.shape[-1]
    return jnp.concatenate([wd, ad, _pad_axis(gd, -1, LO_GD_W), _pad_axis(vd, -1, LO_VD_W)], axis=-1)


def _first_layer_zero(w):
    return jnp.concatenate([jnp.zeros_like(w[:1]), w], axis=0)


def kernel(x, meta_tokens, ffn1_pre_g, ffn1_w_gu, ffn1_w_down, ffn1_post_g, mix_pre_g, w_in, w_in_vres, mu_rwkv, mu_vres, conv_w, conv_norm_g, decay_w0, decay_w2, iclr_a0, iclr_a2, vres_v0, vres_v2, gate_g2, k_k, k_a, r_k, lnx_g, lnx_b, w_out, mix_post_g, ffn2_pre_g, ffn2_w_gu, ffn2_w_down, ffn2_post_g):
    B, S, D = x.shape
    depth = w_in.shape[0]
    conv_width = conv_w.shape[2]
    rw = decay_w0.shape[1]
    L = N_META + S
    Lp = -(-L // WKV_CHUNK) * WKV_CHUNK
    T = B * Lp
    pick = lambda n, cands: next(c for c in cands if n % c == 0)
    tm = pick(T, (640, 512, 256, WKV_CHUNK))
    tm_proj = pick(T, (832, 640, 512, 256, WKV_CHUNK))
    tf = pick(ffn1_w_down.shape[1], (512, 256, 128))
    tt_conv = pick(Lp, (832, 512, 256, WKV_CHUNK))
    nb = pick(B, (2, 1))
    o = 3 * conv_width + 3 * rw

    h = jnp.pad(x, ((0, 0), (N_META, Lp - L), (0, 0)))
    meta = jnp.broadcast_to(meta_tokens.astype(x.dtype)[None], (B, N_META, D))
    h = lax.dynamic_update_slice(h, meta, (0, 0, 0)).reshape(T, D)

    vec = lambda a: a.reshape(depth, 1, -1)
    ffn1 = (vec(ffn1_pre_g), ffn1_w_gu.astype(BF16), ffn1_w_down.astype(BF16), vec(0.5 * ffn1_post_g))
    ffn2 = (vec(ffn2_pre_g), ffn2_w_gu.astype(BF16), ffn2_w_down.astype(BF16), vec(0.5 * ffn2_post_g))
    w_main = w_in.astype(BF16)
    w_lo = _lora_cols(w_in[..., o:o + R_DECAY], w_in[..., o + R_DECAY:o + R_DECAY + R_ICLR],
                      w_in[..., o + R_DECAY + R_ICLR:], _first_layer_zero(w_in_vres)).astype(BF16)
    m = 3 * rw
    mu_lo = _lora_cols(mu_rwkv[:, m:m + R_DECAY], mu_rwkv[:, m + R_DECAY:m + R_DECAY + R_ICLR],
                       mu_rwkv[:, m + R_DECAY + R_ICLR:], _first_layer_zero(mu_vres))
    pvec = jnp.stack([decay_w0, iclr_a0, _first_layer_zero(vres_v0), k_k, k_a, r_k.reshape(depth, rw),
                      lnx_g, lnx_b, mu_rwkv[:, :rw], mu_rwkv[:, rw:2 * rw], mu_rwkv[:, 2 * rw:m],
                      _pad_axis(mu_lo, 1, rw)]
                     + [jnp.zeros((depth, rw), F32)] * (PV_ROWS - 12), axis=1)
    wa2 = jnp.concatenate([_pad_axis(decay_w2, 2, 2 * rw),
                           jnp.pad(iclr_a2, ((0, 0), (0, 0), (rw, 0)))], axis=1).astype(BF16)
    g2 = _pad_axis(gate_g2, 1, LO_GD_W).astype(BF16)
    v2 = _pad_axis(_first_layer_zero(vres_v2), 1, LO_VD_W).astype(BF16)
    w_out_b = w_out.astype(BF16)
    mix_pre, mix_post, conv_g = vec(mix_pre_g), vec(mix_post_g), vec(conv_norm_g)

    v_first = None
    for i in range(depth):
        h = _ffn(h, *ffn1, i, tm=tm, tf=tf)
        proj, proj_lo = _proj_in(h, mix_pre, w_main, o, w_lo, i, tm=tm_proj, tn=rw)
        y_conv = _conv_group(proj, conv_w, conv_g, i, batch=B, tt=tt_conv)
        proj3, proj_lo3 = proj.reshape(B, Lp, o), proj_lo.reshape(B, Lp, LO_W)
        if i == 0:
            y_rwkv, v_first = _rwkv_group(proj3, proj_lo3, None, pvec, wa2, g2, None, i, batch=B, nb=nb)
        else:
            y_rwkv = _rwkv_group(proj3, proj_lo3, v_first, pvec, wa2, g2, v2, i, batch=B, nb=nb)
        h = _proj_out(y_conv, y_rwkv.reshape(T, rw), w_out_b, mix_post, h, i, tm=tm)
        if i < depth - 1:
            h = _ffn(h, *ffn2, i, tm=tm, tf=tf)
    out = _ffn(h, *ffn2, depth - 1, tm=pick(S, (512, 256, WKV_CHUNK)), tf=tf, rows=(Lp, N_META, S))
    return out.reshape(B, S, D)
```

```python
import functools
import math

import jax
import jax.numpy as jnp
from jax import lax
from jax.experimental import pallas as pl
from jax.experimental.pallas import tpu as pltpu

F32 = jnp.float32
BF16 = jnp.bfloat16

NORM_EPS = 1e-6
LNX_EPS = 64e-5
N_META = 16
HEAD_DIM = 64
WKV_CHUNK = 64
R_DECAY, R_ICLR, R_GATE, R_VRES = 64, 64, 160, 32
LO_WD, LO_AD, LO_GD, LO_VD = 0, 64, 128, 384
LO_GD_W = 256
LO_VD_W = 128
LO_W = 512
LANES = 128
GROUP_HEADS = LANES // HEAD_DIM
SUBLANES = 8

VMEM_LIMIT_BYTES = 56 * 1024 * 1024


def _cparams(sem):
    return pltpu.CompilerParams(dimension_semantics=sem, vmem_limit_bytes=VMEM_LIMIT_BYTES)


def _rms(x, g):
    ms = jnp.mean(x * x, axis=-1, keepdims=True)
    return x * lax.rsqrt(ms + NORM_EPS) * g


def _layer_vec(layer, width):
    return pl.BlockSpec((None, 1, width), lambda *_: (layer, 0, 0))


def _ffn_kernel(h_ref, pre_g_ref, wg_ref, wu_ref, wd_ref, half_post_g_ref, o_ref, xn_ref):
    j = pl.program_id(2)

    @pl.when(j == 0)
    def _():
        xn_ref[...] = _rms(h_ref[...], pre_g_ref[...]).astype(BF16)
        o_ref[...] = jnp.zeros_like(o_ref)

    xn = xn_ref[...]
    gate = jnp.dot(xn, wg_ref[...], preferred_element_type=F32)
    up = jnp.dot(xn, wu_ref[...], preferred_element_type=F32)
    act = (gate * jax.nn.sigmoid(gate) * up).astype(BF16)
    o_ref[...] += jnp.dot(act, wd_ref[...], preferred_element_type=F32)

    @pl.when(j == pl.num_programs(2) - 1)
    def _():
        o_ref[...] = h_ref[...] + _rms(o_ref[...], half_post_g_ref[...])


def _ffn(h, pre_g, w_gu, w_down, half_post_g, layer, *, tm, tf, rows=None):
    T, D = h.shape
    FF = w_down.shape[1]
    nj = FF // tf
    period, start, count = rows or (T, 0, T)
    n_seq, per = T // period, count // tm
    if rows is None:
        h_spec = pl.BlockSpec((tm, D), lambda s, i, j: (i, 0))
    else:
        assert period % SUBLANES == 0 and start % SUBLANES == 0 and tm % SUBLANES == 0
        h_spec = pl.BlockSpec((pl.Element(tm), pl.Element(D)),
                              lambda s, i, j: (pl.multiple_of(s * period + start + i * tm, SUBLANES), 0))
    return pl.pallas_call(
        _ffn_kernel,
        out_shape=jax.ShapeDtypeStruct((n_seq * count, D), F32),
        grid=(n_seq, per, nj),
        in_specs=[
            h_spec,
            _layer_vec(layer, D),
            pl.BlockSpec((None, D, tf), lambda s, i, j: (layer, 0, j)),
            pl.BlockSpec((None, D, tf), lambda s, i, j: (layer, 0, j + nj)),
            pl.BlockSpec((None, tf, D), lambda s, i, j: (layer, j, 0)),
            _layer_vec(layer, D),
        ],
        out_specs=pl.BlockSpec((tm, D), lambda s, i, j: (s * per + i, 0)),
        scratch_shapes=[pltpu.VMEM((tm, D), BF16)],
        compiler_params=_cparams(("parallel", "parallel", "arbitrary")),
        name="ffn",
    )(h, pre_g, w_gu, w_gu, w_down, half_post_g)


def _proj_in_kernel(h_ref, g_ref, w_ref, w_lo_ref, o_ref, o_lo_ref, xn_ref):
    j = pl.program_id(1)
    last = pl.num_programs(1) - 1

    @pl.when(j == 0)
    def _():
        xn_ref[...] = _rms(h_ref[...], g_ref[...]).astype(BF16)

    @pl.when(j < last)
    def _():
        o_ref[...] = jnp.dot(xn_ref[...], w_ref[...], preferred_element_type=F32)

    @pl.when(j == last)
    def _():
        o_lo_ref[...] = jnp.dot(xn_ref[...], w_lo_ref[...], preferred_element_type=F32)


def _proj_in(h, g, w_main, n_main, w_lo, layer, *, tm, tn):
    T, D = h.shape
    n_lo = w_lo.shape[2]
    n_tiles = n_main // tn
    main_col = lambda j: jnp.minimum(j, n_tiles - 1)
    return pl.pallas_call(
        _proj_in_kernel,
        out_shape=(jax.ShapeDtypeStruct((T, n_main), F32), jax.ShapeDtypeStruct((T, n_lo), F32)),
        grid=(T // tm, n_tiles + 1),
        in_specs=[
            pl.BlockSpec((tm, D), lambda i, j: (i, 0)),
            _layer_vec(layer, D),
            pl.BlockSpec((None, D, tn), lambda i, j: (layer, 0, main_col(j))),
            pl.BlockSpec((None, D, n_lo), lambda i, j: (layer, 0, 0)),
        ],
        out_specs=(pl.BlockSpec((tm, tn), lambda i, j: (i, main_col(j))),
                   pl.BlockSpec((tm, n_lo), lambda i, j: (i, 0))),
        scratch_shapes=[pltpu.VMEM((tm, D), BF16)],
        compiler_params=_cparams(("parallel", "arbitrary")),
        name="proj_in",
    )(h, g, w_main, w_lo)


def _conv_kernel(gb_ref, gc_ref, u_ref, cw_ref, cg_ref, o_ref, ext_ref):
    tt = u_ref.shape[0]

    @pl.when(pl.program_id(1) == 0)
    def _():
        ext_ref[0:8, :] = jnp.zeros((8, ext_ref.shape[1]), F32)

    uc = gc_ref[...] * u_ref[...]
    ext_ref[8:8 + tt, :] = uc
    conv = (cw_ref[0:1, :] * ext_ref[6:6 + tt, :] + cw_ref[1:2, :] * ext_ref[7:7 + tt, :]
            + cw_ref[2:3, :] * uc)
    o_ref[...] = _rms(gb_ref[...] * conv, cg_ref[...]).astype(BF16)
    ext_ref[0:8, :] = ext_ref[tt:tt + 8, :]


def _conv_group(proj, conv_w, conv_g, layer, *, batch, tt):
    T = proj.shape[0]
    K, W = conv_w.shape[1:]
    nt = T // batch // tt
    row = lambda b, t: b * nt + t
    return pl.pallas_call(
        _conv_kernel,
        out_shape=jax.ShapeDtypeStruct((T, W), BF16),
        grid=(batch, nt),
        in_specs=[
            pl.BlockSpec((tt, W), lambda b, t: (row(b, t), 0)),
            pl.BlockSpec((tt, W), lambda b, t: (row(b, t), 1)),
            pl.BlockSpec((tt, W), lambda b, t: (row(b, t), 2)),
            pl.BlockSpec((None, K, W), lambda b, t: (layer, 0, 0)),
            _layer_vec(layer, W),
        ],
        out_specs=pl.BlockSpec((tt, W), lambda b, t: (row(b, t), 0)),
        scratch_shapes=[pltpu.VMEM((tt + 8, W), F32)],
        compiler_params=_cparams(("parallel", "arbitrary")),
        name="conv_group",
    )(proj, proj, proj, conv_w, conv_g)


PV_W0, PV_A0, PV_V0, PV_KK, PV_KA, PV_RK, PV_LNG, PV_LNB, PV_MUR, PV_MUK, PV_MUV, PV_MULO = range(12)
PV_ROWS = 16

_DIMS = {"nn": ((1,), (0,)), "nt": ((1,), (1,)), "tn": ((0,), (0,))}


def _split(x):
    hi = x.astype(BF16)
    return hi, (x - hi.astype(F32)).astype(BF16)


def _mm(x, w, form="nn", x_hp=False):
    dot = lambda p, q: lax.dot_general(p, q, (_DIMS[form], ((), ())), preferred_element_type=F32)
    w = w.astype(BF16)
    if not x_hp:
        return dot(x.astype(BF16), w)
    hi, lo = _split(x)
    return dot(hi, w) + dot(lo, w)


def _wkv_kernel(has_vres, *refs):
    if has_vres:
        (r_ref, k_ref, v_ref, lo_ref, vf_ref, pv_ref, wa2_ref, g2_ref, v2_ref,
         y_ref, state_ref, carry_ref) = refs
    else:
        (r_ref, k_ref, v_ref, lo_ref, pv_ref, wa2_ref, g2_ref,
         y_ref, vf_out_ref, state_ref, carry_ref) = refs
    NB, C, W = r_ref.shape
    R = NB * C
    G = LANES
    n_groups = W // G

    @pl.when(pl.program_id(1) == 0)
    def _():
        state_ref[...] = jnp.zeros_like(state_ref)
        carry_ref[...] = jnp.zeros_like(carry_ref)

    def pv(i):
        return pv_ref[i:i + 1, :]

    def token_shift(x_ref, slot, mu):
        width = x_ref.shape[-1]
        first = lax.broadcasted_iota(jnp.int32, (C, width), 0) == 0
        parts = []
        for b in range(NB):
            x = x_ref[b]
            crow = b * SUBLANES + slot
            prev = jnp.where(first, carry_ref[crow:crow + 1, :width], pltpu.roll(x, 1, 0))
            carry_ref[crow:crow + 1, :width] = x[C - 1:C, :]
            parts.append(x + mu[:, :width] * (prev - x))
        return jnp.concatenate(parts, axis=0)

    r = token_shift(r_ref, 0, pv(PV_MUR))
    k = token_shift(k_ref, 1, pv(PV_MUK))
    v = token_shift(v_ref, 2, pv(PV_MUV))
    lo = token_shift(lo_ref, 3, pv(PV_MULO))

    wa = lo[:, LO_WD:LO_WD + LANES]
    wa = jnp.where(lax.broadcasted_iota(jnp.int32, wa.shape, 1) < LO_AD, jnp.tanh(wa), wa)
    gd = lo[:, LO_GD:LO_GD + LO_GD_W]
    if has_vres:
        vd = lo[:, LO_VD:LO_VD + LO_VD_W]
        mix = jax.nn.sigmoid(pv(PV_V0) + _mm(vd, v2_ref[...]))
        v = v + (vf_ref[...].reshape(R, W) - v) * mix
    else:
        vf_out_ref[...] = v.reshape(NB, C, W)

    za = _mm(wa, wa2_ref[...])
    logw = -math.exp(-0.5) * jax.nn.sigmoid(pv(PV_W0) + za[:, :W])
    a = jax.nn.sigmoid(pv(PV_A0) + za[:, W:])
    g = _mm(jax.nn.sigmoid(gd), g2_ref[...])
    kk_raw = k * pv(PV_KK)
    k2 = k * (1.0 + (a - 1.0) * pv(PV_KA))

    gi = lax.broadcasted_iota(jnp.int32, (G, G), 0)
    gj = lax.broadcasted_iota(jnp.int32, (G, G), 1)
    same_head = (gi // HEAD_DIM) == (gj // HEAD_DIM)
    bd_mask = same_head.astype(F32).astype(BF16)
    ti = lax.broadcasted_iota(jnp.int32, (C, G), 0)
    si = lax.broadcasted_iota(jnp.int32, (C, G), 1) % HEAD_DIM
    incl_p, strict_p = ti >= si, ti > si
    eye_p = (ti == si).astype(F32)
    half_pair = []
    blk = 2
    while blk <= C:
        half_pair.append(((ti // blk) == (si // blk)) & ((ti // (blk // 2)) != (si // (blk // 2))))
        blk *= 2

    def bd(y):
        return jnp.concatenate([y] * GROUP_HEADS, axis=0) * bd_mask

    def group_cols(x):
        return [x[:, i * G:(i + 1) * G] for i in range(n_groups)]

    def head_sum(x, x_hp):
        s = _mm(jnp.concatenate(group_cols(x), axis=0), bd_mask, x_hp=x_hp)
        return jnp.concatenate([s[i * R:(i + 1) * R] for i in range(n_groups)], axis=1)

    kk = kk_raw * jnp.minimum(lax.rsqrt(head_sum(kk_raw * kk_raw, False)), 1e12)
    bonus = head_sum(r * k2 * pv(PV_RK), True) * v

    tc = lax.broadcasted_iota(jnp.int32, (C, C), 0)
    sc = lax.broadcasted_iota(jnp.int32, (C, C), 1)
    tri = (tc >= sc).astype(F32).astype(BF16)
    tri3 = jnp.concatenate([tri, tri, tri], axis=1)
    l_hi = logw.astype(BF16)
    l_rest = logw - l_hi.astype(F32)
    l_mid = l_rest.astype(BF16)
    l_lo = (l_rest - l_mid.astype(F32)).astype(BF16)
    seq_rows = lambda x, b: x[b * C:(b + 1) * C]
    cs_b = [jnp.dot(tri3, jnp.concatenate([seq_rows(l_hi, b), seq_rows(l_mid, b), seq_rows(l_lo, b)],
                                          axis=0), preferred_element_type=F32) for b in range(NB)]
    cs = jnp.concatenate(cs_b, axis=0)
    cs_last = [c[C - 1:C, :] for c in cs_b]
    cs_end = jnp.concatenate([jnp.broadcast_to(c, (C, W)) for c in cs_last], axis=0)
    e_neg = jnp.exp(-cs)
    e_tail = jnp.exp(cs_end - cs)
    p_last = [jnp.exp(c) for c in cs_last]

    b_raw = kk * a
    r_t = (r * jnp.exp(cs)).astype(BF16)
    a_t = (-kk * jnp.exp(cs - logw)).astype(BF16)
    b_t = (b_raw * e_neg).astype(BF16)
    k_t = (k2 * e_neg).astype(BF16)
    b_p = (b_raw * e_tail).astype(BF16)
    k_p = (k2 * e_tail).astype(BF16)
    v_b = v.astype(BF16)

    def units_of(x):
        return [x[b * C:(b + 1) * C, i * G:(i + 1) * G] for b in range(NB) for i in range(n_groups)]

    rg, ag, bg, kg, vg, bpg, kpg = map(units_of, (r_t, a_t, b_t, k_t, v_b, b_p, k_p))
    units = range(NB * n_groups)
    stack = lambda p, q: jnp.concatenate([p, q], axis=0)
    lhs = [stack(ag[i], rg[i]) for i in units]
    sc = [_mm(lhs[i], stack(bd(bg[i]), bd(kg[i])), "nt") for i in units]
    a_ab = [jnp.where(strict_p, sc[i][:C, :G], 0.0) for i in units]
    a_rb = [jnp.where(incl_p, sc[i][C:, :G], 0.0) for i in units]
    a_ak = [jnp.where(strict_p, sc[i][:C, G:], 0.0) for i in units]
    a_rk = [jnp.where(incl_p, sc[i][C:, G:], 0.0) for i in units]

    inv = [eye_p + jnp.where(half_pair[0], a_ab[i], 0.0) for i in units]
    inv_b = [x.astype(BF16) for x in inv]
    for lvl in range(1, len(half_pair)):
        n_inv = [_mm(jnp.where(half_pair[lvl], a_ab[i], 0.0), bd(inv_b[i])) for i in units]
        inv = [inv[i] + _mm(inv_b[i], bd(n_inv[i].astype(BF16))) for i in units]
        inv_b = [x.astype(BF16) for x in inv]

    av_yv = [_mm(stack(a_ak[i], a_rk[i]), bd(vg[i])) for i in units]
    hat = [_mm(inv_b[i], jnp.concatenate([bd(ag[i]), bd(av_yv[i][:C].astype(BF16))], axis=1))
           for i in units]

    s0 = [state_ref[i] for i in units]
    res = [_mm(stack(hat[i][:, :G].astype(BF16), rg[i]), s0[i], "nt") for i in units]
    u = [(res[i][:C] + hat[i][:, G:]).astype(BF16) for i in units]
    upd = [_mm(stack(u[i], vg[i]), stack(bpg[i], kpg[i]), "tn") for i in units]
    for i in units:
        b, gidx = divmod(i, n_groups)
        state_ref[i] = (s0[i] * p_last[b][:, gidx * G:(gidx + 1) * G]
                        + jnp.where(same_head, upd[i], 0.0))
    y_units = [res[i][C:] + _mm(a_rb[i], bd(u[i])) + av_yv[i][C:] for i in units]

    y = jnp.concatenate([jnp.concatenate(y_units[b * n_groups:(b + 1) * n_groups], axis=1)
                         for b in range(NB)], axis=0)
    yc = y - head_sum(y, True) * (1.0 / HEAD_DIM)
    var = head_sum(yc * yc, False) * (1.0 / HEAD_DIM)
    yn = yc * lax.rsqrt(var + LNX_EPS) * pv(PV_LNG) + pv(PV_LNB)
    y_ref[...] = ((yn + bonus) * g).astype(BF16).reshape(NB, C, W)


def _rwkv_group(proj, proj_lo, v_first, pvec, wa2, g2, v2, layer, *, batch, nb):
    Lp = proj.shape[1]
    W = pvec.shape[2]
    C = WKV_CHUNK
    has_vres = v_first is not None
    tok = lambda col: pl.BlockSpec((nb, C, W), lambda bp, c: (bp, c, col))
    par = lambda arr: pl.BlockSpec((None,) + arr.shape[1:], lambda bp, c: (layer, 0, 0))
    in_specs = [tok(3), tok(4), tok(5), pl.BlockSpec((nb, C, LO_W), lambda bp, c: (bp, c, 0))]
    args = [proj, proj, proj, proj_lo]
    if has_vres:
        in_specs.append(tok(0))
        args.append(v_first)
    in_specs += [par(pvec), par(wa2), par(g2)]
    args += [pvec, wa2, g2]
    if has_vres:
        in_specs.append(par(v2))
        args.append(v2)
        out_shape = jax.ShapeDtypeStruct((batch, Lp, W), BF16)
        out_specs = tok(0)
    else:
        out_shape = (jax.ShapeDtypeStruct((batch, Lp, W), BF16),
                     jax.ShapeDtypeStruct((batch, Lp, W), F32))
        out_specs = (tok(0), tok(0))
    return pl.pallas_call(
        functools.partial(_wkv_kernel, has_vres),
        out_shape=out_shape,
        grid=(batch // nb, Lp // C),
        in_specs=in_specs,
        out_specs=out_specs,
        scratch_shapes=[
            pltpu.VMEM((nb * (W // LANES), LANES, LANES), F32),
            pltpu.VMEM((nb * SUBLANES, W), F32),
        ],
        compiler_params=_cparams(("parallel", "arbitrary")),
        name="rwkv_group",
    )(*args)


def _proj_out_kernel(yc_ref, yr_ref, w_ref, g_ref, h_ref, o_ref):
    half = yc_ref.shape[1]
    y = (jnp.dot(yc_ref[...], w_ref[0:half, :], preferred_element_type=F32)
         + jnp.dot(yr_ref[...], w_ref[half:2 * half, :], preferred_element_type=F32))
    o_ref[...] = h_ref[...] + _rms(y, g_ref[...])


def _proj_out(y_conv, y_rwkv, w_out, g, h, layer, *, tm):
    T, D = h.shape
    half = y_conv.shape[1]
    return pl.pallas_call(
        _proj_out_kernel,
        out_shape=jax.ShapeDtypeStruct((T, D), F32),
        grid=(T // tm,),
        in_specs=[
            pl.BlockSpec((tm, half), lambda i: (i, 0)),
            pl.BlockSpec((tm, half), lambda i: (i, 0)),
            pl.BlockSpec((None,) + w_out.shape[1:], lambda i: (layer, 0, 0)),
            _layer_vec(layer, D),
            pl.BlockSpec((tm, D), lambda i: (i, 0)),
        ],
        out_specs=pl.BlockSpec((tm, D), lambda i: (i, 0)),
        compiler_params=_cparams(("parallel",)),
        name="proj_out",
    )(y_conv, y_rwkv, w_out, g, h)


def _pad_axis(w, axis, size):
    pad = [(0, 0)] * w.ndim
    pad[axis] = (0, size - w.shape[axis])
    return jnp.pad(w, pad)


def _lora_cols(wd, ad, gd, vd):
    assert LO_AD == wd.shape[-1] and LO_GD == LO_AD + ad.shape[-1]
    return jnp.concatenate([wd, ad, _pad_axis(gd, -1, LO_GD_W), _pad_axis(vd, -1, LO_VD_W)], axis=-1)


def _first_layer_zero(w):
    return jnp.concatenate([jnp.zeros_like(w[:1]), w], axis=0)


def kernel(x, meta_tokens, ffn1_pre_g, ffn1_w_gu, ffn1_w_down, ffn1_post_g, mix_pre_g, w_in, w_in_vres, mu_rwkv, mu_vres, conv_w, conv_norm_g, decay_w0, decay_w2, iclr_a0, iclr_a2, vres_v0, vres_v2, gate_g2, k_k, k_a, r_k, lnx_g, lnx_b, w_out, mix_post_g, ffn2_pre_g, ffn2_w_gu, ffn2_w_down, ffn2_post_g):
    B, S, D = x.shape
    depth = w_in.shape[0]
    conv_width = conv_w.shape[2]
    rw = decay_w0.shape[1]
    L = N_META + S
    Lp = -(-L // WKV_CHUNK) * WKV_CHUNK
    T = B * Lp
    pick = lambda n, cands: next(c for c in cands if n % c == 0)
    tm = pick(T, (640, 512, 256, WKV_CHUNK))
    tm_proj = pick(T, (1040, 832, 640, 512, 256, WKV_CHUNK))
    tf = pick(ffn1_w_down.shape[1], (512, 256, 128))
    tt_conv = pick(Lp, (832, 512, 256, WKV_CHUNK))
    nb = pick(B, (4, 2, 1))
    o = 3 * conv_width + 3 * rw

    h = jnp.pad(x, ((0, 0), (N_META, Lp - L), (0, 0)))
    meta = jnp.broadcast_to(meta_tokens.astype(x.dtype)[None], (B, N_META, D))
    h = lax.dynamic_update_slice(h, meta, (0, 0, 0)).reshape(T, D)

    vec = lambda a: a.reshape(depth, 1, -1)
    ffn1 = (vec(ffn1_pre_g), ffn1_w_gu.astype(BF16), ffn1_w_down.astype(BF16), vec(0.5 * ffn1_post_g))
    ffn2 = (vec(ffn2_pre_g), ffn2_w_gu.astype(BF16), ffn2_w_down.astype(BF16), vec(0.5 * ffn2_post_g))
    w_main = w_in.astype(BF16)
    w_lo = _lora_cols(w_in[..., o:o + R_DECAY], w_in[..., o + R_DECAY:o + R_DECAY + R_ICLR],
                      w_in[..., o + R_DECAY + R_ICLR:], _first_layer_zero(w_in_vres)).astype(BF16)
    m = 3 * rw
    mu_lo = _lora_cols(mu_rwkv[:, m:m + R_DECAY], mu_rwkv[:, m + R_DECAY:m + R_DECAY + R_ICLR],
                       mu_rwkv[:, m + R_DECAY + R_ICLR:], _first_layer_zero(mu_vres))
    pvec = jnp.stack([decay_w0, iclr_a0, _first_layer_zero(vres_v0), k_k, k_a, r_k.reshape(depth, rw),
                      lnx_g, lnx_b, mu_rwkv[:, :rw], mu_rwkv[:, rw:2 * rw], mu_rwkv[:, 2 * rw:m],
                      _pad_axis(mu_lo, 1, rw)]
                     + [jnp.zeros((depth, rw), F32)] * (PV_ROWS - 12), axis=1)
    wa2 = jnp.concatenate([_pad_axis(decay_w2, 2, 2 * rw),
                           jnp.pad(iclr_a2, ((0, 0), (0, 0), (rw, 0)))], axis=1).astype(BF16)
    g2 = _pad_axis(gate_g2, 1, LO_GD_W).astype(BF16)
    v2 = _pad_axis(_first_layer_zero(vres_v2), 1, LO_VD_W).astype(BF16)
    w_out_b = w_out.astype(BF16)
    mix_pre, mix_post, conv_g = vec(mix_pre_g), vec(mix_post_g), vec(conv_norm_g)

    v_first = None
    for i in range(depth):
        h = _ffn(h, *ffn1, i, tm=tm, tf=tf)
        proj, proj_lo = _proj_in(h, mix_pre, w_main, o, w_lo, i, tm=tm_proj, tn=rw)
        y_conv = _conv_group(proj, conv_w, conv_g, i, batch=B, tt=tt_conv)
        proj3, proj_lo3 = proj.reshape(B, Lp, o), proj_lo.reshape(B, Lp, LO_W)
        if i == 0:
            y_rwkv, v_first = _rwkv_group(proj3, proj_lo3, None, pvec, wa2, g2, None, i, batch=B, nb=nb)
        else:
            y_rwkv = _rwkv_group(proj3, proj_lo3, v_first, pvec, wa2, g2, v2, i, batch=B, nb=nb)
        h = _proj_out(y_conv, y_rwkv.reshape(T, rw), w_out_b, mix_post, h, i, tm=tm)
        if i < depth - 1:
            h = _ffn(h, *ffn2, i, tm=tm, tf=tf)
    out = _ffn(h, *ffn2, depth - 1, tm=pick(S, (512, 256, WKV_CHUNK)), tf=tf, rows=(Lp, N_META, S))
    return out.reshape(B, S, D)
```

```python
import functools
import math

import jax
import jax.numpy as jnp
from jax import lax
from jax.experimental import pallas as pl
from jax.experimental.pallas import tpu as pltpu

F32 = jnp.float32
BF16 = jnp.bfloat16

NORM_EPS = 1e-6
LNX_EPS = 64e-5
N_META = 16
HEAD_DIM = 64
WKV_CHUNK = 64
R_DECAY, R_ICLR, R_GATE, R_VRES = 64, 64, 160, 32
LO_WD, LO_AD, LO_GD, LO_VD = 0, 64, 128, 384
LO_GD_W = 256
LO_VD_W = 128
LO_W = 512
LANES = 128
GROUP_HEADS = LANES // HEAD_DIM
SUBLANES = 8

VMEM_LIMIT_BYTES = 56 * 1024 * 1024


def _cparams(sem):
    return pltpu.CompilerParams(dimension_semantics=sem, vmem_limit_bytes=VMEM_LIMIT_BYTES)


def _rms(x, g):
    ms = jnp.mean(x * x, axis=-1, keepdims=True)
    return x * lax.rsqrt(ms + NORM_EPS) * g


def _layer_vec(layer, width):
    return pl.BlockSpec((None, 1, width), lambda *_: (layer, 0, 0))


def _ffn_kernel(h_ref, pre_g_ref, wg_ref, wu_ref, wd_ref, half_post_g_ref, o_ref, xn_ref):
    j = pl.program_id(2)

    @pl.when(j == 0)
    def _():
        xn_ref[...] = _rms(h_ref[...], pre_g_ref[...]).astype(BF16)
        o_ref[...] = jnp.zeros_like(o_ref)

    xn = xn_ref[...]
    gate = jnp.dot(xn, wg_ref[...], preferred_element_type=F32)
    up = jnp.dot(xn, wu_ref[...], preferred_element_type=F32)
    act = (gate * jax.nn.sigmoid(gate) * up).astype(BF16)
    o_ref[...] += jnp.dot(act, wd_ref[...], preferred_element_type=F32)

    @pl.when(j == pl.num_programs(2) - 1)
    def _():
        o_ref[...] = h_ref[...] + _rms(o_ref[...], half_post_g_ref[...])


def _ffn(h, pre_g, w_gu, w_down, half_post_g, layer, *, tm, tf, rows=None):
    T, D = h.shape
    FF = w_down.shape[1]
    nj = FF // tf
    period, start, count = rows or (T, 0, T)
    n_seq, per = T // period, count // tm
    if rows is None:
        h_spec = pl.BlockSpec((tm, D), lambda s, i, j: (i, 0))
    else:
        assert period % SUBLANES == 0 and start % SUBLANES == 0 and tm % SUBLANES == 0
        h_spec = pl.BlockSpec((pl.Element(tm), pl.Element(D)),
                              lambda s, i, j: (pl.multiple_of(s * period + start + i * tm, SUBLANES), 0))
    return pl.pallas_call(
        _ffn_kernel,
        out_shape=jax.ShapeDtypeStruct((n_seq * count, D), F32),
        grid=(n_seq, per, nj),
        in_specs=[
            h_spec,
            _layer_vec(layer, D),
            pl.BlockSpec((None, D, tf), lambda s, i, j: (layer, 0, j)),
            pl.BlockSpec((None, D, tf), lambda s, i, j: (layer, 0, j + nj)),
            pl.BlockSpec((None, tf, D), lambda s, i, j: (layer, j, 0)),
            _layer_vec(layer, D),
        ],
        out_specs=pl.BlockSpec((tm, D), lambda s, i, j: (s * per + i, 0)),
        scratch_shapes=[pltpu.VMEM((tm, D), BF16)],
        compiler_params=_cparams(("parallel", "parallel", "arbitrary")),
        name="ffn",
    )(h, pre_g, w_gu, w_gu, w_down, half_post_g)


def _proj_in_kernel(h_ref, g_ref, w_ref, w_lo_ref, o_ref, o_lo_ref, xn_ref):
    j = pl.program_id(1)
    last = pl.num_programs(1) - 1

    @pl.when(j == 0)
    def _():
        xn_ref[...] = _rms(h_ref[...], g_ref[...]).astype(BF16)

    @pl.when(j < last)
    def _():
        o_ref[...] = jnp.dot(xn_ref[...], w_ref[...], preferred_element_type=F32)

    @pl.when(j == last)
    def _():
        o_lo_ref[...] = jnp.dot(xn_ref[...], w_lo_ref[...], preferred_element_type=F32)


def _proj_in(h, g, w_main, n_main, w_lo, layer, *, tm, tn):
    T, D = h.shape
    n_lo = w_lo.shape[2]
    n_tiles = n_main // tn
    main_col = lambda j: jnp.minimum(j, n_tiles - 1)
    return pl.pallas_call(
        _proj_in_kernel,
        out_shape=(jax.ShapeDtypeStruct((T, n_main), F32), jax.ShapeDtypeStruct((T, n_lo), F32)),
        grid=(T // tm, n_tiles + 1),
        in_specs=[
            pl.BlockSpec((tm, D), lambda i, j: (i, 0)),
            _layer_vec(layer, D),
            pl.BlockSpec((None, D, tn), lambda i, j: (layer, 0, main_col(j))),
            pl.BlockSpec((None, D, n_lo), lambda i, j: (layer, 0, 0)),
        ],
        out_specs=(pl.BlockSpec((tm, tn), lambda i, j: (i, main_col(j))),
                   pl.BlockSpec((tm, n_lo), lambda i, j: (i, 0))),
        scratch_shapes=[pltpu.VMEM((tm, D), BF16)],
        compiler_params=_cparams(("parallel", "arbitrary")),
        name="proj_in",
    )(h, g, w_main, w_lo)


(PV_W0, PV_A0, PV_V0, PV_KK, PV_KA, PV_RK, PV_LNG, PV_LNB, PV_MUR, PV_MUK, PV_MUV, PV_MULO,
 PV_CONVG, PV_CONVW) = range(14)
CONV_K = 3
PV_ROWS = 16

_DIMS = {"nn": ((1,), (0,)), "nt": ((1,), (1,)), "tn": ((0,), (0,))}


def _split(x):
    hi = x.astype(BF16)
    return hi, (x - hi.astype(F32)).astype(BF16)


def _mm(x, w, form="nn", x_hp=False):
    dot = lambda p, q: lax.dot_general(p, q, (_DIMS[form], ((), ())), preferred_element_type=F32)
    w = w.astype(BF16)
    if not x_hp:
        return dot(x.astype(BF16), w)
    hi, lo = _split(x)
    return dot(hi, w) + dot(lo, w)


def _mixer_kernel(has_vres, *refs):
    if has_vres:
        (gb_ref, gc_ref, u_ref, r_ref, k_ref, v_ref, lo_ref, vf_ref, pv_ref, wa2_ref, g2_ref, v2_ref,
         yc_ref, y_ref, state_ref, carry_ref, conv_carry_ref) = refs
    else:
        (gb_ref, gc_ref, u_ref, r_ref, k_ref, v_ref, lo_ref, pv_ref, wa2_ref, g2_ref,
         yc_ref, y_ref, vf_out_ref, state_ref, carry_ref, conv_carry_ref) = refs
    NB, C, W = r_ref.shape
    R = NB * C
    G = LANES
    n_groups = W // G

    @pl.when(pl.program_id(1) == 0)
    def _():
        state_ref[...] = jnp.zeros_like(state_ref)
        carry_ref[...] = jnp.zeros_like(carry_ref)
        conv_carry_ref[...] = jnp.zeros_like(conv_carry_ref)

    def pv(i):
        return pv_ref[i:i + 1, :]

    trow = lax.broadcasted_iota(jnp.int32, (C, W), 0)
    for b in range(NB):
        uc = gc_ref[b] * u_ref[b]
        base = b * SUBLANES
        last2 = conv_carry_ref[base + 1:base + 2, :]
        last1 = conv_carry_ref[base + 2:base + 3, :]
        back1 = jnp.where(trow == 0, last1, pltpu.roll(uc, 1, 0))
        back2 = jnp.where(trow == 0, last2, jnp.where(trow == 1, last1, pltpu.roll(uc, 2, 0)))
        conv_carry_ref[base + 1:base + CONV_K, :] = uc[C - CONV_K + 1:C, :]
        conv = pv(PV_CONVW) * back2 + pv(PV_CONVW + 1) * back1 + pv(PV_CONVW + 2) * uc
        yc_ref[b] = _rms(gb_ref[b] * conv, pv(PV_CONVG)).astype(BF16)

    def token_shift(x_ref, slot, mu):
        width = x_ref.shape[-1]
        first = lax.broadcasted_iota(jnp.int32, (C, width), 0) == 0
        parts = []
        for b in range(NB):
            x = x_ref[b]
            crow = b * SUBLANES + slot
            prev = jnp.where(first, carry_ref[crow:crow + 1, :width], pltpu.roll(x, 1, 0))
            carry_ref[crow:crow + 1, :width] = x[C - 1:C, :]
            parts.append(x + mu[:, :width] * (prev - x))
        return jnp.concatenate(parts, axis=0)

    r = token_shift(r_ref, 0, pv(PV_MUR))
    k = token_shift(k_ref, 1, pv(PV_MUK))
    v = token_shift(v_ref, 2, pv(PV_MUV))
    lo = token_shift(lo_ref, 3, pv(PV_MULO))

    wa = lo[:, LO_WD:LO_WD + LANES]
    wa = jnp.where(lax.broadcasted_iota(jnp.int32, wa.shape, 1) < LO_AD, jnp.tanh(wa), wa)
    gd = lo[:, LO_GD:LO_GD + LO_GD_W]
    if has_vres:
        vd = lo[:, LO_VD:LO_VD + LO_VD_W]
        mix = jax.nn.sigmoid(pv(PV_V0) + _mm(vd, v2_ref[...]))
        v = v + (vf_ref[...].reshape(R, W) - v) * mix
    else:
        vf_out_ref[...] = v.reshape(NB, C, W)

    za = _mm(wa, wa2_ref[...])
    logw = -math.exp(-0.5) * jax.nn.sigmoid(pv(PV_W0) + za[:, :W])
    a = jax.nn.sigmoid(pv(PV_A0) + za[:, W:])
    g = _mm(jax.nn.sigmoid(gd), g2_ref[...])
    kk_raw = k * pv(PV_KK)
    k2 = k * (1.0 + (a - 1.0) * pv(PV_KA))

    gi = lax.broadcasted_iota(jnp.int32, (G, G), 0)
    gj = lax.broadcasted_iota(jnp.int32, (G, G), 1)
    same_head = (gi // HEAD_DIM) == (gj // HEAD_DIM)
    bd_mask = same_head.astype(F32).astype(BF16)
    ti = lax.broadcasted_iota(jnp.int32, (C, G), 0)
    si = lax.broadcasted_iota(jnp.int32, (C, G), 1) % HEAD_DIM
    incl_p, strict_p = ti >= si, ti > si
    eye_p = (ti == si).astype(F32)
    half_pair = []
    blk = 2
    while blk <= C:
        half_pair.append(((ti // blk) == (si // blk)) & ((ti // (blk // 2)) != (si // (blk // 2))))
        blk *= 2

    def bd(y):
        return jnp.concatenate([y] * GROUP_HEADS, axis=0) * bd_mask

    def group_cols(x):
        return [x[:, i * G:(i + 1) * G] for i in range(n_groups)]

    def head_sum(x, x_hp):
        s = _mm(jnp.concatenate(group_cols(x), axis=0), bd_mask, x_hp=x_hp)
        return jnp.concatenate([s[i * R:(i + 1) * R] for i in range(n_groups)], axis=1)

    kk = kk_raw * jnp.minimum(lax.rsqrt(head_sum(kk_raw * kk_raw, False)), 1e12)
    bonus = head_sum(r * k2 * pv(PV_RK), True) * v

    tc = lax.broadcasted_iota(jnp.int32, (C, C), 0)
    sc = lax.broadcasted_iota(jnp.int32, (C, C), 1)
    tri = (tc >= sc).astype(F32).astype(BF16)
    tri3 = jnp.concatenate([tri, tri, tri], axis=1)
    l_hi = logw.astype(BF16)
    l_rest = logw - l_hi.astype(F32)
    l_mid = l_rest.astype(BF16)
    l_lo = (l_rest - l_mid.astype(F32)).astype(BF16)
    seq_rows = lambda x, b: x[b * C:(b + 1) * C]
    cs_b = [jnp.dot(tri3, jnp.concatenate([seq_rows(l_hi, b), seq_rows(l_mid, b), seq_rows(l_lo, b)],
                                          axis=0), preferred_element_type=F32) for b in range(NB)]
    cs = jnp.concatenate(cs_b, axis=0)
    cs_last = [c[C - 1:C, :] for c in cs_b]
    cs_end = jnp.concatenate([jnp.broadcast_to(c, (C, W)) for c in cs_last], axis=0)
    e_neg = jnp.exp(-cs)
    e_tail = jnp.exp(cs_end - cs)
    p_last = [jnp.exp(c) for c in cs_last]

    b_raw = kk * a
    r_t = (r * jnp.exp(cs)).astype(BF16)
    a_t = (-kk * jnp.exp(cs - logw)).astype(BF16)
    b_t = (b_raw * e_neg).astype(BF16)
    k_t = (k2 * e_neg).astype(BF16)
    b_p = (b_raw * e_tail).astype(BF16)
    k_p = (k2 * e_tail).astype(BF16)
    v_b = v.astype(BF16)

    def units_of(x):
        return [x[b * C:(b + 1) * C, i * G:(i + 1) * G] for b in range(NB) for i in range(n_groups)]

    rg, ag, bg, kg, vg, bpg, kpg = map(units_of, (r_t, a_t, b_t, k_t, v_b, b_p, k_p))
    units = range(NB * n_groups)
    stack = lambda p, q: jnp.concatenate([p, q], axis=0)
    lhs = [stack(ag[i], rg[i]) for i in units]
    sc = [_mm(lhs[i], stack(bd(bg[i]), bd(kg[i])), "nt") for i in units]
    a_ab = [jnp.where(strict_p, sc[i][:C, :G], 0.0) for i in units]
    a_rb = [jnp.where(incl_p, sc[i][C:, :G], 0.0) for i in units]
    a_ak = [jnp.where(strict_p, sc[i][:C, G:], 0.0) for i in units]
    a_rk = [jnp.where(incl_p, sc[i][C:, G:], 0.0) for i in units]

    inv = [eye_p + jnp.where(half_pair[0], a_ab[i], 0.0) for i in units]
    inv_b = [x.astype(BF16) for x in inv]
    for lvl in range(1, len(half_pair)):
        n_inv = [_mm(jnp.where(half_pair[lvl], a_ab[i], 0.0), bd(inv_b[i])) for i in units]
        inv = [inv[i] + _mm(inv_b[i], bd(n_inv[i].astype(BF16))) for i in units]
        inv_b = [x.astype(BF16) for x in inv]

    av_yv = [_mm(stack(a_ak[i], a_rk[i]), bd(vg[i])) for i in units]
    hat = [_mm(inv_b[i], jnp.concatenate([bd(ag[i]), bd(av_yv[i][:C].astype(BF16))], axis=1))
           for i in units]

    s0 = [state_ref[i] for i in units]
    res = [_mm(stack(hat[i][:, :G].astype(BF16), rg[i]), s0[i], "nt") for i in units]
    u = [(res[i][:C] + hat[i][:, G:]).astype(BF16) for i in units]
    upd = [_mm(stack(u[i], vg[i]), stack(bpg[i], kpg[i]), "tn") for i in units]
    for i in units:
        b, gidx = divmod(i, n_groups)
        state_ref[i] = (s0[i] * p_last[b][:, gidx * G:(gidx + 1) * G]
                        + jnp.where(same_head, upd[i], 0.0))
    y_units = [res[i][C:] + _mm(a_rb[i], bd(u[i])) + av_yv[i][C:] for i in units]

    y = jnp.concatenate([jnp.concatenate(y_units[b * n_groups:(b + 1) * n_groups], axis=1)
                         for b in range(NB)], axis=0)
    yc = y - head_sum(y, True) * (1.0 / HEAD_DIM)
    var = head_sum(yc * yc, False) * (1.0 / HEAD_DIM)
    yn = yc * lax.rsqrt(var + LNX_EPS) * pv(PV_LNG) + pv(PV_LNB)
    y_ref[...] = ((yn + bonus) * g).astype(BF16).reshape(NB, C, W)


def _mixer(proj, proj_lo, v_first, pvec, wa2, g2, v2, layer, *, batch, nb):
    Lp = proj.shape[1]
    W = pvec.shape[2]
    C = WKV_CHUNK
    has_vres = v_first is not None
    tok = lambda col: pl.BlockSpec((nb, C, W), lambda bp, c: (bp, c, col))
    par = lambda arr: pl.BlockSpec((None,) + arr.shape[1:], lambda bp, c: (layer, 0, 0))
    in_specs = [tok(col) for col in range(6)] + [pl.BlockSpec((nb, C, LO_W), lambda bp, c: (bp, c, 0))]
    args = [proj] * 6 + [proj_lo]
    if has_vres:
        in_specs.append(tok(0))
        args.append(v_first)
    in_specs += [par(pvec), par(wa2), par(g2)]
    args += [pvec, wa2, g2]
    token_bf16 = jax.ShapeDtypeStruct((batch, Lp, W), BF16)
    out_shape, out_specs = [token_bf16, token_bf16], [tok(0), tok(0)]
    if has_vres:
        in_specs.append(par(v2))
        args.append(v2)
    else:
        out_shape.append(jax.ShapeDtypeStruct((batch, Lp, W), F32))
        out_specs.append(tok(0))
    return pl.pallas_call(
        functools.partial(_mixer_kernel, has_vres),
        out_shape=tuple(out_shape),
        grid=(batch // nb, Lp // C),
        in_specs=in_specs,
        out_specs=tuple(out_specs),
        scratch_shapes=[
            pltpu.VMEM((nb * (W // LANES), LANES, LANES), F32),
            pltpu.VMEM((nb * SUBLANES, W), F32),
            pltpu.VMEM((nb * SUBLANES, W), F32),
        ],
        compiler_params=_cparams(("parallel", "arbitrary")),
        name="mixer",
    )(*args)


def _proj_out_kernel(yc_ref, yr_ref, w_ref, g_ref, h_ref, o_ref):
    half = yc_ref.shape[1]
    y = (jnp.dot(yc_ref[...], w_ref[0:half, :], preferred_element_type=F32)
         + jnp.dot(yr_ref[...], w_ref[half:2 * half, :], preferred_element_type=F32))
    o_ref[...] = h_ref[...] + _rms(y, g_ref[...])


def _proj_out(y_conv, y_rwkv, w_out, g, h, layer, *, tm):
    T, D = h.shape
    half = y_conv.shape[1]
    return pl.pallas_call(
        _proj_out_kernel,
        out_shape=jax.ShapeDtypeStruct((T, D), F32),
        grid=(T // tm,),
        in_specs=[
            pl.BlockSpec((tm, half), lambda i: (i, 0)),
            pl.BlockSpec((tm, half), lambda i: (i, 0)),
            pl.BlockSpec((None,) + w_out.shape[1:], lambda i: (layer, 0, 0)),
            _layer_vec(layer, D),
            pl.BlockSpec((tm, D), lambda i: (i, 0)),
        ],
        out_specs=pl.BlockSpec((tm, D), lambda i: (i, 0)),
        compiler_params=_cparams(("parallel",)),
        name="proj_out",
    )(y_conv, y_rwkv, w_out, g, h)


def _pad_axis(w, axis, size):
    pad = [(0, 0)] * w.ndim
    pad[axis] = (0, size - w.shape[axis])
    return jnp.pad(w, pad)


def _lora_cols(wd, ad, gd, vd):
    assert LO_AD == wd.shape[-1] and LO_GD == LO_AD + ad.shape[-1]
    return jnp.concatenate([wd, ad, _pad_axis(gd, -1, LO_GD_W), _pad_axis(vd, -1, LO_VD_W)], axis=-1)


def _first_layer_zero(w):
    return jnp.concatenate([jnp.zeros_like(w[:1]), w], axis=0)


def kernel(x, meta_tokens, ffn1_pre_g, ffn1_w_gu, ffn1_w_down, ffn1_post_g, mix_pre_g, w_in, w_in_vres, mu_rwkv, mu_vres, conv_w, conv_norm_g, decay_w0, decay_w2, iclr_a0, iclr_a2, vres_v0, vres_v2, gate_g2, k_k, k_a, r_k, lnx_g, lnx_b, w_out, mix_post_g, ffn2_pre_g, ffn2_w_gu, ffn2_w_down, ffn2_post_g):
    B, S, D = x.shape
    depth = w_in.shape[0]
    conv_width = conv_w.shape[2]
    rw = decay_w0.shape[1]
    L = N_META + S
    Lp = -(-L // WKV_CHUNK) * WKV_CHUNK
    T = B * Lp
    pick = lambda n, cands: next(c for c in cands if n % c == 0)
    tm = pick(T, (640, 512, 256, WKV_CHUNK))
    tm_proj = pick(T, (1040, 832, 640, 512, 256, WKV_CHUNK))
    tf = pick(ffn1_w_down.shape[1], (512, 256, 128))
    nb = pick(B, (4, 2, 1))
    o = 3 * conv_width + 3 * rw

    h = jnp.pad(x, ((0, 0), (N_META, Lp - L), (0, 0)))
    meta = jnp.broadcast_to(meta_tokens.astype(x.dtype)[None], (B, N_META, D))
    h = lax.dynamic_update_slice(h, meta, (0, 0, 0)).reshape(T, D)

    vec = lambda a: a.reshape(depth, 1, -1)
    ffn1 = (vec(ffn1_pre_g), ffn1_w_gu.astype(BF16), ffn1_w_down.astype(BF16), vec(0.5 * ffn1_post_g))
    ffn2 = (vec(ffn2_pre_g), ffn2_w_gu.astype(BF16), ffn2_w_down.astype(BF16), vec(0.5 * ffn2_post_g))
    w_main = w_in.astype(BF16)
    w_lo = _lora_cols(w_in[..., o:o + R_DECAY], w_in[..., o + R_DECAY:o + R_DECAY + R_ICLR],
                      w_in[..., o + R_DECAY + R_ICLR:], _first_layer_zero(w_in_vres)).astype(BF16)
    m = 3 * rw
    mu_lo = _lora_cols(mu_rwkv[:, m:m + R_DECAY], mu_rwkv[:, m + R_DECAY:m + R_DECAY + R_ICLR],
                       mu_rwkv[:, m + R_DECAY + R_ICLR:], _first_layer_zero(mu_vres))
    pvec = jnp.stack([decay_w0, iclr_a0, _first_layer_zero(vres_v0), k_k, k_a, r_k.reshape(depth, rw),
                      lnx_g, lnx_b, mu_rwkv[:, :rw], mu_rwkv[:, rw:2 * rw], mu_rwkv[:, 2 * rw:m],
                      _pad_axis(mu_lo, 1, rw), conv_norm_g]
                     + [conv_w[:, tap] for tap in range(CONV_K)], axis=1)
    assert pvec.shape[1:] == (PV_ROWS, rw) and conv_w.shape[1] == CONV_K and conv_width == rw
    wa2 = jnp.concatenate([_pad_axis(decay_w2, 2, 2 * rw),
                           jnp.pad(iclr_a2, ((0, 0), (0, 0), (rw, 0)))], axis=1).astype(BF16)
    g2 = _pad_axis(gate_g2, 1, LO_GD_W).astype(BF16)
    v2 = _pad_axis(_first_layer_zero(vres_v2), 1, LO_VD_W).astype(BF16)
    w_out_b = w_out.astype(BF16)
    mix_pre, mix_post = vec(mix_pre_g), vec(mix_post_g)

    v_first = None
    for i in range(depth):
        h = _ffn(h, *ffn1, i, tm=tm, tf=tf)
        proj, proj_lo = _proj_in(h, mix_pre, w_main, o, w_lo, i, tm=tm_proj, tn=rw)
        proj3, proj_lo3 = proj.reshape(B, Lp, o), proj_lo.reshape(B, Lp, LO_W)
        if i == 0:
            y_conv, y_rwkv, v_first = _mixer(proj3, proj_lo3, None, pvec, wa2, g2, None, i, batch=B, nb=nb)
        else:
            y_conv, y_rwkv = _mixer(proj3, proj_lo3, v_first, pvec, wa2, g2, v2, i, batch=B, nb=nb)
        h = _proj_out(y_conv.reshape(T, rw), y_rwkv.reshape(T, rw), w_out_b, mix_post, h, i, tm=tm)
        if i < depth - 1:
            h = _ffn(h, *ffn2, i, tm=tm, tf=tf)
    out = _ffn(h, *ffn2, depth - 1, tm=pick(S, (512, 256, WKV_CHUNK)), tf=tf, rows=(Lp, N_META, S))
    return out.reshape(B, S, D)
```

```python
import functools
import math

import jax
import jax.numpy as jnp
from jax import lax
from jax.experimental import pallas as pl
from jax.experimental.pallas import tpu as pltpu

F32 = jnp.float32
BF16 = jnp.bfloat16

NORM_EPS = 1e-6
LNX_EPS = 64e-5
N_META = 16
HEAD_DIM = 64
WKV_CHUNK = 64
R_DECAY, R_ICLR, R_GATE, R_VRES = 64, 64, 160, 32
LO_WD, LO_AD, LO_GD, LO_VD = 0, 64, 128, 384
LO_GD_W = 256
LO_VD_W = 128
LO_W = 512
LANES = 128
GROUP_HEADS = LANES // HEAD_DIM
SUBLANES = 8

VMEM_LIMIT_BYTES = 56 * 1024 * 1024


def _cparams(sem):
    return pltpu.CompilerParams(dimension_semantics=sem, vmem_limit_bytes=VMEM_LIMIT_BYTES)


def _rms(x, g):
    ms = jnp.mean(x * x, axis=-1, keepdims=True)
    return x * lax.rsqrt(ms + NORM_EPS) * g


def _layer_vec(layer, width):
    return pl.BlockSpec((None, 1, width), lambda *_: (layer, 0, 0))


def _ffn_kernel(h_ref, pre_g_ref, wg_ref, wu_ref, wd_ref, half_post_g_ref, o_ref, xn_ref):
    j = pl.program_id(2)

    @pl.when(j == 0)
    def _():
        xn_ref[...] = _rms(h_ref[...], pre_g_ref[...]).astype(BF16)
        o_ref[...] = jnp.zeros_like(o_ref)

    xn = xn_ref[...]
    gate = jnp.dot(xn, wg_ref[...], preferred_element_type=F32)
    up = jnp.dot(xn, wu_ref[...], preferred_element_type=F32)
    act = (gate * jax.nn.sigmoid(gate) * up).astype(BF16)
    o_ref[...] += jnp.dot(act, wd_ref[...], preferred_element_type=F32)

    @pl.when(j == pl.num_programs(2) - 1)
    def _():
        o_ref[...] = h_ref[...] + _rms(o_ref[...], half_post_g_ref[...])


def _ffn(h, pre_g, w_gu, w_down, half_post_g, layer, *, tm, tf, rows=None):
    T, D = h.shape
    FF = w_down.shape[1]
    nj = FF // tf
    period, start, count = rows or (T, 0, T)
    n_seq, per = T // period, count // tm
    if rows is None:
        h_spec = pl.BlockSpec((tm, D), lambda s, i, j: (i, 0))
    else:
        assert period % SUBLANES == 0 and start % SUBLANES == 0 and tm % SUBLANES == 0
        h_spec = pl.BlockSpec((pl.Element(tm), pl.Element(D)),
                              lambda s, i, j: (pl.multiple_of(s * period + start + i * tm, SUBLANES), 0))
    return pl.pallas_call(
        _ffn_kernel,
        out_shape=jax.ShapeDtypeStruct((n_seq * count, D), F32),
        grid=(n_seq, per, nj),
        in_specs=[
            h_spec,
            _layer_vec(layer, D),
            pl.BlockSpec((None, D, tf), lambda s, i, j: (layer, 0, j)),
            pl.BlockSpec((None, D, tf), lambda s, i, j: (layer, 0, j + nj)),
            pl.BlockSpec((None, tf, D), lambda s, i, j: (layer, j, 0)),
            _layer_vec(layer, D),
        ],
        out_specs=pl.BlockSpec((tm, D), lambda s, i, j: (s * per + i, 0)),
        scratch_shapes=[pltpu.VMEM((tm, D), BF16)],
        compiler_params=_cparams(("parallel", "parallel", "arbitrary")),
        name="ffn",
    )(h, pre_g, w_gu, w_gu, w_down, half_post_g)


def _proj_in_kernel(h_ref, g_ref, w_ref, w_lo_ref, o_ref, o_lo_ref, xn_ref):
    j = pl.program_id(1)
    last = pl.num_programs(1) - 1

    @pl.when(j == 0)
    def _():
        xn_ref[...] = _rms(h_ref[...], g_ref[...]).astype(BF16)

    @pl.when(j < last)
    def _():
        o_ref[...] = jnp.dot(xn_ref[...], w_ref[...], preferred_element_type=F32)

    @pl.when(j == last)
    def _():
        o_lo_ref[...] = jnp.dot(xn_ref[...], w_lo_ref[...], preferred_element_type=F32)


def _proj_in(h, g, w_main, n_main, w_lo, layer, *, tm, tn):
    T, D = h.shape
    n_lo = w_lo.shape[2]
    n_tiles = n_main // tn
    main_col = lambda j: jnp.minimum(j, n_tiles - 1)
    return pl.pallas_call(
        _proj_in_kernel,
        out_shape=(jax.ShapeDtypeStruct((T, n_main), F32), jax.ShapeDtypeStruct((T, n_lo), F32)),
        grid=(T // tm, n_tiles + 1),
        in_specs=[
            pl.BlockSpec((tm, D), lambda i, j: (i, 0)),
            _layer_vec(layer, D),
            pl.BlockSpec((None, D, tn), lambda i, j: (layer, 0, main_col(j))),
            pl.BlockSpec((None, D, n_lo), lambda i, j: (layer, 0, 0)),
        ],
        out_specs=(pl.BlockSpec((tm, tn), lambda i, j: (i, main_col(j))),
                   pl.BlockSpec((tm, n_lo), lambda i, j: (i, 0))),
        scratch_shapes=[pltpu.VMEM((tm, D), BF16)],
        compiler_params=_cparams(("parallel", "arbitrary")),
        name="proj_in",
    )(h, g, w_main, w_lo)


(PV_W0, PV_A0, PV_V0, PV_KK, PV_KA, PV_RK, PV_LNG, PV_LNB, PV_MUR, PV_MUK, PV_MUV, PV_MULO,
 PV_CONVG, PV_CONVW) = range(14)
CONV_K = 3
PV_ROWS = 16

_DIMS = {"nn": ((1,), (0,)), "nt": ((1,), (1,)), "tn": ((0,), (0,))}


def _split(x):
    hi = x.astype(BF16)
    return hi, (x - hi.astype(F32)).astype(BF16)


def _mm(x, w, form="nn", x_hp=False):
    dot = lambda p, q: lax.dot_general(p, q, (_DIMS[form], ((), ())), preferred_element_type=F32)
    w = w.astype(BF16)
    if not x_hp:
        return dot(x.astype(BF16), w)
    hi, lo = _split(x)
    return dot(hi, w) + dot(lo, w)


def _mixer_kernel(has_vres, *refs):
    if has_vres:
        (gb_ref, gc_ref, u_ref, r_ref, k_ref, v_ref, lo_ref, vf_ref, pv_ref, wa2_ref, g2_ref, v2_ref,
         yc_ref, y_ref, state_ref, carry_ref, conv_carry_ref) = refs
    else:
        (gb_ref, gc_ref, u_ref, r_ref, k_ref, v_ref, lo_ref, pv_ref, wa2_ref, g2_ref,
         yc_ref, y_ref, vf_out_ref, state_ref, carry_ref, conv_carry_ref) = refs
    NB, C, W = r_ref.shape
    R = NB * C
    G = LANES
    n_groups = W // G

    @pl.when(pl.program_id(1) == 0)
    def _():
        state_ref[...] = jnp.zeros_like(state_ref)
        carry_ref[...] = jnp.zeros_like(carry_ref)
        conv_carry_ref[...] = jnp.zeros_like(conv_carry_ref)

    def pv(i):
        return pv_ref[i:i + 1, :]

    trow = lax.broadcasted_iota(jnp.int32, (C, W), 0)
    for b in range(NB):
        uc = gc_ref[b] * u_ref[b]
        base = b * SUBLANES
        last2 = conv_carry_ref[base + 1:base + 2, :]
        last1 = conv_carry_ref[base + 2:base + 3, :]
        back1 = jnp.where(trow == 0, last1, pltpu.roll(uc, 1, 0))
        back2 = jnp.where(trow == 0, last2, jnp.where(trow == 1, last1, pltpu.roll(uc, 2, 0)))
        conv_carry_ref[base + 1:base + CONV_K, :] = uc[C - CONV_K + 1:C, :]
        conv = pv(PV_CONVW) * back2 + pv(PV_CONVW + 1) * back1 + pv(PV_CONVW + 2) * uc
        yc_ref[b] = _rms(gb_ref[b] * conv, pv(PV_CONVG)).astype(BF16)

    def token_shift(x_ref, slot, mu):
        width = x_ref.shape[-1]
        first = lax.broadcasted_iota(jnp.int32, (C, width), 0) == 0
        parts = []
        for b in range(NB):
            x = x_ref[b]
            crow = b * SUBLANES + slot
            prev = jnp.where(first, carry_ref[crow:crow + 1, :width], pltpu.roll(x, 1, 0))
            carry_ref[crow:crow + 1, :width] = x[C - 1:C, :]
            parts.append(x + mu[:, :width] * (prev - x))
        return jnp.concatenate(parts, axis=0)

    r = token_shift(r_ref, 0, pv(PV_MUR))
    k = token_shift(k_ref, 1, pv(PV_MUK))
    v = token_shift(v_ref, 2, pv(PV_MUV))
    lo = token_shift(lo_ref, 3, pv(PV_MULO))

    wa = lo[:, LO_WD:LO_WD + LANES]
    wa = jnp.where(lax.broadcasted_iota(jnp.int32, wa.shape, 1) < LO_AD, jnp.tanh(wa), wa)
    gd = lo[:, LO_GD:LO_GD + LO_GD_W]
    if has_vres:
        vd = lo[:, LO_VD:LO_VD + LO_VD_W]
        mix = jax.nn.sigmoid(pv(PV_V0) + _mm(vd, v2_ref[...]))
        v = v + (vf_ref[...].reshape(R, W) - v) * mix
    else:
        vf_out_ref[...] = v.reshape(NB, C, W)

    za = _mm(wa, wa2_ref[...])
    logw = -math.exp(-0.5) * jax.nn.sigmoid(pv(PV_W0) + za[:, :W])
    a = jax.nn.sigmoid(pv(PV_A0) + za[:, W:])
    g = _mm(jax.nn.sigmoid(gd), g2_ref[...])
    kk_raw = k * pv(PV_KK)
    k2 = k * (1.0 + (a - 1.0) * pv(PV_KA))

    gi = lax.broadcasted_iota(jnp.int32, (G, G), 0)
    gj = lax.broadcasted_iota(jnp.int32, (G, G), 1)
    same_head = (gi // HEAD_DIM) == (gj // HEAD_DIM)
    bd_mask = same_head.astype(F32).astype(BF16)
    ti = lax.broadcasted_iota(jnp.int32, (C, G), 0)
    si = lax.broadcasted_iota(jnp.int32, (C, G), 1) % HEAD_DIM
    incl_p, strict_p = ti >= si, ti > si
    eye_p = (ti == si).astype(F32)
    half_pair = []
    blk = 2
    while blk <= C:
        half_pair.append(((ti // blk) == (si // blk)) & ((ti // (blk // 2)) != (si // (blk // 2))))
        blk *= 2

    def bd(y):
        return jnp.concatenate([y] * GROUP_HEADS, axis=0) * bd_mask

    def group_cols(x):
        return [x[:, i * G:(i + 1) * G] for i in range(n_groups)]

    def head_sum(x, x_hp):
        s = _mm(jnp.concatenate(group_cols(x), axis=0), bd_mask, x_hp=x_hp)
        return jnp.concatenate([s[i * R:(i + 1) * R] for i in range(n_groups)], axis=1)

    kk = kk_raw * jnp.minimum(lax.rsqrt(head_sum(kk_raw * kk_raw, False)), 1e12)
    bonus = head_sum(r * k2 * pv(PV_RK), True) * v

    tc = lax.broadcasted_iota(jnp.int32, (C, C), 0)
    sc = lax.broadcasted_iota(jnp.int32, (C, C), 1)
    tri = (tc >= sc).astype(F32).astype(BF16)
    tri3 = jnp.concatenate([tri, tri, tri], axis=1)
    l_hi = logw.astype(BF16)
    l_rest = logw - l_hi.astype(F32)
    l_mid = l_rest.astype(BF16)
    l_lo = (l_rest - l_mid.astype(F32)).astype(BF16)
    seq_rows = lambda x, b: x[b * C:(b + 1) * C]
    cs_b = [jnp.dot(tri3, jnp.concatenate([seq_rows(l_hi, b), seq_rows(l_mid, b), seq_rows(l_lo, b)],
                                          axis=0), preferred_element_type=F32) for b in range(NB)]
    cs = jnp.concatenate(cs_b, axis=0)
    cs_last = [c[C - 1:C, :] for c in cs_b]
    cs_end = jnp.concatenate([jnp.broadcast_to(c, (C, W)) for c in cs_last], axis=0)
    e_neg = jnp.exp(-cs)
    e_tail = jnp.exp(cs_end - cs)
    p_last = [jnp.exp(c) for c in cs_last]

    b_raw = kk * a
    r_t = (r * jnp.exp(cs)).astype(BF16)
    a_t = (-kk * jnp.exp(cs - logw)).astype(BF16)
    b_t = (b_raw * e_neg).astype(BF16)
    k_t = (k2 * e_neg).astype(BF16)
    b_p = (b_raw * e_tail).astype(BF16)
    k_p = (k2 * e_tail).astype(BF16)
    v_b = v.astype(BF16)

    def units_of(x):
        return [x[b * C:(b + 1) * C, i * G:(i + 1) * G] for b in range(NB) for i in range(n_groups)]

    rg, ag, bg, kg, vg, bpg, kpg = map(units_of, (r_t, a_t, b_t, k_t, v_b, b_p, k_p))
    units = range(NB * n_groups)
    stack = lambda p, q: jnp.concatenate([p, q], axis=0)
    lhs = [stack(ag[i], rg[i]) for i in units]
    sc = [_mm(lhs[i], stack(bd(bg[i]), bd(kg[i])), "nt") for i in units]
    a_ab = [jnp.where(strict_p, sc[i][:C, :G], 0.0) for i in units]
    a_rb = [jnp.where(incl_p, sc[i][C:, :G], 0.0) for i in units]
    a_ak = [jnp.where(strict_p, sc[i][:C, G:], 0.0) for i in units]
    a_rk = [jnp.where(incl_p, sc[i][C:, G:], 0.0) for i in units]

    inv = [eye_p + jnp.where(half_pair[0], a_ab[i], 0.0) for i in units]
    inv_b = [x.astype(BF16) for x in inv]
    for lvl in range(1, len(half_pair)):
        n_inv = [_mm(jnp.where(half_pair[lvl], a_ab[i], 0.0), bd(inv_b[i])) for i in units]
        inv = [inv[i] + _mm(inv_b[i], bd(n_inv[i].astype(BF16))) for i in units]
        inv_b = [x.astype(BF16) for x in inv]

    av_yv = [_mm(stack(a_ak[i], a_rk[i]), bd(vg[i])) for i in units]
    hat = [_mm(inv_b[i], jnp.concatenate([bd(ag[i]), bd(av_yv[i][:C].astype(BF16))], axis=1))
           for i in units]

    s0 = [state_ref[i] for i in units]
    res = [_mm(stack(hat[i][:, :G].astype(BF16), rg[i]), s0[i], "nt") for i in units]
    u = [(res[i][:C] + hat[i][:, G:]).astype(BF16) for i in units]
    upd = [_mm(stack(u[i], vg[i]), stack(bpg[i], kpg[i]), "tn") for i in units]
    for i in units:
        b, gidx = divmod(i, n_groups)
        state_ref[i] = (s0[i] * p_last[b][:, gidx * G:(gidx + 1) * G]
                        + jnp.where(same_head, upd[i], 0.0))
    y_units = [res[i][C:] + _mm(a_rb[i], bd(u[i])) + av_yv[i][C:] for i in units]

    y = jnp.concatenate([jnp.concatenate(y_units[b * n_groups:(b + 1) * n_groups], axis=1)
                         for b in range(NB)], axis=0)
    yc = y - head_sum(y, True) * (1.0 / HEAD_DIM)
    var = head_sum(yc * yc, False) * (1.0 / HEAD_DIM)
    yn = yc * lax.rsqrt(var + LNX_EPS) * pv(PV_LNG) + pv(PV_LNB)
    y_ref[...] = ((yn + bonus) * g).astype(BF16).reshape(NB, C, W)


def _mixer(proj, proj_lo, v_first, pvec, wa2, g2, v2, layer, *, batch, nb):
    Lp = proj.shape[1]
    W = pvec.shape[2]
    C = WKV_CHUNK
    has_vres = v_first is not None
    tok = lambda col: pl.BlockSpec((nb, C, W), lambda bp, c: (bp, c, col))
    par = lambda arr: pl.BlockSpec((None,) + arr.shape[1:], lambda bp, c: (layer, 0, 0))
    in_specs = [tok(col) for col in range(6)] + [pl.BlockSpec((nb, C, LO_W), lambda bp, c: (bp, c, 0))]
    args = [proj] * 6 + [proj_lo]
    if has_vres:
        in_specs.append(tok(0))
        args.append(v_first)
    in_specs += [par(pvec), par(wa2), par(g2)]
    args += [pvec, wa2, g2]
    token_bf16 = jax.ShapeDtypeStruct((batch, Lp, W), BF16)
    out_shape, out_specs = [token_bf16, token_bf16], [tok(0), tok(0)]
    if has_vres:
        in_specs.append(par(v2))
        args.append(v2)
    else:
        out_shape.append(jax.ShapeDtypeStruct((batch, Lp, W), F32))
        out_specs.append(tok(0))
    return pl.pallas_call(
        functools.partial(_mixer_kernel, has_vres),
        out_shape=tuple(out_shape),
        grid=(batch // nb, Lp // C),
        in_specs=in_specs,
        out_specs=tuple(out_specs),
        scratch_shapes=[
            pltpu.VMEM((nb * (W // LANES), LANES, LANES), F32),
            pltpu.VMEM((nb * SUBLANES, W), F32),
            pltpu.VMEM((nb * SUBLANES, W), F32),
        ],
        compiler_params=_cparams(("parallel", "arbitrary")),
        name="mixer",
    )(*args)


def _proj_out_kernel(yc_ref, yr_ref, w_ref, g_ref, h_ref, o_ref):
    half = yc_ref.shape[1]
    y = (jnp.dot(yc_ref[...], w_ref[0:half, :], preferred_element_type=F32)
         + jnp.dot(yr_ref[...], w_ref[half:2 * half, :], preferred_element_type=F32))
    o_ref[...] = h_ref[...] + _rms(y, g_ref[...])


def _proj_out(y_conv, y_rwkv, w_out, g, h, layer, *, tm):
    T, D = h.shape
    half = y_conv.shape[1]
    return pl.pallas_call(
        _proj_out_kernel,
        out_shape=jax.ShapeDtypeStruct((T, D), F32),
        grid=(T // tm,),
        in_specs=[
            pl.BlockSpec((tm, half), lambda i: (i, 0)),
            pl.BlockSpec((tm, half), lambda i: (i, 0)),
            pl.BlockSpec((None,) + w_out.shape[1:], lambda i: (layer, 0, 0)),
            _layer_vec(layer, D),
            pl.BlockSpec((tm, D), lambda i: (i, 0)),
        ],
        out_specs=pl.BlockSpec((tm, D), lambda i: (i, 0)),
        compiler_params=_cparams(("parallel",)),
        name="proj_out",
    )(y_conv, y_rwkv, w_out, g, h)


def _pad_axis(w, axis, size):
    pad = [(0, 0)] * w.ndim
    pad[axis] = (0, size - w.shape[axis])
    return jnp.pad(w, pad)


def _lora_cols(wd, ad, gd, vd):
    assert LO_AD == wd.shape[-1] and LO_GD == LO_AD + ad.shape[-1]
    return jnp.concatenate([wd, ad, _pad_axis(gd, -1, LO_GD_W), _pad_axis(vd, -1, LO_VD_W)], axis=-1)


def _first_layer_zero(w):
    return jnp.concatenate([jnp.zeros_like(w[:1]), w], axis=0)


def kernel(x, meta_tokens, ffn1_pre_g, ffn1_w_gu, ffn1_w_down, ffn1_post_g, mix_pre_g, w_in, w_in_vres, mu_rwkv, mu_vres, conv_w, conv_norm_g, decay_w0, decay_w2, iclr_a0, iclr_a2, vres_v0, vres_v2, gate_g2, k_k, k_a, r_k, lnx_g, lnx_b, w_out, mix_post_g, ffn2_pre_g, ffn2_w_gu, ffn2_w_down, ffn2_post_g):
    B, S, D = x.shape
    depth = w_in.shape[0]
    conv_width = conv_w.shape[2]
    rw = decay_w0.shape[1]
    L = N_META + S
    Lp = -(-L // WKV_CHUNK) * WKV_CHUNK
    T = B * Lp
    pick = lambda n, cands: next(c for c in cands if n % c == 0)
    tm = pick(T, (640, 512, 256, WKV_CHUNK))
    tm_proj = pick(T, (1040, 832, 640, 512, 256, WKV_CHUNK))
    tf = pick(ffn1_w_down.shape[1], (512, 256, 128))
    nb = pick(B, (4, 2, 1))
    o = 3 * conv_width + 3 * rw

    h = jnp.pad(x, ((0, 0), (N_META, Lp - L), (0, 0)))
    meta = jnp.broadcast_to(meta_tokens.astype(x.dtype)[None], (B, N_META, D))
    h = lax.dynamic_update_slice(h, meta, (0, 0, 0)).reshape(T, D)

    vec = lambda a: a.reshape(depth, 1, -1)
    ffn1 = (vec(ffn1_pre_g), ffn1_w_gu.astype(BF16), ffn1_w_down.astype(BF16), vec(0.5 * ffn1_post_g))
    ffn2 = (vec(ffn2_pre_g), ffn2_w_gu.astype(BF16), ffn2_w_down.astype(BF16), vec(0.5 * ffn2_post_g))
    w_main = w_in[..., :o].astype(BF16)
    w_lo = _lora_cols(w_in[..., o:o + R_DECAY], w_in[..., o + R_DECAY:o + R_DECAY + R_ICLR],
                      w_in[..., o + R_DECAY + R_ICLR:], _first_layer_zero(w_in_vres)).astype(BF16)
    m = 3 * rw
    mu_lo = _lora_cols(mu_rwkv[:, m:m + R_DECAY], mu_rwkv[:, m + R_DECAY:m + R_DECAY + R_ICLR],
                       mu_rwkv[:, m + R_DECAY + R_ICLR:], _first_layer_zero(mu_vres))
    pvec = jnp.stack([decay_w0, iclr_a0, _first_layer_zero(vres_v0), k_k, k_a, r_k.reshape(depth, rw),
                      lnx_g, lnx_b, mu_rwkv[:, :rw], mu_rwkv[:, rw:2 * rw], mu_rwkv[:, 2 * rw:m],
                      _pad_axis(mu_lo, 1, rw), conv_norm_g]
                     + [conv_w[:, tap] for tap in range(CONV_K)], axis=1)
    assert pvec.shape[1:] == (PV_ROWS, rw) and conv_w.shape[1] == CONV_K and conv_width == rw
    wa2 = jnp.concatenate([_pad_axis(decay_w2, 2, 2 * rw),
                           jnp.pad(iclr_a2, ((0, 0), (0, 0), (rw, 0)))], axis=1).astype(BF16)
    g2 = _pad_axis(gate_g2, 1, LO_GD_W).astype(BF16)
    v2 = _pad_axis(_first_layer_zero(vres_v2), 1, LO_VD_W).astype(BF16)
    w_out_b = w_out.astype(BF16)
    mix_pre, mix_post = vec(mix_pre_g), vec(mix_post_g)

    v_first = None
    for i in range(depth):
        h = _ffn(h, *ffn1, i, tm=tm, tf=tf)
        proj, proj_lo = _proj_in(h, mix_pre, w_main, o, w_lo, i, tm=tm_proj, tn=rw)
        proj3, proj_lo3 = proj.reshape(B, Lp, o), proj_lo.reshape(B, Lp, LO_W)
        if i == 0:
            y_conv, y_rwkv, v_first = _mixer(proj3, proj_lo3, None, pvec, wa2, g2, None, i, batch=B, nb=nb)
        else:
            y_conv, y_rwkv = _mixer(proj3, proj_lo3, v_first, pvec, wa2, g2, v2, i, batch=B, nb=nb)
        h = _proj_out(y_conv.reshape(T, rw), y_rwkv.reshape(T, rw), w_out_b, mix_post, h, i, tm=tm)
        if i < depth - 1:
            h = _ffn(h, *ffn2, i, tm=tm, tf=tf)
    out = _ffn(h, *ffn2, depth - 1, tm=pick(S, (512, 256, WKV_CHUNK)), tf=tf, rows=(Lp, N_META, S))
    return out.reshape(B, S, D)
```

```python
import functools
import math

import jax
import jax.numpy as jnp
from jax import lax
from jax.experimental import pallas as pl
from jax.experimental.pallas import tpu as pltpu

F32 = jnp.float32
BF16 = jnp.bfloat16

NORM_EPS = 1e-6
LNX_EPS = 64e-5
N_META = 16
HEAD_DIM = 64
WKV_CHUNK = 64
R_DECAY, R_ICLR, R_GATE, R_VRES = 64, 64, 160, 32
LO_WD, LO_AD, LO_GD, LO_VD = 0, 64, 128, 384
LO_GD_W = 256
LO_VD_W = 128
LO_W = 512
LANES = 128
GROUP_HEADS = LANES // HEAD_DIM
SUBLANES = 8

VMEM_LIMIT_BYTES = 56 * 1024 * 1024


def _cparams(sem):
    return pltpu.CompilerParams(dimension_semantics=sem, vmem_limit_bytes=VMEM_LIMIT_BYTES)


def _rms(x, g):
    ms = jnp.mean(x * x, axis=-1, keepdims=True)
    return x * lax.rsqrt(ms + NORM_EPS) * g


NORM_STRIP = 16


def _rms_strips(n_rows, read, write, g):
    pending = None
    for start in range(0, n_rows + NORM_STRIP, NORM_STRIP):
        if start < n_rows:
            rows = slice(start, start + NORM_STRIP)
            x = read(rows)
            inv_rms = lax.rsqrt(jnp.mean(x * x, axis=-1, keepdims=True) + NORM_EPS)
            upcoming = (rows, x, inv_rms)
        else:
            upcoming = None
        if pending is not None:
            rows_p, x_p, inv_rms_p = pending
            write(rows_p, x_p * inv_rms_p * g)
        pending = upcoming


def _layer_vec(layer, width):
    return pl.BlockSpec((None, 1, width), lambda *_: (layer, 0, 0))


def _ffn_kernel(h_ref, pre_g_ref, wg_ref, wu_ref, wd_ref, half_post_g_ref, o_ref, xn_ref):
    j = pl.program_id(2)

    n_rows = h_ref.shape[0]

    def down_projection():
        xn = xn_ref[...]
        gate = jnp.dot(xn, wg_ref[...], preferred_element_type=F32)
        up = jnp.dot(xn, wu_ref[...], preferred_element_type=F32)
        act = (gate * jax.nn.sigmoid(gate) * up).astype(BF16)
        return jnp.dot(act, wd_ref[...], preferred_element_type=F32)

    @pl.when(j == 0)
    def _():
        def write_xn(rows, y):
            xn_ref[rows, :] = y.astype(BF16)

        _rms_strips(n_rows, lambda rows: h_ref[rows, :], write_xn, pre_g_ref[...])
        o_ref[...] = down_projection()

    @pl.when(j > 0)
    def _():
        o_ref[...] += down_projection()

    @pl.when(j == pl.num_programs(2) - 1)
    def _():
        def write_out(rows, y):
            o_ref[rows, :] = h_ref[rows, :] + y

        _rms_strips(n_rows, lambda rows: o_ref[rows, :], write_out, half_post_g_ref[...])


def _ffn(h, pre_g, w_gu, w_down, half_post_g, layer, *, tm, tf, rows=None):
    T, D = h.shape
    FF = w_down.shape[1]
    nj = FF // tf
    period, start, count = rows or (T, 0, T)
    n_seq, per = T // period, count // tm
    if rows is None:
        h_spec = pl.BlockSpec((tm, D), lambda s, i, j: (i, 0))
    else:
        assert period % SUBLANES == 0 and start % SUBLANES == 0 and tm % SUBLANES == 0
        h_spec = pl.BlockSpec((pl.Element(tm), pl.Element(D)),
                              lambda s, i, j: (pl.multiple_of(s * period + start + i * tm, SUBLANES), 0))
    return pl.pallas_call(
        _ffn_kernel,
        out_shape=jax.ShapeDtypeStruct((n_seq * count, D), F32),
        grid=(n_seq, per, nj),
        in_specs=[
            h_spec,
            _layer_vec(layer, D),
            pl.BlockSpec((None, D, tf), lambda s, i, j: (layer, 0, j)),
            pl.BlockSpec((None, D, tf), lambda s, i, j: (layer, 0, j + nj)),
            pl.BlockSpec((None, tf, D), lambda s, i, j: (layer, j, 0)),
            _layer_vec(layer, D),
        ],
        out_specs=pl.BlockSpec((tm, D), lambda s, i, j: (s * per + i, 0)),
        scratch_shapes=[pltpu.VMEM((tm, D), BF16)],
        compiler_params=_cparams(("parallel", "parallel", "arbitrary")),
        name="ffn",
    )(h, pre_g, w_gu, w_gu, w_down, half_post_g)


def _proj_in_kernel(h_ref, g_ref, w_ref, w_lo_ref, o_ref, o_lo_ref, xn_ref):
    j = pl.program_id(1)
    last = pl.num_programs(1) - 1

    @pl.when(j == 0)
    def _():
        def write_xn(rows, y):
            xn_ref[rows, :] = y.astype(BF16)

        _rms_strips(h_ref.shape[0], lambda rows: h_ref[rows, :], write_xn, g_ref[...])

    @pl.when(j < last)
    def _():
        o_ref[...] = jnp.dot(xn_ref[...], w_ref[...], preferred_element_type=F32)

    @pl.when(j == last)
    def _():
        o_lo_ref[...] = jnp.dot(xn_ref[...], w_lo_ref[...], preferred_element_type=F32)


def _proj_in(h, g, w_main, n_main, w_lo, layer, *, tm, tn):
    T, D = h.shape
    n_lo = w_lo.shape[2]
    n_tiles = n_main // tn
    main_col = lambda j: jnp.minimum(j, n_tiles - 1)
    return pl.pallas_call(
        _proj_in_kernel,
        out_shape=(jax.ShapeDtypeStruct((T, n_main), F32), jax.ShapeDtypeStruct((T, n_lo), F32)),
        grid=(T // tm, n_tiles + 1),
        in_specs=[
            pl.BlockSpec((tm, D), lambda i, j: (i, 0)),
            _layer_vec(layer, D),
            pl.BlockSpec((None, D, tn), lambda i, j: (layer, 0, main_col(j))),
            pl.BlockSpec((None, D, n_lo), lambda i, j: (layer, 0, 0)),
        ],
        out_specs=(pl.BlockSpec((tm, tn), lambda i, j: (i, main_col(j))),
                   pl.BlockSpec((tm, n_lo), lambda i, j: (i, 0))),
        scratch_shapes=[pltpu.VMEM((tm, D), BF16)],
        compiler_params=_cparams(("parallel", "arbitrary")),
        name="proj_in",
    )(h, g, w_main, w_lo)


(PV_W0, PV_A0, PV_V0, PV_KK, PV_KA, PV_RK, PV_LNG, PV_LNB, PV_MUR, PV_MUK, PV_MUV, PV_MULO,
 PV_CONVG, PV_CONVW) = range(14)
CONV_K = 3
PV_ROWS = 16

_DIMS = {"nn": ((1,), (0,)), "nt": ((1,), (1,)), "tn": ((0,), (0,))}


def _split(x):
    hi = x.astype(BF16)
    return hi, (x - hi.astype(F32)).astype(BF16)


def _mm(x, w, form="nn", x_hp=False):
    dot = lambda p, q: lax.dot_general(p, q, (_DIMS[form], ((), ())), preferred_element_type=F32)
    w = w.astype(BF16)
    if not x_hp:
        return dot(x.astype(BF16), w)
    hi, lo = _split(x)
    return dot(hi, w) + dot(lo, w)


def _mixer_kernel(has_vres, *refs):
    if has_vres:
        (gb_ref, gc_ref, u_ref, r_ref, k_ref, v_ref, lo_ref, vf_ref, pv_ref, wa2_ref, g2_ref, v2_ref,
         yc_ref, y_ref, state_ref, carry_ref, conv_carry_ref) = refs
    else:
        (gb_ref, gc_ref, u_ref, r_ref, k_ref, v_ref, lo_ref, pv_ref, wa2_ref, g2_ref,
         yc_ref, y_ref, vf_out_ref, state_ref, carry_ref, conv_carry_ref) = refs
    NB, C, W = r_ref.shape
    R = NB * C
    G = LANES
    n_groups = W // G

    @pl.when(pl.program_id(1) == 0)
    def _():
        state_ref[...] = jnp.zeros_like(state_ref)
        carry_ref[...] = jnp.zeros_like(carry_ref)
        conv_carry_ref[...] = jnp.zeros_like(conv_carry_ref)

    def pv(i):
        return pv_ref[i:i + 1, :]

    trow = lax.broadcasted_iota(jnp.int32, (C, W), 0)
    for b in range(NB):
        uc = gc_ref[b] * u_ref[b]
        base = b * SUBLANES
        last2 = conv_carry_ref[base + 1:base + 2, :]
        last1 = conv_carry_ref[base + 2:base + 3, :]
        back1 = jnp.where(trow == 0, last1, pltpu.roll(uc, 1, 0))
        back2 = jnp.where(trow == 0, last2, jnp.where(trow == 1, last1, pltpu.roll(uc, 2, 0)))
        conv_carry_ref[base + 1:base + CONV_K, :] = uc[C - CONV_K + 1:C, :]
        conv = pv(PV_CONVW) * back2 + pv(PV_CONVW + 1) * back1 + pv(PV_CONVW + 2) * uc
        yc_ref[b] = _rms(gb_ref[b] * conv, pv(PV_CONVG)).astype(BF16)

    def token_shift(x_ref, slot, mu):
        width = x_ref.shape[-1]
        first = lax.broadcasted_iota(jnp.int32, (C, width), 0) == 0
        parts = []
        for b in range(NB):
            x = x_ref[b]
            crow = b * SUBLANES + slot
            prev = jnp.where(first, carry_ref[crow:crow + 1, :width], pltpu.roll(x, 1, 0))
            carry_ref[crow:crow + 1, :width] = x[C - 1:C, :]
            parts.append(x + mu[:, :width] * (prev - x))
        return jnp.concatenate(parts, axis=0)

    r = token_shift(r_ref, 0, pv(PV_MUR))
    k = token_shift(k_ref, 1, pv(PV_MUK))
    v = token_shift(v_ref, 2, pv(PV_MUV))
    lo = token_shift(lo_ref, 3, pv(PV_MULO))

    wa = lo[:, LO_WD:LO_WD + LANES]
    wa = jnp.where(lax.broadcasted_iota(jnp.int32, wa.shape, 1) < LO_AD, jnp.tanh(wa), wa)
    gd = lo[:, LO_GD:LO_GD + LO_GD_W]
    if has_vres:
        vd = lo[:, LO_VD:LO_VD + LO_VD_W]
        mix = jax.nn.sigmoid(pv(PV_V0) + _mm(vd, v2_ref[...]))
        v = v + (vf_ref[...].reshape(R, W) - v) * mix
    else:
        vf_out_ref[...] = v.reshape(NB, C, W)

    za = _mm(wa, wa2_ref[...])
    logw = -math.exp(-0.5) * jax.nn.sigmoid(pv(PV_W0) + za[:, :W])
    a = jax.nn.sigmoid(pv(PV_A0) + za[:, W:])
    g = _mm(jax.nn.sigmoid(gd), g2_ref[...])
    kk_raw = k * pv(PV_KK)
    k2 = k * (1.0 + (a - 1.0) * pv(PV_KA))

    gi = lax.broadcasted_iota(jnp.int32, (G, G), 0)
    gj = lax.broadcasted_iota(jnp.int32, (G, G), 1)
    same_head = (gi // HEAD_DIM) == (gj // HEAD_DIM)
    bd_mask = same_head.astype(F32).astype(BF16)
    ti = lax.broadcasted_iota(jnp.int32, (C, G), 0)
    si = lax.broadcasted_iota(jnp.int32, (C, G), 1) % HEAD_DIM
    incl_p, strict_p = ti >= si, ti > si
    eye_p = (ti == si).astype(F32)
    half_pair = []
    blk = 2
    while blk <= C:
        half_pair.append(((ti // blk) == (si // blk)) & ((ti // (blk // 2)) != (si // (blk // 2))))
        blk *= 2

    def bd(y):
        return jnp.concatenate([y] * GROUP_HEADS, axis=0) * bd_mask

    def group_cols(x):
        return [x[:, i * G:(i + 1) * G] for i in range(n_groups)]

    def head_sum(x, x_hp):
        s = _mm(jnp.concatenate(group_cols(x), axis=0), bd_mask, x_hp=x_hp)
        return jnp.concatenate([s[i * R:(i + 1) * R] for i in range(n_groups)], axis=1)

    kk = kk_raw * jnp.minimum(lax.rsqrt(head_sum(kk_raw * kk_raw, False)), 1e12)
    bonus = head_sum(r * k2 * pv(PV_RK), True) * v

    tc = lax.broadcasted_iota(jnp.int32, (C, C), 0)
    sc = lax.broadcasted_iota(jnp.int32, (C, C), 1)
    tri = (tc >= sc).astype(F32).astype(BF16)
    tri3 = jnp.concatenate([tri, tri, tri], axis=1)
    l_hi = logw.astype(BF16)
    l_rest = logw - l_hi.astype(F32)
    l_mid = l_rest.astype(BF16)
    l_lo = (l_rest - l_mid.astype(F32)).astype(BF16)
    seq_rows = lambda x, b: x[b * C:(b + 1) * C]
    cs_b = [jnp.dot(tri3, jnp.concatenate([seq_rows(l_hi, b), seq_rows(l_mid, b), seq_rows(l_lo, b)],
                                          axis=0), preferred_element_type=F32) for b in range(NB)]
    cs = jnp.concatenate(cs_b, axis=0)
    cs_last = [c[C - 1:C, :] for c in cs_b]
    cs_end = jnp.concatenate([jnp.broadcast_to(c, (C, W)) for c in cs_last], axis=0)
    e_neg = jnp.exp(-cs)
    e_tail = jnp.exp(cs_end - cs)
    p_last = [jnp.exp(c) for c in cs_last]

    b_raw = kk * a
    r_t = (r * jnp.exp(cs)).astype(BF16)
    a_t = (-kk * jnp.exp(cs - logw)).astype(BF16)
    b_t = (b_raw * e_neg).astype(BF16)
    k_t = (k2 * e_neg).astype(BF16)
    b_p = (b_raw * e_tail).astype(BF16)
    k_p = (k2 * e_tail).astype(BF16)
    v_b = v.astype(BF16)

    def units_of(x):
        return [x[b * C:(b + 1) * C, i * G:(i + 1) * G] for b in range(NB) for i in range(n_groups)]

    rg, ag, bg, kg, vg, bpg, kpg = map(units_of, (r_t, a_t, b_t, k_t, v_b, b_p, k_p))
    units = range(NB * n_groups)
    stack = lambda p, q: jnp.concatenate([p, q], axis=0)
    lhs = [stack(ag[i], rg[i]) for i in units]
    sc = [_mm(lhs[i], stack(bd(bg[i]), bd(kg[i])), "nt") for i in units]
    a_ab = [jnp.where(strict_p, sc[i][:C, :G], 0.0) for i in units]
    a_rb = [jnp.where(incl_p, sc[i][C:, :G], 0.0) for i in units]
    a_ak = [jnp.where(strict_p, sc[i][:C, G:], 0.0) for i in units]
    a_rk = [jnp.where(incl_p, sc[i][C:, G:], 0.0) for i in units]

    inv = [eye_p + jnp.where(half_pair[0], a_ab[i], 0.0) for i in units]
    inv_b = [x.astype(BF16) for x in inv]
    for lvl in range(1, len(half_pair)):
        n_inv = [_mm(jnp.where(half_pair[lvl], a_ab[i], 0.0), bd(inv_b[i])) for i in units]
        inv = [inv[i] + _mm(inv_b[i], bd(n_inv[i].astype(BF16))) for i in units]
        inv_b = [x.astype(BF16) for x in inv]

    av_yv = [_mm(stack(a_ak[i], a_rk[i]), bd(vg[i])) for i in units]
    hat = [_mm(inv_b[i], jnp.concatenate([bd(ag[i]), bd(av_yv[i][:C].astype(BF16))], axis=1))
           for i in units]

    s0 = [state_ref[i] for i in units]
    res = [_mm(stack(hat[i][:, :G].astype(BF16), rg[i]), s0[i], "nt") for i in units]
    u = [(res[i][:C] + hat[i][:, G:]).astype(BF16) for i in units]
    upd = [_mm(stack(u[i], vg[i]), stack(bpg[i], kpg[i]), "tn") for i in units]
    for i in units:
        b, gidx = divmod(i, n_groups)
        state_ref[i] = (s0[i] * p_last[b][:, gidx * G:(gidx + 1) * G]
                        + jnp.where(same_head, upd[i], 0.0))
    y_units = [res[i][C:] + _mm(a_rb[i], bd(u[i])) + av_yv[i][C:] for i in units]

    y = jnp.concatenate([jnp.concatenate(y_units[b * n_groups:(b + 1) * n_groups], axis=1)
                         for b in range(NB)], axis=0)
    yc = y - head_sum(y, True) * (1.0 / HEAD_DIM)
    var = head_sum(yc * yc, False) * (1.0 / HEAD_DIM)
    yn = yc * lax.rsqrt(var + LNX_EPS) * pv(PV_LNG) + pv(PV_LNB)
    y_ref[...] = ((yn + bonus) * g).astype(BF16).reshape(NB, C, W)


def _mixer(proj, proj_lo, v_first, pvec, wa2, g2, v2, layer, *, batch, nb):
    Lp = proj.shape[1]
    W = pvec.shape[2]
    C = WKV_CHUNK
    has_vres = v_first is not None
    tok = lambda col: pl.BlockSpec((nb, C, W), lambda bp, c: (bp, c, col))
    par = lambda arr: pl.BlockSpec((None,) + arr.shape[1:], lambda bp, c: (layer, 0, 0))
    in_specs = [tok(col) for col in range(6)] + [pl.BlockSpec((nb, C, LO_W), lambda bp, c: (bp, c, 0))]
    args = [proj] * 6 + [proj_lo]
    if has_vres:
        in_specs.append(tok(0))
        args.append(v_first)
    in_specs += [par(pvec), par(wa2), par(g2)]
    args += [pvec, wa2, g2]
    token_bf16 = jax.ShapeDtypeStruct((batch, Lp, W), BF16)
    out_shape, out_specs = [token_bf16, token_bf16], [tok(0), tok(0)]
    if has_vres:
        in_specs.append(par(v2))
        args.append(v2)
    else:
        out_shape.append(jax.ShapeDtypeStruct((batch, Lp, W), F32))
        out_specs.append(tok(0))
    return pl.pallas_call(
        functools.partial(_mixer_kernel, has_vres),
        out_shape=tuple(out_shape),
        grid=(batch // nb, Lp // C),
        in_specs=in_specs,
        out_specs=tuple(out_specs),
        scratch_shapes=[
            pltpu.VMEM((nb * (W // LANES), LANES, LANES), F32),
            pltpu.VMEM((nb * SUBLANES, W), F32),
            pltpu.VMEM((nb * SUBLANES, W), F32),
        ],
        compiler_params=_cparams(("parallel", "arbitrary")),
        name="mixer",
    )(*args)


def _proj_out_kernel(yc_ref, yr_ref, w_ref, g_ref, h_ref, o_ref):
    half = yc_ref.shape[1]
    y = (jnp.dot(yc_ref[...], w_ref[0:half, :], preferred_element_type=F32)
         + jnp.dot(yr_ref[...], w_ref[half:2 * half, :], preferred_element_type=F32))

    def write_out(rows, y_normed):
        o_ref[rows, :] = h_ref[rows, :] + y_normed

    _rms_strips(y.shape[0], lambda rows: y[rows, :], write_out, g_ref[...])


def _proj_out(y_conv, y_rwkv, w_out, g, h, layer, *, tm):
    T, D = h.shape
    half = y_conv.shape[1]
    return pl.pallas_call(
        _proj_out_kernel,
        out_shape=jax.ShapeDtypeStruct((T, D), F32),
        grid=(T // tm,),
        in_specs=[
            pl.BlockSpec((tm, half), lambda i: (i, 0)),
            pl.BlockSpec((tm, half), lambda i: (i, 0)),
            pl.BlockSpec((None,) + w_out.shape[1:], lambda i: (layer, 0, 0)),
            _layer_vec(layer, D),
            pl.BlockSpec((tm, D), lambda i: (i, 0)),
        ],
        out_specs=pl.BlockSpec((tm, D), lambda i: (i, 0)),
        compiler_params=_cparams(("parallel",)),
        name="proj_out",
    )(y_conv, y_rwkv, w_out, g, h)


def _pad_axis(w, axis, size):
    pad = [(0, 0)] * w.ndim
    pad[axis] = (0, size - w.shape[axis])
    return jnp.pad(w, pad)


def _lora_cols(wd, ad, gd, vd):
    assert LO_AD == wd.shape[-1] and LO_GD == LO_AD + ad.shape[-1]
    return jnp.concatenate([wd, ad, _pad_axis(gd, -1, LO_GD_W), _pad_axis(vd, -1, LO_VD_W)], axis=-1)


def _first_layer_zero(w):
    return jnp.concatenate([jnp.zeros_like(w[:1]), w], axis=0)


def kernel(x, meta_tokens, ffn1_pre_g, ffn1_w_gu, ffn1_w_down, ffn1_post_g, mix_pre_g, w_in, w_in_vres, mu_rwkv, mu_vres, conv_w, conv_norm_g, decay_w0, decay_w2, iclr_a0, iclr_a2, vres_v0, vres_v2, gate_g2, k_k, k_a, r_k, lnx_g, lnx_b, w_out, mix_post_g, ffn2_pre_g, ffn2_w_gu, ffn2_w_down, ffn2_post_g):
    B, S, D = x.shape
    depth = w_in.shape[0]
    conv_width = conv_w.shape[2]
    rw = decay_w0.shape[1]
    L = N_META + S
    Lp = -(-L // WKV_CHUNK) * WKV_CHUNK
    T = B * Lp
    pick = lambda n, cands: next(c for c in cands if n % c == 0)
    tm = pick(T, (640, 512, 256, WKV_CHUNK))
    tm_proj = pick(T, (1040, 832, 640, 512, 256, WKV_CHUNK))
    tf = pick(ffn1_w_down.shape[1], (512, 256, 128))
    nb = pick(B, (4, 2, 1))
    o = 3 * conv_width + 3 * rw

    h = jnp.pad(x, ((0, 0), (N_META, Lp - L), (0, 0)))
    meta = jnp.broadcast_to(meta_tokens.astype(x.dtype)[None], (B, N_META, D))
    h = lax.dynamic_update_slice(h, meta, (0, 0, 0)).reshape(T, D)

    vec = lambda a: a.reshape(depth, 1, -1)
    ffn1 = (vec(ffn1_pre_g), ffn1_w_gu.astype(BF16), ffn1_w_down.astype(BF16), vec(0.5 * ffn1_post_g))
    ffn2 = (vec(ffn2_pre_g), ffn2_w_gu.astype(BF16), ffn2_w_down.astype(BF16), vec(0.5 * ffn2_post_g))
    w_main = w_in.astype(BF16)
    w_lo = _lora_cols(w_in[..., o:o + R_DECAY], w_in[..., o + R_DECAY:o + R_DECAY + R_ICLR],
                      w_in[..., o + R_DECAY + R_ICLR:], _first_layer_zero(w_in_vres)).astype(BF16)
    m = 3 * rw
    mu_lo = _lora_cols(mu_rwkv[:, m:m + R_DECAY], mu_rwkv[:, m + R_DECAY:m + R_DECAY + R_ICLR],
                       mu_rwkv[:, m + R_DECAY + R_ICLR:], _first_layer_zero(mu_vres))
    pvec = jnp.stack([decay_w0, iclr_a0, _first_layer_zero(vres_v0), k_k, k_a, r_k.reshape(depth, rw),
                      lnx_g, lnx_b, mu_rwkv[:, :rw], mu_rwkv[:, rw:2 * rw], mu_rwkv[:, 2 * rw:m],
                      _pad_axis(mu_lo, 1, rw), conv_norm_g]
                     + [conv_w[:, tap] for tap in range(CONV_K)], axis=1)
    assert pvec.shape[1:] == (PV_ROWS, rw) and conv_w.shape[1] == CONV_K and conv_width == rw
    wa2 = jnp.concatenate([_pad_axis(decay_w2, 2, 2 * rw),
                           jnp.pad(iclr_a2, ((0, 0), (0, 0), (rw, 0)))], axis=1).astype(BF16)
    g2 = _pad_axis(gate_g2, 1, LO_GD_W).astype(BF16)
    v2 = _pad_axis(_first_layer_zero(vres_v2), 1, LO_VD_W).astype(BF16)
    w_out_b = w_out.astype(BF16)
    mix_pre, mix_post = vec(mix_pre_g), vec(mix_post_g)

    v_first = None
    for i in range(depth):
        h = _ffn(h, *ffn1, i, tm=tm, tf=tf)
        proj, proj_lo = _proj_in(h, mix_pre, w_main, o, w_lo, i, tm=tm_proj, tn=rw)
        proj3, proj_lo3 = proj.reshape(B, Lp, o), proj_lo.reshape(B, Lp, LO_W)
        if i == 0:
            y_conv, y_rwkv, v_first = _mixer(proj3, proj_lo3, None, pvec, wa2, g2, None, i, batch=B, nb=nb)
        else:
            y_conv, y_rwkv = _mixer(proj3, proj_lo3, v_first, pvec, wa2, g2, v2, i, batch=B, nb=nb)
        h = _proj_out(y_conv.reshape(T, rw), y_rwkv.reshape(T, rw), w_out_b, mix_post, h, i, tm=tm)
        if i < depth - 1:
            h = _ffn(h, *ffn2, i, tm=tm, tf=tf)
    out = _ffn(h, *ffn2, depth - 1, tm=pick(S, (512, 256, WKV_CHUNK)), tf=tf, rows=(Lp, N_META, S))
    return out.reshape(B, S, D)
```

```python
import functools
import math

import jax
import jax.numpy as jnp
from jax import lax
from jax.experimental import pallas as pl
from jax.experimental.pallas import tpu as pltpu

F32 = jnp.float32
BF16 = jnp.bfloat16

NORM_EPS = 1e-6
LNX_EPS = 64e-5
N_META = 16
HEAD_DIM = 64
WKV_CHUNK = 64
R_DECAY, R_ICLR, R_GATE, R_VRES = 64, 64, 160, 32
LO_WD, LO_AD, LO_GD, LO_VD = 0, 64, 128, 384
LO_GD_W = 256
LO_VD_W = 128
LO_W = 512
LANES = 128
GROUP_HEADS = LANES // HEAD_DIM
SUBLANES = 8

VMEM_LIMIT_BYTES = 60 * 1024 * 1024


def _cparams(sem):
    return pltpu.CompilerParams(dimension_semantics=sem, vmem_limit_bytes=VMEM_LIMIT_BYTES)


def _rms(x, g):
    ms = jnp.mean(x * x, axis=-1, keepdims=True)
    return x * lax.rsqrt(ms + NORM_EPS) * g


NORM_STRIP = 16


def _rms_strips(n_rows, read, write, g):
    pending = None
    for start in range(0, n_rows + NORM_STRIP, NORM_STRIP):
        if start < n_rows:
            rows = slice(start, start + NORM_STRIP)
            x = read(rows)
            inv_rms = lax.rsqrt(jnp.mean(x * x, axis=-1, keepdims=True) + NORM_EPS)
            upcoming = (rows, x, inv_rms)
        else:
            upcoming = None
        if pending is not None:
            rows_p, x_p, inv_rms_p = pending
            write(rows_p, x_p * inv_rms_p * g)
        pending = upcoming


def _layer_vec(layer, width):
    return pl.BlockSpec((None, 1, width), lambda *_: (layer, 0, 0))


def _ffn_kernel(h_ref, pre_g_ref, wg_ref, wu_ref, wd_ref, half_post_g_ref, o_ref, xn_ref):
    j = pl.program_id(2)

    n_rows = h_ref.shape[0]

    def down_projection():
        xn = xn_ref[...]
        gate = jnp.dot(xn, wg_ref[...], preferred_element_type=F32)
        up = jnp.dot(xn, wu_ref[...], preferred_element_type=F32)
        act = (gate * jax.nn.sigmoid(gate) * up).astype(BF16)
        return jnp.dot(act, wd_ref[...], preferred_element_type=F32)

    @pl.when(j == 0)
    def _():
        def write_xn(rows, y):
            xn_ref[rows, :] = y.astype(BF16)

        _rms_strips(n_rows, lambda rows: h_ref[rows, :], write_xn, pre_g_ref[...])
        o_ref[...] = down_projection()

    @pl.when(j > 0)
    def _():
        o_ref[...] += down_projection()

    @pl.when(j == pl.num_programs(2) - 1)
    def _():
        def write_out(rows, y):
            o_ref[rows, :] = h_ref[rows, :] + y

        _rms_strips(n_rows, lambda rows: o_ref[rows, :], write_out, half_post_g_ref[...])


def _ffn(h, pre_g, w_gu, w_down, half_post_g, layer, *, tm, tf, rows=None):
    T, D = h.shape
    FF = w_down.shape[1]
    nj = FF // tf
    period, start, count = rows or (T, 0, T)
    n_seq, per = T // period, count // tm
    if rows is None:
        h_spec = pl.BlockSpec((tm, D), lambda s, i, j: (i, 0))
    else:
        assert period % SUBLANES == 0 and start % SUBLANES == 0 and tm % SUBLANES == 0
        h_spec = pl.BlockSpec((pl.Element(tm), pl.Element(D)),
                              lambda s, i, j: (pl.multiple_of(s * period + start + i * tm, SUBLANES), 0))
    return pl.pallas_call(
        _ffn_kernel,
        out_shape=jax.ShapeDtypeStruct((n_seq * count, D), F32),
        grid=(n_seq, per, nj),
        in_specs=[
            h_spec,
            _layer_vec(layer, D),
            pl.BlockSpec((None, D, tf), lambda s, i, j: (layer, 0, j)),
            pl.BlockSpec((None, D, tf), lambda s, i, j: (layer, 0, j + nj)),
            pl.BlockSpec((None, tf, D), lambda s, i, j: (layer, j, 0)),
            _layer_vec(layer, D),
        ],
        out_specs=pl.BlockSpec((tm, D), lambda s, i, j: (s * per + i, 0)),
        scratch_shapes=[pltpu.VMEM((tm, D), BF16)],
        compiler_params=_cparams(("parallel", "parallel", "arbitrary")),
        name="ffn",
    )(h, pre_g, w_gu, w_gu, w_down, half_post_g)


def _proj_in_kernel(h_ref, g_ref, w_ref, w_lo_ref, o_ref, o_lo_ref, xn_ref):
    j = pl.program_id(1)
    last = pl.num_programs(1) - 1

    @pl.when(j == 0)
    def _():
        def write_xn(rows, y):
            xn_ref[rows, :] = y.astype(BF16)

        _rms_strips(h_ref.shape[0], lambda rows: h_ref[rows, :], write_xn, g_ref[...])

    @pl.when(j < last)
    def _():
        o_ref[...] = jnp.dot(xn_ref[...], w_ref[...], preferred_element_type=F32)

    @pl.when(j == last)
    def _():
        o_lo_ref[...] = jnp.dot(xn_ref[...], w_lo_ref[...], preferred_element_type=F32)


def _proj_in(h, g, w_main, n_main, w_lo, layer, *, tm, tn):
    T, D = h.shape
    n_lo = w_lo.shape[2]
    n_tiles = n_main // tn
    main_col = lambda j: jnp.minimum(j, n_tiles - 1)
    return pl.pallas_call(
        _proj_in_kernel,
        out_shape=(jax.ShapeDtypeStruct((T, n_main), F32), jax.ShapeDtypeStruct((T, n_lo), F32)),
        grid=(T // tm, n_tiles + 1),
        in_specs=[
            pl.BlockSpec((tm, D), lambda i, j: (i, 0)),
            _layer_vec(layer, D),
            pl.BlockSpec((None, D, tn), lambda i, j: (layer, 0, main_col(j))),
            pl.BlockSpec((None, D, n_lo), lambda i, j: (layer, 0, 0)),
        ],
        out_specs=(pl.BlockSpec((tm, tn), lambda i, j: (i, main_col(j))),
                   pl.BlockSpec((tm, n_lo), lambda i, j: (i, 0))),
        scratch_shapes=[pltpu.VMEM((tm, D), BF16)],
        compiler_params=_cparams(("parallel", "arbitrary")),
        name="proj_in",
    )(h, g, w_main, w_lo)


(PV_W0, PV_A0, PV_V0, PV_KK, PV_KA, PV_RK, PV_LNG, PV_LNB, PV_MUR, PV_MUK, PV_MUV, PV_MULO,
 PV_CONVG, PV_CONVW) = range(14)
CONV_K = 3
PV_ROWS = 16

_DIMS = {"nn": ((1,), (0,)), "nt": ((1,), (1,)), "tn": ((0,), (0,))}


def _split(x):
    hi = x.astype(BF16)
    return hi, (x - hi.astype(F32)).astype(BF16)


def _mm(x, w, form="nn", x_hp=False):
    dot = lambda p, q: lax.dot_general(p, q, (_DIMS[form], ((), ())), preferred_element_type=F32)
    w = w.astype(BF16)
    if not x_hp:
        return dot(x.astype(BF16), w)
    hi, lo = _split(x)
    return dot(hi, w) + dot(lo, w)


def _mixer_kernel(has_vres, *refs):
    if has_vres:
        (gb_ref, gc_ref, u_ref, r_ref, k_ref, v_ref, lo_ref, vf_ref, pv_ref, wa2_ref, g2_ref, v2_ref,
         yc_ref, y_ref, state_ref, carry_ref, conv_carry_ref) = refs
    else:
        (gb_ref, gc_ref, u_ref, r_ref, k_ref, v_ref, lo_ref, pv_ref, wa2_ref, g2_ref,
         yc_ref, y_ref, vf_out_ref, state_ref, carry_ref, conv_carry_ref) = refs
    NB, C, W = r_ref.shape
    R = NB * C
    G = LANES
    n_groups = W // G

    @pl.when(pl.program_id(1) == 0)
    def _():
        state_ref[...] = jnp.zeros_like(state_ref)
        carry_ref[...] = jnp.zeros_like(carry_ref)
        conv_carry_ref[...] = jnp.zeros_like(conv_carry_ref)

    def pv(i):
        return pv_ref[i:i + 1, :]

    trow = lax.broadcasted_iota(jnp.int32, (C, W), 0)
    for b in range(NB):
        uc = gc_ref[b] * u_ref[b]
        base = b * SUBLANES
        last2 = conv_carry_ref[base + 1:base + 2, :]
        last1 = conv_carry_ref[base + 2:base + 3, :]
        back1 = jnp.where(trow == 0, last1, pltpu.roll(uc, 1, 0))
        back2 = jnp.where(trow == 0, last2, jnp.where(trow == 1, last1, pltpu.roll(uc, 2, 0)))
        conv_carry_ref[base + 1:base + CONV_K, :] = uc[C - CONV_K + 1:C, :]
        conv = pv(PV_CONVW) * back2 + pv(PV_CONVW + 1) * back1 + pv(PV_CONVW + 2) * uc
        yc_ref[b] = _rms(gb_ref[b] * conv, pv(PV_CONVG)).astype(BF16)

    def token_shift(x_ref, slot, mu):
        width = x_ref.shape[-1]
        first = lax.broadcasted_iota(jnp.int32, (C, width), 0) == 0
        parts = []
        for b in range(NB):
            x = x_ref[b]
            crow = b * SUBLANES + slot
            prev = jnp.where(first, carry_ref[crow:crow + 1, :width], pltpu.roll(x, 1, 0))
            carry_ref[crow:crow + 1, :width] = x[C - 1:C, :]
            parts.append(x + mu[:, :width] * (prev - x))
        return jnp.concatenate(parts, axis=0)

    r = token_shift(r_ref, 0, pv(PV_MUR))
    k = token_shift(k_ref, 1, pv(PV_MUK))
    v = token_shift(v_ref, 2, pv(PV_MUV))
    lo = token_shift(lo_ref, 3, pv(PV_MULO))

    wa = lo[:, LO_WD:LO_WD + LANES]
    wa = jnp.where(lax.broadcasted_iota(jnp.int32, wa.shape, 1) < LO_AD, jnp.tanh(wa), wa)
    gd = lo[:, LO_GD:LO_GD + LO_GD_W]
    if has_vres:
        vd = lo[:, LO_VD:LO_VD + LO_VD_W]
        mix = jax.nn.sigmoid(pv(PV_V0) + _mm(vd, v2_ref[...]))
        v = v + (vf_ref[...].reshape(R, W) - v) * mix
    else:
        vf_out_ref[...] = v.reshape(NB, C, W)

    za = _mm(wa, wa2_ref[...])
    logw = -math.exp(-0.5) * jax.nn.sigmoid(pv(PV_W0) + za[:, :W])
    a = jax.nn.sigmoid(pv(PV_A0) + za[:, W:])
    g = _mm(jax.nn.sigmoid(gd), g2_ref[...])
    kk_raw = k * pv(PV_KK)
    k2 = k * (1.0 + (a - 1.0) * pv(PV_KA))

    gi = lax.broadcasted_iota(jnp.int32, (G, G), 0)
    gj = lax.broadcasted_iota(jnp.int32, (G, G), 1)
    same_head = (gi // HEAD_DIM) == (gj // HEAD_DIM)
    bd_mask = same_head.astype(F32).astype(BF16)
    ti = lax.broadcasted_iota(jnp.int32, (C, G), 0)
    si = lax.broadcasted_iota(jnp.int32, (C, G), 1) % HEAD_DIM
    incl_p, strict_p = ti >= si, ti > si
    eye_p = (ti == si).astype(F32)
    half_pair = []
    blk = 2
    while blk <= C:
        half_pair.append(((ti // blk) == (si // blk)) & ((ti // (blk // 2)) != (si // (blk // 2))))
        blk *= 2

    def bd(y):
        return jnp.concatenate([y] * GROUP_HEADS, axis=0) * bd_mask

    def group_cols(x):
        return [x[:, i * G:(i + 1) * G] for i in range(n_groups)]

    def head_sum(x, x_hp):
        s = _mm(jnp.concatenate(group_cols(x), axis=0), bd_mask, x_hp=x_hp)
        return jnp.concatenate([s[i * R:(i + 1) * R] for i in range(n_groups)], axis=1)

    kk = kk_raw * jnp.minimum(lax.rsqrt(head_sum(kk_raw * kk_raw, False)), 1e12)
    bonus = head_sum(r * k2 * pv(PV_RK), True) * v

    tc = lax.broadcasted_iota(jnp.int32, (C, C), 0)
    sc = lax.broadcasted_iota(jnp.int32, (C, C), 1)
    tri = (tc >= sc).astype(F32).astype(BF16)
    tri3 = jnp.concatenate([tri, tri, tri], axis=1)
    l_hi = logw.astype(BF16)
    l_rest = logw - l_hi.astype(F32)
    l_mid = l_rest.astype(BF16)
    l_lo = (l_rest - l_mid.astype(F32)).astype(BF16)
    seq_rows = lambda x, b: x[b * C:(b + 1) * C]
    cs_b = [jnp.dot(tri3, jnp.concatenate([seq_rows(l_hi, b), seq_rows(l_mid, b), seq_rows(l_lo, b)],
                                          axis=0), preferred_element_type=F32) for b in range(NB)]
    cs = jnp.concatenate(cs_b, axis=0)
    cs_last = [c[C - 1:C, :] for c in cs_b]
    cs_end = jnp.concatenate([jnp.broadcast_to(c, (C, W)) for c in cs_last], axis=0)
    e_neg = jnp.exp(-cs)
    e_tail = jnp.exp(cs_end - cs)
    p_last = [jnp.exp(c) for c in cs_last]

    b_raw = kk * a
    r_t = (r * jnp.exp(cs)).astype(BF16)
    a_t = (-kk * jnp.exp(cs - logw)).astype(BF16)
    b_t = (b_raw * e_neg).astype(BF16)
    k_t = (k2 * e_neg).astype(BF16)
    b_p = (b_raw * e_tail).astype(BF16)
    k_p = (k2 * e_tail).astype(BF16)
    v_b = v.astype(BF16)

    def units_of(x):
        return [x[b * C:(b + 1) * C, i * G:(i + 1) * G] for b in range(NB) for i in range(n_groups)]

    rg, ag, bg, kg, vg, bpg, kpg = map(units_of, (r_t, a_t, b_t, k_t, v_b, b_p, k_p))
    units = range(NB * n_groups)
    stack = lambda p, q: jnp.concatenate([p, q], axis=0)
    lhs = [stack(ag[i], rg[i]) for i in units]
    sc = [_mm(lhs[i], stack(bd(bg[i]), bd(kg[i])), "nt") for i in units]
    a_ab = [jnp.where(strict_p, sc[i][:C, :G], 0.0) for i in units]
    a_rb = [jnp.where(incl_p, sc[i][C:, :G], 0.0) for i in units]
    a_ak = [jnp.where(strict_p, sc[i][:C, G:], 0.0) for i in units]
    a_rk = [jnp.where(incl_p, sc[i][C:, G:], 0.0) for i in units]

    inv = [eye_p + jnp.where(half_pair[0], a_ab[i], 0.0) for i in units]
    inv_b = [x.astype(BF16) for x in inv]
    for lvl in range(1, len(half_pair)):
        n_inv = [_mm(jnp.where(half_pair[lvl], a_ab[i], 0.0), bd(inv_b[i])) for i in units]
        inv = [inv[i] + _mm(inv_b[i], bd(n_inv[i].astype(BF16))) for i in units]
        inv_b = [x.astype(BF16) for x in inv]

    av_yv = [_mm(stack(a_ak[i], a_rk[i]), bd(vg[i])) for i in units]
    hat = [_mm(inv_b[i], jnp.concatenate([bd(ag[i]), bd(av_yv[i][:C].astype(BF16))], axis=1))
           for i in units]

    s0 = [state_ref[i] for i in units]
    res = [_mm(stack(hat[i][:, :G].astype(BF16), rg[i]), s0[i], "nt") for i in units]
    u = [(res[i][:C] + hat[i][:, G:]).astype(BF16) for i in units]
    upd = [_mm(stack(u[i], vg[i]), stack(bpg[i], kpg[i]), "tn") for i in units]
    for i in units:
        b, gidx = divmod(i, n_groups)
        state_ref[i] = (s0[i] * p_last[b][:, gidx * G:(gidx + 1) * G]
                        + jnp.where(same_head, upd[i], 0.0))
    y_units = [res[i][C:] + _mm(a_rb[i], bd(u[i])) + av_yv[i][C:] for i in units]

    y = jnp.concatenate([jnp.concatenate(y_units[b * n_groups:(b + 1) * n_groups], axis=1)
                         for b in range(NB)], axis=0)
    yc = y - head_sum(y, True) * (1.0 / HEAD_DIM)
    var = head_sum(yc * yc, False) * (1.0 / HEAD_DIM)
    yn = yc * lax.rsqrt(var + LNX_EPS) * pv(PV_LNG) + pv(PV_LNB)
    y_ref[...] = ((yn + bonus) * g).astype(BF16).reshape(NB, C, W)


def _mixer(proj, proj_lo, v_first, pvec, wa2, g2, v2, layer, *, batch, nb):
    Lp = proj.shape[1]
    W = pvec.shape[2]
    C = WKV_CHUNK
    has_vres = v_first is not None
    tok = lambda col: pl.BlockSpec((nb, C, W), lambda bp, c: (bp, c, col))
    par = lambda arr: pl.BlockSpec((None,) + arr.shape[1:], lambda bp, c: (layer, 0, 0))
    in_specs = [tok(col) for col in range(6)] + [pl.BlockSpec((nb, C, LO_W), lambda bp, c: (bp, c, 0))]
    args = [proj] * 6 + [proj_lo]
    if has_vres:
        in_specs.append(tok(0))
        args.append(v_first)
    in_specs += [par(pvec), par(wa2), par(g2)]
    args += [pvec, wa2, g2]
    token_bf16 = jax.ShapeDtypeStruct((batch, Lp, W), BF16)
    out_shape, out_specs = [token_bf16, token_bf16], [tok(0), tok(0)]
    if has_vres:
        in_specs.append(par(v2))
        args.append(v2)
    else:
        out_shape.append(jax.ShapeDtypeStruct((batch, Lp, W), F32))
        out_specs.append(tok(0))
    return pl.pallas_call(
        functools.partial(_mixer_kernel, has_vres),
        out_shape=tuple(out_shape),
        grid=(batch // nb, Lp // C),
        in_specs=in_specs,
        out_specs=tuple(out_specs),
        scratch_shapes=[
            pltpu.VMEM((nb * (W // LANES), LANES, LANES), F32),
            pltpu.VMEM((nb * SUBLANES, W), F32),
            pltpu.VMEM((nb * SUBLANES, W), F32),
        ],
        compiler_params=_cparams(("parallel", "arbitrary")),
        name="mixer",
    )(*args)


def _proj_out_kernel(yc_ref, yr_ref, w_ref, g_ref, h_ref, o_ref):
    half = yc_ref.shape[1]
    y = (jnp.dot(yc_ref[...], w_ref[0:half, :], preferred_element_type=F32)
         + jnp.dot(yr_ref[...], w_ref[half:2 * half, :], preferred_element_type=F32))

    def write_out(rows, y_normed):
        o_ref[rows, :] = h_ref[rows, :] + y_normed

    _rms_strips(y.shape[0], lambda rows: y[rows, :], write_out, g_ref[...])


def _proj_out(y_conv, y_rwkv, w_out, g, h, layer, *, tm):
    T, D = h.shape
    half = y_conv.shape[1]
    return pl.pallas_call(
        _proj_out_kernel,
        out_shape=jax.ShapeDtypeStruct((T, D), F32),
        grid=(T // tm,),
        in_specs=[
            pl.BlockSpec((tm, half), lambda i: (i, 0)),
            pl.BlockSpec((tm, half), lambda i: (i, 0)),
            pl.BlockSpec((None,) + w_out.shape[1:], lambda i: (layer, 0, 0)),
            _layer_vec(layer, D),
            pl.BlockSpec((tm, D), lambda i: (i, 0)),
        ],
        out_specs=pl.BlockSpec((tm, D), lambda i: (i, 0)),
        compiler_params=_cparams(("parallel",)),
        name="proj_out",
    )(y_conv, y_rwkv, w_out, g, h)


def _pad_axis(w, axis, size):
    pad = [(0, 0)] * w.ndim
    pad[axis] = (0, size - w.shape[axis])
    return jnp.pad(w, pad)


def _lora_cols(wd, ad, gd, vd):
    assert LO_AD == wd.shape[-1] and LO_GD == LO_AD + ad.shape[-1]
    return jnp.concatenate([wd, ad, _pad_axis(gd, -1, LO_GD_W), _pad_axis(vd, -1, LO_VD_W)], axis=-1)


def _first_layer_zero(w):
    return jnp.concatenate([jnp.zeros_like(w[:1]), w], axis=0)


def kernel(x, meta_tokens, ffn1_pre_g, ffn1_w_gu, ffn1_w_down, ffn1_post_g, mix_pre_g, w_in, w_in_vres, mu_rwkv, mu_vres, conv_w, conv_norm_g, decay_w0, decay_w2, iclr_a0, iclr_a2, vres_v0, vres_v2, gate_g2, k_k, k_a, r_k, lnx_g, lnx_b, w_out, mix_post_g, ffn2_pre_g, ffn2_w_gu, ffn2_w_down, ffn2_post_g):
    B, S, D = x.shape
    depth = w_in.shape[0]
    conv_width = conv_w.shape[2]
    rw = decay_w0.shape[1]
    L = N_META + S
    Lp = -(-L // WKV_CHUNK) * WKV_CHUNK
    T = B * Lp
    pick = lambda n, cands: next(c for c in cands if n % c == 0)
    tm = pick(T, (640, 512, 256, WKV_CHUNK))
    tm_proj = pick(T, (1280, 1040, 832, 640, 512, 256, WKV_CHUNK))
    tf = pick(ffn1_w_down.shape[1], (512, 256, 128))
    nb = pick(B, (4, 2, 1))
    o = 3 * conv_width + 3 * rw

    h = jnp.pad(x, ((0, 0), (N_META, Lp - L), (0, 0)))
    meta = jnp.broadcast_to(meta_tokens.astype(x.dtype)[None], (B, N_META, D))
    h = lax.dynamic_update_slice(h, meta, (0, 0, 0)).reshape(T, D)

    vec = lambda a: a.reshape(depth, 1, -1)
    ffn1 = (vec(ffn1_pre_g), ffn1_w_gu.astype(BF16), ffn1_w_down.astype(BF16), vec(0.5 * ffn1_post_g))
    ffn2 = (vec(ffn2_pre_g), ffn2_w_gu.astype(BF16), ffn2_w_down.astype(BF16), vec(0.5 * ffn2_post_g))
    w_main = w_in.astype(BF16)
    w_lo = _lora_cols(w_in[..., o:o + R_DECAY], w_in[..., o + R_DECAY:o + R_DECAY + R_ICLR],
                      w_in[..., o + R_DECAY + R_ICLR:], _first_layer_zero(w_in_vres)).astype(BF16)
    m = 3 * rw
    mu_lo = _lora_cols(mu_rwkv[:, m:m + R_DECAY], mu_rwkv[:, m + R_DECAY:m + R_DECAY + R_ICLR],
                       mu_rwkv[:, m + R_DECAY + R_ICLR:], _first_layer_zero(mu_vres))
    pvec = jnp.stack([decay_w0, iclr_a0, _first_layer_zero(vres_v0), k_k, k_a, r_k.reshape(depth, rw),
                      lnx_g, lnx_b, mu_rwkv[:, :rw], mu_rwkv[:, rw:2 * rw], mu_rwkv[:, 2 * rw:m],
                      _pad_axis(mu_lo, 1, rw), conv_norm_g]
                     + [conv_w[:, tap] for tap in range(CONV_K)], axis=1)
    assert pvec.shape[1:] == (PV_ROWS, rw) and conv_w.shape[1] == CONV_K and conv_width == rw
    wa2 = jnp.concatenate([_pad_axis(decay_w2, 2, 2 * rw),
                           jnp.pad(iclr_a2, ((0, 0), (0, 0), (rw, 0)))], axis=1).astype(BF16)
    g2 = _pad_axis(gate_g2, 1, LO_GD_W).astype(BF16)
    v2 = _pad_axis(_first_layer_zero(vres_v2), 1, LO_VD_W).astype(BF16)
    w_out_b = w_out.astype(BF16)
    mix_pre, mix_post = vec(mix_pre_g), vec(mix_post_g)

    v_first = None
    for i in range(depth):
        h = _ffn(h, *ffn1, i, tm=tm, tf=tf)
        proj, proj_lo = _proj_in(h, mix_pre, w_main, o, w_lo, i, tm=tm_proj, tn=rw)
        proj3, proj_lo3 = proj.reshape(B, Lp, o), proj_lo.reshape(B, Lp, LO_W)
        if i == 0:
            y_conv, y_rwkv, v_first = _mixer(proj3, proj_lo3, None, pvec, wa2, g2, None, i, batch=B, nb=nb)
        else:
            y_conv, y_rwkv = _mixer(proj3, proj_lo3, v_first, pvec, wa2, g2, v2, i, batch=B, nb=nb)
        h = _proj_out(y_conv.reshape(T, rw), y_rwkv.reshape(T, rw), w_out_b, mix_post, h, i, tm=tm)
        if i < depth - 1:
            h = _ffn(h, *ffn2, i, tm=tm, tf=tf)
    out = _ffn(h, *ffn2, depth - 1, tm=pick(S, (512, 256, WKV_CHUNK)), tf=tf, rows=(Lp, N_META, S))
    return out.reshape(B, S, D)
```

```python
import functools
import math

import jax
import jax.numpy as jnp
from jax import lax
from jax.experimental import pallas as pl
from jax.experimental.pallas import tpu as pltpu

F32 = jnp.float32
BF16 = jnp.bfloat16

NORM_EPS = 1e-6
LNX_EPS = 64e-5
N_META = 16
HEAD_DIM = 64
WKV_CHUNK = 64
R_DECAY, R_ICLR, R_GATE, R_VRES = 64, 64, 160, 32
LO_WD, LO_AD, LO_GD, LO_VD = 0, 64, 128, 384
LO_GD_W = 256
LO_VD_W = 128
LO_W = 512
LANES = 128
GROUP_HEADS = LANES // HEAD_DIM
SUBLANES = 8

VMEM_LIMIT_BYTES = 60 * 1024 * 1024


def _cparams(sem):
    return pltpu.CompilerParams(dimension_semantics=sem, vmem_limit_bytes=VMEM_LIMIT_BYTES)


def _rms(x, g):
    ms = jnp.mean(x * x, axis=-1, keepdims=True)
    return x * lax.rsqrt(ms + NORM_EPS) * g


NORM_STRIP = 16


def _rms_strips(n_rows, read, write, g):
    pending = None
    for start in range(0, n_rows + NORM_STRIP, NORM_STRIP):
        if start < n_rows:
            rows = slice(start, start + NORM_STRIP)
            x = read(rows)
            inv_rms = lax.rsqrt(jnp.mean(x * x, axis=-1, keepdims=True) + NORM_EPS)
            upcoming = (rows, x, inv_rms)
        else:
            upcoming = None
        if pending is not None:
            rows_p, x_p, inv_rms_p = pending
            write(rows_p, x_p * inv_rms_p * g)
        pending = upcoming


def _layer_vec(layer, width):
    return pl.BlockSpec((None, 1, width), lambda *_: (layer, 0, 0))


def _ffn_kernel(h_ref, pre_g_ref, wg_ref, wu_ref, wd_ref, half_post_g_ref, o_ref, xn_ref):
    j = pl.program_id(2)

    n_rows = h_ref.shape[0]

    def down_projection():
        xn = xn_ref[...]
        gate = jnp.dot(xn, wg_ref[...], preferred_element_type=F32)
        up = jnp.dot(xn, wu_ref[...], preferred_element_type=F32)
        act = (gate * jax.nn.sigmoid(gate) * up).astype(BF16)
        return jnp.dot(act, wd_ref[...], preferred_element_type=F32)

    @pl.when(j == 0)
    def _():
        def write_xn(rows, y):
            xn_ref[rows, :] = y.astype(BF16)

        _rms_strips(n_rows, lambda rows: h_ref[rows, :], write_xn, pre_g_ref[...])
        o_ref[...] = down_projection()

    @pl.when(j > 0)
    def _():
        o_ref[...] += down_projection()

    @pl.when(j == pl.num_programs(2) - 1)
    def _():
        def write_out(rows, y):
            o_ref[rows, :] = h_ref[rows, :] + y

        _rms_strips(n_rows, lambda rows: o_ref[rows, :], write_out, half_post_g_ref[...])


def _ffn(h, pre_g, w_gu, w_down, half_post_g, layer, *, tm, tf, rows=None):
    T, D = h.shape
    FF = w_down.shape[1]
    nj = FF // tf
    period, start, count = rows or (T, 0, T)
    n_seq, per = T // period, count // tm
    if rows is None:
        h_spec = pl.BlockSpec((tm, D), lambda s, i, j: (i, 0))
    else:
        assert period % SUBLANES == 0 and start % SUBLANES == 0 and tm % SUBLANES == 0
        h_spec = pl.BlockSpec((pl.Element(tm), pl.Element(D)),
                              lambda s, i, j: (pl.multiple_of(s * period + start + i * tm, SUBLANES), 0))
    return pl.pallas_call(
        _ffn_kernel,
        out_shape=jax.ShapeDtypeStruct((n_seq * count, D), F32),
        grid=(n_seq, per, nj),
        in_specs=[
            h_spec,
            _layer_vec(layer, D),
            pl.BlockSpec((None, D, tf), lambda s, i, j: (layer, 0, j)),
            pl.BlockSpec((None, D, tf), lambda s, i, j: (layer, 0, j + nj)),
            pl.BlockSpec((None, tf, D), lambda s, i, j: (layer, j, 0)),
            _layer_vec(layer, D),
        ],
        out_specs=pl.BlockSpec((tm, D), lambda s, i, j: (s * per + i, 0)),
        scratch_shapes=[pltpu.VMEM((tm, D), BF16)],
        compiler_params=_cparams(("parallel", "parallel", "arbitrary")),
        name="ffn",
    )(h, pre_g, w_gu, w_gu, w_down, half_post_g)


def _proj_in_kernel(h_ref, g_ref, w_ref, w_lo_ref, o_ref, o_lo_ref, xn_ref):
    j = pl.program_id(1)
    last = pl.num_programs(1) - 1

    @pl.when(j == 0)
    def _():
        def write_xn(rows, y):
            xn_ref[rows, :] = y.astype(BF16)

        _rms_strips(h_ref.shape[0], lambda rows: h_ref[rows, :], write_xn, g_ref[...])

    @pl.when(j < last)
    def _():
        o_ref[...] = jnp.dot(xn_ref[...], w_ref[...], preferred_element_type=F32)

    @pl.when(j == last)
    def _():
        o_lo_ref[...] = jnp.dot(xn_ref[...], w_lo_ref[...], preferred_element_type=F32)


def _proj_in(h, g, w_main, n_main, w_lo, layer, *, tm, tn):
    T, D = h.shape
    n_lo = w_lo.shape[2]
    n_tiles = n_main // tn
    main_col = lambda j: jnp.minimum(j, n_tiles - 1)
    return pl.pallas_call(
        _proj_in_kernel,
        out_shape=(jax.ShapeDtypeStruct((T, n_main), F32), jax.ShapeDtypeStruct((T, n_lo), F32)),
        grid=(T // tm, n_tiles + 1),
        in_specs=[
            pl.BlockSpec((tm, D), lambda i, j: (i, 0)),
            _layer_vec(layer, D),
            pl.BlockSpec((None, D, tn), lambda i, j: (layer, 0, main_col(j))),
            pl.BlockSpec((None, D, n_lo), lambda i, j: (layer, 0, 0)),
        ],
        out_specs=(pl.BlockSpec((tm, tn), lambda i, j: (i, main_col(j))),
                   pl.BlockSpec((tm, n_lo), lambda i, j: (i, 0))),
        scratch_shapes=[pltpu.VMEM((tm, D), BF16)],
        compiler_params=_cparams(("parallel", "arbitrary")),
        name="proj_in",
    )(h, g, w_main, w_lo)


(PV_W0, PV_A0, PV_V0, PV_KK, PV_KA, PV_RK, PV_LNG, PV_LNB, PV_MUR, PV_MUK, PV_MUV, PV_MULO,
 PV_CONVG, PV_CONVW) = range(14)
CONV_K = 3
PV_ROWS = 16

_DIMS = {"nn": ((1,), (0,)), "nt": ((1,), (1,)), "tn": ((0,), (0,))}


def _split(x):
    hi = x.astype(BF16)
    return hi, (x - hi.astype(F32)).astype(BF16)


def _mm(x, w, form="nn", x_hp=False):
    dot = lambda p, q: lax.dot_general(p, q, (_DIMS[form], ((), ())), preferred_element_type=F32)
    w = w.astype(BF16)
    if not x_hp:
        return dot(x.astype(BF16), w)
    hi, lo = _split(x)
    return dot(hi, w) + dot(lo, w)


def _mixer_kernel(has_vres, *refs):
    if has_vres:
        (gb_ref, gc_ref, u_ref, r_ref, k_ref, v_ref, lo_ref, vf_ref, pv_ref, wa2_ref, g2_ref, v2_ref,
         yc_ref, y_ref, state_ref, carry_ref, conv_carry_ref) = refs
    else:
        (gb_ref, gc_ref, u_ref, r_ref, k_ref, v_ref, lo_ref, pv_ref, wa2_ref, g2_ref,
         yc_ref, y_ref, vf_out_ref, state_ref, carry_ref, conv_carry_ref) = refs
    NB, C, W = r_ref.shape
    R = NB * C
    G = LANES
    n_groups = W // G

    @pl.when(pl.program_id(1) == 0)
    def _():
        state_ref[...] = jnp.zeros_like(state_ref)
        carry_ref[...] = jnp.zeros_like(carry_ref)
        conv_carry_ref[...] = jnp.zeros_like(conv_carry_ref)

    def pv(i):
        return pv_ref[i:i + 1, :]

    trow = lax.broadcasted_iota(jnp.int32, (C, W), 0)
    for b in range(NB):
        uc = gc_ref[b] * u_ref[b]
        base = b * SUBLANES
        last2 = conv_carry_ref[base + 1:base + 2, :]
        last1 = conv_carry_ref[base + 2:base + 3, :]
        back1 = jnp.where(trow == 0, last1, pltpu.roll(uc, 1, 0))
        back2 = jnp.where(trow == 0, last2, jnp.where(trow == 1, last1, pltpu.roll(uc, 2, 0)))
        conv_carry_ref[base + 1:base + CONV_K, :] = uc[C - CONV_K + 1:C, :]
        conv = pv(PV_CONVW) * back2 + pv(PV_CONVW + 1) * back1 + pv(PV_CONVW + 2) * uc
        yc_ref[b] = _rms(gb_ref[b] * conv, pv(PV_CONVG)).astype(BF16)

    def token_shift(x_ref, slot, mu):
        width = x_ref.shape[-1]
        first = lax.broadcasted_iota(jnp.int32, (C, width), 0) == 0
        parts = []
        for b in range(NB):
            x = x_ref[b]
            crow = b * SUBLANES + slot
            prev = jnp.where(first, carry_ref[crow:crow + 1, :width], pltpu.roll(x, 1, 0))
            carry_ref[crow:crow + 1, :width] = x[C - 1:C, :]
            parts.append(x + mu[:, :width] * (prev - x))
        return jnp.concatenate(parts, axis=0)

    r = token_shift(r_ref, 0, pv(PV_MUR))
    k = token_shift(k_ref, 1, pv(PV_MUK))
    v = token_shift(v_ref, 2, pv(PV_MUV))
    lo = token_shift(lo_ref, 3, pv(PV_MULO))

    wa = lo[:, LO_WD:LO_WD + LANES]
    wa = jnp.where(lax.broadcasted_iota(jnp.int32, wa.shape, 1) < LO_AD, jnp.tanh(wa), wa)
    gd = lo[:, LO_GD:LO_GD + LO_GD_W]
    if has_vres:
        vd = lo[:, LO_VD:LO_VD + LO_VD_W]
        mix = jax.nn.sigmoid(pv(PV_V0) + _mm(vd, v2_ref[...]))
        v = v + (vf_ref[...].reshape(R, W) - v) * mix
    else:
        vf_out_ref[...] = v.reshape(NB, C, W)

    za = _mm(wa, wa2_ref[...])
    logw = -math.exp(-0.5) * jax.nn.sigmoid(pv(PV_W0) + za[:, :W])
    a = jax.nn.sigmoid(pv(PV_A0) + za[:, W:])
    g = _mm(jax.nn.sigmoid(gd), g2_ref[...])
    kk_raw = k * pv(PV_KK)
    k2 = k * (1.0 + (a - 1.0) * pv(PV_KA))

    gi = lax.broadcasted_iota(jnp.int32, (G, G), 0)
    gj = lax.broadcasted_iota(jnp.int32, (G, G), 1)
    same_head = (gi // HEAD_DIM) == (gj // HEAD_DIM)
    bd_mask = same_head.astype(F32).astype(BF16)
    ti = lax.broadcasted_iota(jnp.int32, (C, G), 0)
    si = lax.broadcasted_iota(jnp.int32, (C, G), 1) % HEAD_DIM
    incl_p, strict_p = ti >= si, ti > si
    eye_p = (ti == si).astype(F32)
    half_pair = []
    blk = 2
    while blk <= C:
        half_pair.append(((ti // blk) == (si // blk)) & ((ti // (blk // 2)) != (si // (blk // 2))))
        blk *= 2

    def bd(y):
        return jnp.concatenate([y] * GROUP_HEADS, axis=0) * bd_mask

    def group_cols(x):
        return [x[:, i * G:(i + 1) * G] for i in range(n_groups)]

    def head_sum(x, x_hp):
        s = _mm(jnp.concatenate(group_cols(x), axis=0), bd_mask, x_hp=x_hp)
        return jnp.concatenate([s[i * R:(i + 1) * R] for i in range(n_groups)], axis=1)

    kk = kk_raw * jnp.minimum(lax.rsqrt(head_sum(kk_raw * kk_raw, False)), 1e12)
    bonus = head_sum(r * k2 * pv(PV_RK), True) * v

    tc = lax.broadcasted_iota(jnp.int32, (C, C), 0)
    sc = lax.broadcasted_iota(jnp.int32, (C, C), 1)
    tri = (tc >= sc).astype(F32).astype(BF16)
    tri3 = jnp.concatenate([tri, tri, tri], axis=1)
    l_hi = logw.astype(BF16)
    l_rest = logw - l_hi.astype(F32)
    l_mid = l_rest.astype(BF16)
    l_lo = (l_rest - l_mid.astype(F32)).astype(BF16)
    seq_rows = lambda x, b: x[b * C:(b + 1) * C]
    cs_b = [jnp.dot(tri3, jnp.concatenate([seq_rows(l_hi, b), seq_rows(l_mid, b), seq_rows(l_lo, b)],
                                          axis=0), preferred_element_type=F32) for b in range(NB)]
    cs = jnp.concatenate(cs_b, axis=0)
    cs_last = [c[C - 1:C, :] for c in cs_b]
    cs_end = jnp.concatenate([jnp.broadcast_to(c, (C, W)) for c in cs_last], axis=0)
    e_neg = jnp.exp(-cs)
    e_tail = jnp.exp(cs_end - cs)
    p_last = [jnp.exp(c) for c in cs_last]

    b_raw = kk * a
    r_t = (r * jnp.exp(cs)).astype(BF16)
    a_t = (-kk * jnp.exp(cs - logw)).astype(BF16)
    b_t = (b_raw * e_neg).astype(BF16)
    k_t = (k2 * e_neg).astype(BF16)
    b_p = (b_raw * e_tail).astype(BF16)
    k_p = (k2 * e_tail).astype(BF16)
    v_b = v.astype(BF16)

    def units_of(x):
        return [x[b * C:(b + 1) * C, i * G:(i + 1) * G] for b in range(NB) for i in range(n_groups)]

    rg, ag, bg, kg, vg, bpg, kpg = map(units_of, (r_t, a_t, b_t, k_t, v_b, b_p, k_p))
    units = range(NB * n_groups)
    stack = lambda p, q: jnp.concatenate([p, q], axis=0)
    lhs = [stack(ag[i], rg[i]) for i in units]
    sc = [_mm(lhs[i], stack(bd(bg[i]), bd(kg[i])), "nt") for i in units]
    a_ab = [jnp.where(strict_p, sc[i][:C, :G], 0.0) for i in units]
    a_rb = [jnp.where(incl_p, sc[i][C:, :G], 0.0) for i in units]
    a_ak = [jnp.where(strict_p, sc[i][:C, G:], 0.0) for i in units]
    a_rk = [jnp.where(incl_p, sc[i][C:, G:], 0.0) for i in units]

    inv = [eye_p + jnp.where(half_pair[0], a_ab[i], 0.0) for i in units]
    inv_b = [x.astype(BF16) for x in inv]
    for lvl in range(1, len(half_pair)):
        n_inv = [_mm(jnp.where(half_pair[lvl], a_ab[i], 0.0), bd(inv_b[i])) for i in units]
        inv = [inv[i] + _mm(inv_b[i], bd(n_inv[i].astype(BF16))) for i in units]
        inv_b = [x.astype(BF16) for x in inv]

    av_yv = [_mm(stack(a_ak[i], a_rk[i]), bd(vg[i])) for i in units]
    hat = [_mm(inv_b[i], jnp.concatenate([bd(ag[i]), bd(av_yv[i][:C].astype(BF16))], axis=1))
           for i in units]

    s0 = [state_ref[i] for i in units]
    res = [_mm(stack(hat[i][:, :G].astype(BF16), rg[i]), s0[i], "nt") for i in units]
    u = [(res[i][:C] + hat[i][:, G:]).astype(BF16) for i in units]
    upd = [_mm(stack(u[i], vg[i]), stack(bpg[i], kpg[i]), "tn") for i in units]
    for i in units:
        b, gidx = divmod(i, n_groups)
        state_ref[i] = (s0[i] * p_last[b][:, gidx * G:(gidx + 1) * G]
                        + jnp.where(same_head, upd[i], 0.0))
    y_units = [res[i][C:] + _mm(a_rb[i], bd(u[i])) + av_yv[i][C:] for i in units]

    y = jnp.concatenate([jnp.concatenate(y_units[b * n_groups:(b + 1) * n_groups], axis=1)
                         for b in range(NB)], axis=0)
    yc = y - head_sum(y, True) * (1.0 / HEAD_DIM)
    var = head_sum(yc * yc, False) * (1.0 / HEAD_DIM)
    yn = yc * lax.rsqrt(var + LNX_EPS) * pv(PV_LNG) + pv(PV_LNB)
    y_ref[...] = ((yn + bonus) * g).astype(BF16).reshape(NB, C, W)


def _mixer(proj, proj_lo, v_first, pvec, wa2, g2, v2, layer, *, batch, nb):
    Lp = proj.shape[1]
    W = pvec.shape[2]
    C = WKV_CHUNK
    has_vres = v_first is not None
    tok = lambda col: pl.BlockSpec((nb, C, W), lambda bp, c: (bp, c, col))
    par = lambda arr: pl.BlockSpec((None,) + arr.shape[1:], lambda bp, c: (layer, 0, 0))
    in_specs = [tok(col) for col in range(6)] + [pl.BlockSpec((nb, C, LO_W), lambda bp, c: (bp, c, 0))]
    args = [proj] * 6 + [proj_lo]
    if has_vres:
        in_specs.append(tok(0))
        args.append(v_first)
    in_specs += [par(pvec), par(wa2), par(g2)]
    args += [pvec, wa2, g2]
    token_bf16 = jax.ShapeDtypeStruct((batch, Lp, W), BF16)
    out_shape, out_specs = [token_bf16, token_bf16], [tok(0), tok(0)]
    if has_vres:
        in_specs.append(par(v2))
        args.append(v2)
    else:
        out_shape.append(jax.ShapeDtypeStruct((batch, Lp, W), F32))
        out_specs.append(tok(0))
    return pl.pallas_call(
        functools.partial(_mixer_kernel, has_vres),
        out_shape=tuple(out_shape),
        grid=(batch // nb, Lp // C),
        in_specs=in_specs,
        out_specs=tuple(out_specs),
        scratch_shapes=[
            pltpu.VMEM((nb * (W // LANES), LANES, LANES), F32),
            pltpu.VMEM((nb * SUBLANES, W), F32),
            pltpu.VMEM((nb * SUBLANES, W), F32),
        ],
        compiler_params=_cparams(("parallel", "arbitrary")),
        name="mixer",
    )(*args)


def _proj_out_kernel(yc_ref, yr_ref, w_ref, g_ref, h_ref, o_ref):
    half = yc_ref.shape[1]
    y = (jnp.dot(yc_ref[...], w_ref[0:half, :], preferred_element_type=F32)
         + jnp.dot(yr_ref[...], w_ref[half:2 * half, :], preferred_element_type=F32))

    def write_out(rows, y_normed):
        o_ref[rows, :] = h_ref[rows, :] + y_normed

    _rms_strips(y.shape[0], lambda rows: y[rows, :], write_out, g_ref[...])


def _proj_out(y_conv, y_rwkv, w_out, g, h, layer, *, tm):
    T, D = h.shape
    half = y_conv.shape[1]
    return pl.pallas_call(
        _proj_out_kernel,
        out_shape=jax.ShapeDtypeStruct((T, D), F32),
        grid=(T // tm,),
        in_specs=[
            pl.BlockSpec((tm, half), lambda i: (i, 0)),
            pl.BlockSpec((tm, half), lambda i: (i, 0)),
            pl.BlockSpec((None,) + w_out.shape[1:], lambda i: (layer, 0, 0)),
            _layer_vec(layer, D),
            pl.BlockSpec((tm, D), lambda i: (i, 0)),
        ],
        out_specs=pl.BlockSpec((tm, D), lambda i: (i, 0)),
        compiler_params=_cparams(("parallel",)),
        name="proj_out",
    )(y_conv, y_rwkv, w_out, g, h)


def _pad_axis(w, axis, size):
    pad = [(0, 0)] * w.ndim
    pad[axis] = (0, size - w.shape[axis])
    return jnp.pad(w, pad)


def _lora_cols(wd, ad, gd, vd):
    assert LO_AD == wd.shape[-1] and LO_GD == LO_AD + ad.shape[-1]
    return jnp.concatenate([wd, ad, _pad_axis(gd, -1, LO_GD_W), _pad_axis(vd, -1, LO_VD_W)], axis=-1)


def _first_layer_zero(w):
    return jnp.concatenate([jnp.zeros_like(w[:1]), w], axis=0)


def kernel(x, meta_tokens, ffn1_pre_g, ffn1_w_gu, ffn1_w_down, ffn1_post_g, mix_pre_g, w_in, w_in_vres, mu_rwkv, mu_vres, conv_w, conv_norm_g, decay_w0, decay_w2, iclr_a0, iclr_a2, vres_v0, vres_v2, gate_g2, k_k, k_a, r_k, lnx_g, lnx_b, w_out, mix_post_g, ffn2_pre_g, ffn2_w_gu, ffn2_w_down, ffn2_post_g):
    B, S, D = x.shape
    depth = w_in.shape[0]
    conv_width = conv_w.shape[2]
    rw = decay_w0.shape[1]
    L = N_META + S
    Lp = -(-L // WKV_CHUNK) * WKV_CHUNK
    T = B * Lp
    pick = lambda n, cands: next(c for c in cands if n % c == 0)
    tm = pick(T, (640, 512, 256, WKV_CHUNK))
    tm_proj = pick(T, (1280, 1040, 832, 640, 512, 256, WKV_CHUNK))
    tf = pick(ffn1_w_down.shape[1], (512, 256, 128))
    nb = pick(B, (4, 2, 1))
    o = 3 * conv_width + 3 * rw

    h = jnp.pad(x, ((0, 0), (N_META, Lp - L), (0, 0)))
    meta = jnp.broadcast_to(meta_tokens.astype(x.dtype)[None], (B, N_META, D))
    h = lax.dynamic_update_slice(h, meta, (0, 0, 0)).reshape(T, D)

    vec = lambda a: a.reshape(depth, 1, -1)
    ffn1 = (vec(ffn1_pre_g), ffn1_w_gu.astype(BF16), ffn1_w_down.astype(BF16), vec(0.5 * ffn1_post_g))
    ffn2 = (vec(ffn2_pre_g), ffn2_w_gu.astype(BF16), ffn2_w_down.astype(BF16), vec(0.5 * ffn2_post_g))
    w_main = w_in.astype(BF16)
    w_lo = _lora_cols(w_in[..., o:o + R_DECAY], w_in[..., o + R_DECAY:o + R_DECAY + R_ICLR],
                      w_in[..., o + R_DECAY + R_ICLR:], _first_layer_zero(w_in_vres)).astype(BF16)
    m = 3 * rw
    mu_lo = _lora_cols(mu_rwkv[:, m:m + R_DECAY], mu_rwkv[:, m + R_DECAY:m + R_DECAY + R_ICLR],
                       mu_rwkv[:, m + R_DECAY + R_ICLR:], _first_layer_zero(mu_vres))
    pvec = jnp.stack([decay_w0, iclr_a0, _first_layer_zero(vres_v0), k_k, k_a, r_k.reshape(depth, rw),
                      lnx_g, lnx_b, mu_rwkv[:, :rw], mu_rwkv[:, rw:2 * rw], mu_rwkv[:, 2 * rw:m],
                      _pad_axis(mu_lo, 1, rw), conv_norm_g]
                     + [conv_w[:, tap] for tap in range(CONV_K)], axis=1)
    assert pvec.shape[1:] == (PV_ROWS, rw) and conv_w.shape[1] == CONV_K and conv_width == rw
    wa2 = jnp.concatenate([_pad_axis(decay_w2, 2, 2 * rw),
                           jnp.pad(iclr_a2, ((0, 0), (0, 0), (rw, 0)))], axis=1).astype(BF16)
    g2 = _pad_axis(gate_g2, 1, LO_GD_W).astype(BF16)
    v2 = _pad_axis(_first_layer_zero(vres_v2), 1, LO_VD_W).astype(BF16)
    w_out_b = w_out.astype(BF16)
    mix_pre, mix_post = vec(mix_pre_g), vec(mix_post_g)

    v_first = None
    for i in range(depth):
        h = _ffn(h, *ffn1, i, tm=tm, tf=tf)
        proj, proj_lo = _proj_in(h, mix_pre, w_main, o, w_lo, i, tm=tm_proj, tn=rw)
        proj3, proj_lo3 = proj.reshape(B, Lp, o), proj_lo.reshape(B, Lp, LO_W)
        if i == 0:
            y_conv, y_rwkv, v_first = _mixer(proj3, proj_lo3, None, pvec, wa2, g2, None, i, batch=B, nb=nb)
        else:
            y_conv, y_rwkv = _mixer(proj3, proj_lo3, v_first, pvec, wa2, g2, v2, i, batch=B, nb=nb)
        h = _proj_out(y_conv.reshape(T, rw), y_rwkv.reshape(T, rw), w_out_b, mix_post, h, i, tm=tm)
        if i < depth - 1:
            h = _ffn(h, *ffn2, i, tm=tm, tf=tf)
    out = _ffn(h, *ffn2, depth - 1, tm=pick(S, (1024, 512, 256, WKV_CHUNK)), tf=tf, rows=(Lp, N_META, S))
    return out.reshape(B, S, D)
```

```python
import functools
import math

import jax
import jax.numpy as jnp
from jax import lax
from jax.experimental import pallas as pl
from jax.experimental.pallas import tpu as pltpu

F32 = jnp.float32
BF16 = jnp.bfloat16

NORM_EPS = 1e-6
LNX_EPS = 64e-5
N_META = 16
HEAD_DIM = 64
WKV_CHUNK = 64
R_DECAY, R_ICLR, R_GATE, R_VRES = 64, 64, 160, 32
LO_WD, LO_AD, LO_GD, LO_VD = 0, 64, 128, 384
LO_GD_W = 256
LO_VD_W = 128
LO_W = 512
LANES = 128
GROUP_HEADS = LANES // HEAD_DIM
SUBLANES = 8

VMEM_LIMIT_BYTES = 60 * 1024 * 1024


def _cparams(sem):
    return pltpu.CompilerParams(dimension_semantics=sem, vmem_limit_bytes=VMEM_LIMIT_BYTES)


def _rms(x, g):
    ms = jnp.mean(x * x, axis=-1, keepdims=True)
    return x * lax.rsqrt(ms + NORM_EPS) * g


NORM_STRIP = 16


def _rms_strips(n_rows, read, write, g):
    pending = None
    for start in range(0, n_rows + NORM_STRIP, NORM_STRIP):
        if start < n_rows:
            rows = slice(start, start + NORM_STRIP)
            x = read(rows)
            inv_rms = lax.rsqrt(jnp.mean(x * x, axis=-1, keepdims=True) + NORM_EPS)
            upcoming = (rows, x, inv_rms)
        else:
            upcoming = None
        if pending is not None:
            rows_p, x_p, inv_rms_p = pending
            write(rows_p, x_p * inv_rms_p * g)
        pending = upcoming


def _layer_vec(layer, width):
    return pl.BlockSpec((None, 1, width), lambda *_: (layer, 0, 0))


def _ffn_kernel(h_ref, pre_g_ref, wg_ref, wu_ref, wd_ref, half_post_g_ref, o_ref, xn_ref):
    j = pl.program_id(2)

    n_rows = h_ref.shape[0]

    def down_projection():
        xn = xn_ref[...]
        gate = jnp.dot(xn, wg_ref[...], preferred_element_type=F32)
        up = jnp.dot(xn, wu_ref[...], preferred_element_type=F32)
        act = (gate * jax.nn.sigmoid(gate) * up).astype(BF16)
        return jnp.dot(act, wd_ref[...], preferred_element_type=F32)

    @pl.when(j == 0)
    def _():
        def write_xn(rows, y):
            xn_ref[rows, :] = y.astype(BF16)

        _rms_strips(n_rows, lambda rows: h_ref[rows, :], write_xn, pre_g_ref[...])
        o_ref[...] = down_projection()

    @pl.when(j > 0)
    def _():
        o_ref[...] += down_projection()

    @pl.when(j == pl.num_programs(2) - 1)
    def _():
        def write_out(rows, y):
            o_ref[rows, :] = h_ref[rows, :] + y

        _rms_strips(n_rows, lambda rows: o_ref[rows, :], write_out, half_post_g_ref[...])


def _ffn(h, pre_g, w_gu, w_down, half_post_g, layer, *, tm, tf, rows=None):
    T, D = h.shape
    FF = w_down.shape[1]
    nj = FF // tf
    period, start, count = rows or (T, 0, T)
    n_seq, per = T // period, count // tm
    if rows is None:
        h_spec = pl.BlockSpec((tm, D), lambda s, i, j: (i, 0))
    else:
        assert period % SUBLANES == 0 and start % SUBLANES == 0 and tm % SUBLANES == 0
        h_spec = pl.BlockSpec((pl.Element(tm), pl.Element(D)),
                              lambda s, i, j: (pl.multiple_of(s * period + start + i * tm, SUBLANES), 0))
    return pl.pallas_call(
        _ffn_kernel,
        out_shape=jax.ShapeDtypeStruct((n_seq * count, D), F32),
        grid=(n_seq, per, nj),
        in_specs=[
            h_spec,
            _layer_vec(layer, D),
            pl.BlockSpec((None, D, tf), lambda s, i, j: (layer, 0, j)),
            pl.BlockSpec((None, D, tf), lambda s, i, j: (layer, 0, j + nj)),
            pl.BlockSpec((None, tf, D), lambda s, i, j: (layer, j, 0)),
            _layer_vec(layer, D),
        ],
        out_specs=pl.BlockSpec((tm, D), lambda s, i, j: (s * per + i, 0)),
        scratch_shapes=[pltpu.VMEM((tm, D), BF16)],
        compiler_params=_cparams(("parallel", "parallel", "arbitrary")),
        name="ffn",
    )(h, pre_g, w_gu, w_gu, w_down, half_post_g)


def _proj_in_kernel(h_ref, g_ref, w_ref, w_lo_ref, o_ref, o_lo_ref, xn_ref):
    j = pl.program_id(1)
    last = pl.num_programs(1) - 1

    @pl.when(j == 0)
    def _():
        def write_xn(rows, y):
            xn_ref[rows, :] = y.astype(BF16)

        _rms_strips(h_ref.shape[0], lambda rows: h_ref[rows, :], write_xn, g_ref[...])

    @pl.when(j < last)
    def _():
        o_ref[...] = jnp.dot(xn_ref[...], w_ref[...], preferred_element_type=F32)

    @pl.when(j == last)
    def _():
        o_lo_ref[...] = jnp.dot(xn_ref[...], w_lo_ref[...], preferred_element_type=F32)


def _proj_in(h, g, w_main, n_main, w_lo, layer, *, tm, tn):
    T, D = h.shape
    n_lo = w_lo.shape[2]
    n_tiles = n_main // tn
    main_col = lambda j: jnp.minimum(j, n_tiles - 1)
    return pl.pallas_call(
        _proj_in_kernel,
        out_shape=(jax.ShapeDtypeStruct((T, n_main), F32), jax.ShapeDtypeStruct((T, n_lo), F32)),
        grid=(T // tm, n_tiles + 1),
        in_specs=[
            pl.BlockSpec((tm, D), lambda i, j: (i, 0)),
            _layer_vec(layer, D),
            pl.BlockSpec((None, D, tn), lambda i, j: (layer, 0, main_col(j))),
            pl.BlockSpec((None, D, n_lo), lambda i, j: (layer, 0, 0)),
        ],
        out_specs=(pl.BlockSpec((tm, tn), lambda i, j: (i, main_col(j))),
                   pl.BlockSpec((tm, n_lo), lambda i, j: (i, 0))),
        scratch_shapes=[pltpu.VMEM((tm, D), BF16)],
        compiler_params=_cparams(("parallel", "arbitrary")),
        name="proj_in",
    )(h, g, w_main, w_lo)


(PV_W0, PV_A0, PV_V0, PV_KK, PV_KA, PV_RK, PV_LNG, PV_LNB, PV_MUR, PV_MUK, PV_MUV, PV_MULO,
 PV_CONVG, PV_CONVW) = range(14)
CONV_K = 3
PV_ROWS = 16

_DIMS = {"nn": ((1,), (0,)), "nt": ((1,), (1,)), "tn": ((0,), (0,))}


def _split(x):
    hi = x.astype(BF16)
    return hi, (x - hi.astype(F32)).astype(BF16)


def _mm(x, w, form="nn", x_hp=False):
    dot = lambda p, q: lax.dot_general(p, q, (_DIMS[form], ((), ())), preferred_element_type=F32)
    w = w.astype(BF16)
    if not x_hp:
        return dot(x.astype(BF16), w)
    hi, lo = _split(x)
    return dot(hi, w) + dot(lo, w)


def _mixer_kernel(has_vres, *refs):
    if has_vres:
        (gb_ref, gc_ref, u_ref, r_ref, k_ref, v_ref, lo_ref, vf_ref, pv_ref, wa2_ref, g2_ref, v2_ref,
         yc_ref, y_ref, state_ref, carry_ref, conv_carry_ref) = refs
    else:
        (gb_ref, gc_ref, u_ref, r_ref, k_ref, v_ref, lo_ref, pv_ref, wa2_ref, g2_ref,
         yc_ref, y_ref, vf_out_ref, state_ref, carry_ref, conv_carry_ref) = refs
    NB, C, W = r_ref.shape
    R = NB * C
    G = LANES
    n_groups = W // G

    @pl.when(pl.program_id(1) == 0)
    def _():
        state_ref[...] = jnp.zeros_like(state_ref)
        carry_ref[...] = jnp.zeros_like(carry_ref)
        conv_carry_ref[...] = jnp.zeros_like(conv_carry_ref)

    def pv(i):
        return pv_ref[i:i + 1, :]

    trow = lax.broadcasted_iota(jnp.int32, (C, W), 0)
    for b in range(NB):
        uc = gc_ref[b] * u_ref[b]
        base = b * SUBLANES
        last2 = conv_carry_ref[base + 1:base + 2, :]
        last1 = conv_carry_ref[base + 2:base + 3, :]
        back1 = jnp.where(trow == 0, last1, pltpu.roll(uc, 1, 0))
        back2 = jnp.where(trow == 0, last2, jnp.where(trow == 1, last1, pltpu.roll(uc, 2, 0)))
        conv_carry_ref[base + 1:base + CONV_K, :] = uc[C - CONV_K + 1:C, :]
        conv = pv(PV_CONVW) * back2 + pv(PV_CONVW + 1) * back1 + pv(PV_CONVW + 2) * uc
        yc_ref[b] = _rms(gb_ref[b] * conv, pv(PV_CONVG)).astype(BF16)

    def token_shift(x_ref, slot, mu):
        width = x_ref.shape[-1]
        first = lax.broadcasted_iota(jnp.int32, (C, width), 0) == 0
        parts = []
        for b in range(NB):
            x = x_ref[b]
            crow = b * SUBLANES + slot
            prev = jnp.where(first, carry_ref[crow:crow + 1, :width], pltpu.roll(x, 1, 0))
            carry_ref[crow:crow + 1, :width] = x[C - 1:C, :]
            parts.append(x + mu[:, :width] * (prev - x))
        return jnp.concatenate(parts, axis=0)

    r = token_shift(r_ref, 0, pv(PV_MUR))
    k = token_shift(k_ref, 1, pv(PV_MUK))
    v = token_shift(v_ref, 2, pv(PV_MUV))
    lo = token_shift(lo_ref, 3, pv(PV_MULO))

    wa = lo[:, LO_WD:LO_WD + LANES]
    wa = jnp.where(lax.broadcasted_iota(jnp.int32, wa.shape, 1) < LO_AD, jnp.tanh(wa), wa)
    gd = lo[:, LO_GD:LO_GD + LO_GD_W]
    if has_vres:
        vd = lo[:, LO_VD:LO_VD + LO_VD_W]
        mix = jax.nn.sigmoid(pv(PV_V0) + _mm(vd, v2_ref[...]))
        v = v + (vf_ref[...].reshape(R, W) - v) * mix
    else:
        vf_out_ref[...] = v.reshape(NB, C, W)

    za = _mm(wa, wa2_ref[...])
    logw = -math.exp(-0.5) * jax.nn.sigmoid(pv(PV_W0) + za[:, :W])
    a = jax.nn.sigmoid(pv(PV_A0) + za[:, W:])
    g = _mm(jax.nn.sigmoid(gd), g2_ref[...])
    kk_raw = k * pv(PV_KK)
    k2 = k * (1.0 + (a - 1.0) * pv(PV_KA))

    gi = lax.broadcasted_iota(jnp.int32, (G, G), 0)
    gj = lax.broadcasted_iota(jnp.int32, (G, G), 1)
    same_head = (gi // HEAD_DIM) == (gj // HEAD_DIM)
    bd_mask = same_head.astype(F32).astype(BF16)
    ti = lax.broadcasted_iota(jnp.int32, (C, G), 0)
    si = lax.broadcasted_iota(jnp.int32, (C, G), 1) % HEAD_DIM
    incl_p, strict_p = ti >= si, ti > si
    eye_p = (ti == si).astype(F32)
    half_pair = []
    blk = 2
    while blk <= C:
        half_pair.append(((ti // blk) == (si // blk)) & ((ti // (blk // 2)) != (si // (blk // 2))))
        blk *= 2

    def bd(y):
        return jnp.concatenate([y] * GROUP_HEADS, axis=0) * bd_mask

    def group_cols(x):
        return [x[:, i * G:(i + 1) * G] for i in range(n_groups)]

    def head_sum(x, x_hp):
        s = _mm(jnp.concatenate(group_cols(x), axis=0), bd_mask, x_hp=x_hp)
        return jnp.concatenate([s[i * R:(i + 1) * R] for i in range(n_groups)], axis=1)

    kk = kk_raw * jnp.minimum(lax.rsqrt(head_sum(kk_raw * kk_raw, False)), 1e12)
    bonus = head_sum(r * k2 * pv(PV_RK), True) * v

    tc = lax.broadcasted_iota(jnp.int32, (C, C), 0)
    sc = lax.broadcasted_iota(jnp.int32, (C, C), 1)
    tri = (tc >= sc).astype(F32).astype(BF16)
    tri3 = jnp.concatenate([tri, tri, tri], axis=1)
    l_hi = logw.astype(BF16)
    l_rest = logw - l_hi.astype(F32)
    l_mid = l_rest.astype(BF16)
    l_lo = (l_rest - l_mid.astype(F32)).astype(BF16)
    seq_rows = lambda x, b: x[b * C:(b + 1) * C]
    cs_b = [jnp.dot(tri3, jnp.concatenate([seq_rows(l_hi, b), seq_rows(l_mid, b), seq_rows(l_lo, b)],
                                          axis=0), preferred_element_type=F32) for b in range(NB)]
    cs = jnp.concatenate(cs_b, axis=0)
    cs_last = [c[C - 1:C, :] for c in cs_b]
    cs_end = jnp.concatenate([jnp.broadcast_to(c, (C, W)) for c in cs_last], axis=0)
    e_neg = jnp.exp(-cs)
    e_tail = jnp.exp(cs_end - cs)
    p_last = [jnp.exp(c) for c in cs_last]

    b_raw = kk * a
    r_t = (r * jnp.exp(cs)).astype(BF16)
    a_t = (-kk * jnp.exp(cs - logw)).astype(BF16)
    b_t = (b_raw * e_neg).astype(BF16)
    k_t = (k2 * e_neg).astype(BF16)
    b_p = (b_raw * e_tail).astype(BF16)
    k_p = (k2 * e_tail).astype(BF16)
    v_b = v.astype(BF16)

    def units_of(x):
        return [x[b * C:(b + 1) * C, i * G:(i + 1) * G] for b in range(NB) for i in range(n_groups)]

    rg, ag, bg, kg, vg, bpg, kpg = map(units_of, (r_t, a_t, b_t, k_t, v_b, b_p, k_p))
    units = range(NB * n_groups)
    stack = lambda p, q: jnp.concatenate([p, q], axis=0)
    lhs = [stack(ag[i], rg[i]) for i in units]
    sc = [_mm(lhs[i], stack(bd(bg[i]), bd(kg[i])), "nt") for i in units]
    a_ab = [jnp.where(strict_p, sc[i][:C, :G], 0.0) for i in units]
    a_rb = [jnp.where(incl_p, sc[i][C:, :G], 0.0) for i in units]
    a_ak = [jnp.where(strict_p, sc[i][:C, G:], 0.0) for i in units]
    a_rk = [jnp.where(incl_p, sc[i][C:, G:], 0.0) for i in units]

    inv = [eye_p + jnp.where(half_pair[0], a_ab[i], 0.0) for i in units]
    inv_b = [x.astype(BF16) for x in inv]
    for lvl in range(1, len(half_pair)):
        n_inv = [_mm(jnp.where(half_pair[lvl], a_ab[i], 0.0), bd(inv_b[i])) for i in units]
        inv = [inv[i] + _mm(inv_b[i], bd(n_inv[i].astype(BF16))) for i in units]
        inv_b = [x.astype(BF16) for x in inv]

    av_yv = [_mm(stack(a_ak[i], a_rk[i]), bd(vg[i])) for i in units]
    hat = [_mm(inv_b[i], jnp.concatenate([bd(ag[i]), bd(av_yv[i][:C].astype(BF16))], axis=1))
           for i in units]

    s0 = [state_ref[i] for i in units]
    res = [_mm(stack(hat[i][:, :G].astype(BF16), rg[i]), s0[i], "nt") for i in units]
    u = [(res[i][:C] + hat[i][:, G:]).astype(BF16) for i in units]
    upd = [_mm(stack(u[i], vg[i]), stack(bpg[i], kpg[i]), "tn") for i in units]
    for i in units:
        b, gidx = divmod(i, n_groups)
        state_ref[i] = (s0[i] * p_last[b][:, gidx * G:(gidx + 1) * G]
                        + jnp.where(same_head, upd[i], 0.0))
    y_units = [res[i][C:] + _mm(a_rb[i], bd(u[i])) + av_yv[i][C:] for i in units]

    y = jnp.concatenate([jnp.concatenate(y_units[b * n_groups:(b + 1) * n_groups], axis=1)
                         for b in range(NB)], axis=0)
    yc = y - head_sum(y, True) * (1.0 / HEAD_DIM)
    var = head_sum(yc * yc, False) * (1.0 / HEAD_DIM)
    yn = yc * lax.rsqrt(var + LNX_EPS) * pv(PV_LNG) + pv(PV_LNB)
    y_ref[...] = ((yn + bonus) * g).astype(BF16).reshape(NB, C, W)


def _mixer(proj, proj_lo, v_first, pvec, wa2, g2, v2, layer, *, batch, nb):
    Lp = proj.shape[1]
    W = pvec.shape[2]
    C = WKV_CHUNK
    has_vres = v_first is not None
    tok = lambda col: pl.BlockSpec((nb, C, W), lambda bp, c: (bp, c, col))
    par = lambda arr: pl.BlockSpec((None,) + arr.shape[1:], lambda bp, c: (layer, 0, 0))
    in_specs = [tok(col) for col in range(6)] + [pl.BlockSpec((nb, C, LO_W), lambda bp, c: (bp, c, 0))]
    args = [proj] * 6 + [proj_lo]
    if has_vres:
        in_specs.append(tok(0))
        args.append(v_first)
    in_specs += [par(pvec), par(wa2), par(g2)]
    args += [pvec, wa2, g2]
    token_bf16 = jax.ShapeDtypeStruct((batch, Lp, W), BF16)
    out_shape, out_specs = [token_bf16, token_bf16], [tok(0), tok(0)]
    if has_vres:
        in_specs.append(par(v2))
        args.append(v2)
    else:
        out_shape.append(jax.ShapeDtypeStruct((batch, Lp, W), F32))
        out_specs.append(tok(0))
    return pl.pallas_call(
        functools.partial(_mixer_kernel, has_vres),
        out_shape=tuple(out_shape),
        grid=(batch // nb, Lp // C),
        in_specs=in_specs,
        out_specs=tuple(out_specs),
        scratch_shapes=[
            pltpu.VMEM((nb * (W // LANES), LANES, LANES), F32),
            pltpu.VMEM((nb * SUBLANES, W), F32),
            pltpu.VMEM((nb * SUBLANES, W), F32),
        ],
        compiler_params=_cparams(("parallel", "arbitrary")),
        name="mixer",
    )(*args)


def _proj_out_kernel(yc_ref, yr_ref, w_ref, g_ref, h_ref, o_ref):
    half = yc_ref.shape[1]
    y = (jnp.dot(yc_ref[...], w_ref[0:half, :], preferred_element_type=F32)
         + jnp.dot(yr_ref[...], w_ref[half:2 * half, :], preferred_element_type=F32))

    def write_out(rows, y_normed):
        o_ref[rows, :] = h_ref[rows, :] + y_normed

    _rms_strips(y.shape[0], lambda rows: y[rows, :], write_out, g_ref[...])


def _proj_out(y_conv, y_rwkv, w_out, g, h, layer, *, tm):
    T, D = h.shape
    half = y_conv.shape[1]
    return pl.pallas_call(
        _proj_out_kernel,
        out_shape=jax.ShapeDtypeStruct((T, D), F32),
        grid=(T // tm,),
        in_specs=[
            pl.BlockSpec((tm, half), lambda i: (i, 0)),
            pl.BlockSpec((tm, half), lambda i: (i, 0)),
            pl.BlockSpec((None,) + w_out.shape[1:], lambda i: (layer, 0, 0)),
            _layer_vec(layer, D),
            pl.BlockSpec((tm, D), lambda i: (i, 0)),
        ],
        out_specs=pl.BlockSpec((tm, D), lambda i: (i, 0)),
        compiler_params=_cparams(("parallel",)),
        name="proj_out",
    )(y_conv, y_rwkv, w_out, g, h)


def _pad_axis(w, axis, size):
    pad = [(0, 0)] * w.ndim
    pad[axis] = (0, size - w.shape[axis])
    return jnp.pad(w, pad)


def _lora_cols(wd, ad, gd, vd):
    assert LO_AD == wd.shape[-1] and LO_GD == LO_AD + ad.shape[-1]
    return jnp.concatenate([wd, ad, _pad_axis(gd, -1, LO_GD_W), _pad_axis(vd, -1, LO_VD_W)], axis=-1)


def _first_layer_zero(w):
    return jnp.concatenate([jnp.zeros_like(w[:1]), w], axis=0)


def kernel(x, meta_tokens, ffn1_pre_g, ffn1_w_gu, ffn1_w_down, ffn1_post_g, mix_pre_g, w_in, w_in_vres, mu_rwkv, mu_vres, conv_w, conv_norm_g, decay_w0, decay_w2, iclr_a0, iclr_a2, vres_v0, vres_v2, gate_g2, k_k, k_a, r_k, lnx_g, lnx_b, w_out, mix_post_g, ffn2_pre_g, ffn2_w_gu, ffn2_w_down, ffn2_post_g):
    B, S, D = x.shape
    depth = w_in.shape[0]
    conv_width = conv_w.shape[2]
    rw = decay_w0.shape[1]
    L = N_META + S
    Lp = -(-L // WKV_CHUNK) * WKV_CHUNK
    T = B * Lp
    pick = lambda n, cands: next(c for c in cands if n % c == 0)
    tm = pick(T, (640, 512, 256, WKV_CHUNK))
    tm_ffn = pick(T, (1040, 640, 512, 256, WKV_CHUNK))
    tm_proj = pick(T, (1280, 1040, 832, 640, 512, 256, WKV_CHUNK))
    tf = pick(ffn1_w_down.shape[1], (512, 256, 128))
    nb = pick(B, (4, 2, 1))
    o = 3 * conv_width + 3 * rw

    h = jnp.pad(x, ((0, 0), (N_META, Lp - L), (0, 0)))
    meta = jnp.broadcast_to(meta_tokens.astype(x.dtype)[None], (B, N_META, D))
    h = lax.dynamic_update_slice(h, meta, (0, 0, 0)).reshape(T, D)

    vec = lambda a: a.reshape(depth, 1, -1)
    ffn1 = (vec(ffn1_pre_g), ffn1_w_gu.astype(BF16), ffn1_w_down.astype(BF16), vec(0.5 * ffn1_post_g))
    ffn2 = (vec(ffn2_pre_g), ffn2_w_gu.astype(BF16), ffn2_w_down.astype(BF16), vec(0.5 * ffn2_post_g))
    w_main = w_in.astype(BF16)
    w_lo = _lora_cols(w_in[..., o:o + R_DECAY], w_in[..., o + R_DECAY:o + R_DECAY + R_ICLR],
                      w_in[..., o + R_DECAY + R_ICLR:], _first_layer_zero(w_in_vres)).astype(BF16)
    m = 3 * rw
    mu_lo = _lora_cols(mu_rwkv[:, m:m + R_DECAY], mu_rwkv[:, m + R_DECAY:m + R_DECAY + R_ICLR],
                       mu_rwkv[:, m + R_DECAY + R_ICLR:], _first_layer_zero(mu_vres))
    pvec = jnp.stack([decay_w0, iclr_a0, _first_layer_zero(vres_v0), k_k, k_a, r_k.reshape(depth, rw),
                      lnx_g, lnx_b, mu_rwkv[:, :rw], mu_rwkv[:, rw:2 * rw], mu_rwkv[:, 2 * rw:m],
                      _pad_axis(mu_lo, 1, rw), conv_norm_g]
                     + [conv_w[:, tap] for tap in range(CONV_K)], axis=1)
    assert pvec.shape[1:] == (PV_ROWS, rw) and conv_w.shape[1] == CONV_K and conv_width == rw
    wa2 = jnp.concatenate([_pad_axis(decay_w2, 2, 2 * rw),
                           jnp.pad(iclr_a2, ((0, 0), (0, 0), (rw, 0)))], axis=1).astype(BF16)
    g2 = _pad_axis(gate_g2, 1, LO_GD_W).astype(BF16)
    v2 = _pad_axis(_first_layer_zero(vres_v2), 1, LO_VD_W).astype(BF16)
    w_out_b = w_out.astype(BF16)
    mix_pre, mix_post = vec(mix_pre_g), vec(mix_post_g)

    v_first = None
    for i in range(depth):
        h = _ffn(h, *ffn1, i, tm=tm_ffn, tf=tf)
        proj, proj_lo = _proj_in(h, mix_pre, w_main, o, w_lo, i, tm=tm_proj, tn=rw)
        proj3, proj_lo3 = proj.reshape(B, Lp, o), proj_lo.reshape(B, Lp, LO_W)
        if i == 0:
            y_conv, y_rwkv, v_first = _mixer(proj3, proj_lo3, None, pvec, wa2, g2, None, i, batch=B, nb=nb)
        else:
            y_conv, y_rwkv = _mixer(proj3, proj_lo3, v_first, pvec, wa2, g2, v2, i, batch=B, nb=nb)
        h = _proj_out(y_conv.reshape(T, rw), y_rwkv.reshape(T, rw), w_out_b, mix_post, h, i, tm=tm)
        if i < depth - 1:
            h = _ffn(h, *ffn2, i, tm=tm_ffn, tf=tf)
    out = _ffn(h, *ffn2, depth - 1, tm=pick(S, (1024, 512, 256, WKV_CHUNK)), tf=tf, rows=(Lp, N_META, S))
    return out.reshape(B, S, D)
```

```python
import functools
import math

import jax
import jax.numpy as jnp
from jax import lax
from jax.experimental import pallas as pl
from jax.experimental.pallas import tpu as pltpu

F32 = jnp.float32
BF16 = jnp.bfloat16

NORM_EPS = 1e-6
LNX_EPS = 64e-5
N_META = 16
HEAD_DIM = 64
WKV_CHUNK = 64
R_DECAY, R_ICLR, R_GATE, R_VRES = 64, 64, 160, 32
LO_WD, LO_AD, LO_GD, LO_VD = 0, 64, 128, 384
LO_GD_W = 256
LO_VD_W = 128
LO_W = 512
LANES = 128
GROUP_HEADS = LANES // HEAD_DIM
SUBLANES = 8

VMEM_LIMIT_BYTES = 60 * 1024 * 1024


def _cparams(sem):
    return pltpu.CompilerParams(dimension_semantics=sem, vmem_limit_bytes=VMEM_LIMIT_BYTES)


def _rms(x, g):
    ms = jnp.mean(x * x, axis=-1, keepdims=True)
    return x * lax.rsqrt(ms + NORM_EPS) * g


NORM_STRIP = 16


def _rms_strips(n_rows, read, write, g):
    pending = None
    for start in range(0, n_rows + NORM_STRIP, NORM_STRIP):
        if start < n_rows:
            rows = slice(start, start + NORM_STRIP)
            x = read(rows)
            inv_rms = lax.rsqrt(jnp.mean(x * x, axis=-1, keepdims=True) + NORM_EPS)
            upcoming = (rows, x, inv_rms)
        else:
            upcoming = None
        if pending is not None:
            rows_p, x_p, inv_rms_p = pending
            write(rows_p, x_p * inv_rms_p * g)
        pending = upcoming


def _layer_vec(layer, width):
    return pl.BlockSpec((None, 1, width), lambda *_: (layer, 0, 0))


def _ffn_kernel(h_ref, pre_g_ref, wg_ref, wu_ref, wd_ref, half_post_g_ref, o_ref, xn_ref):
    j = pl.program_id(2)

    n_rows = h_ref.shape[0]

    def down_projection():
        xn = xn_ref[...]
        gate = jnp.dot(xn, wg_ref[...].astype(BF16), preferred_element_type=F32)
        up = jnp.dot(xn, wu_ref[...].astype(BF16), preferred_element_type=F32)
        act = (gate * jax.nn.sigmoid(gate) * up).astype(BF16)
        return jnp.dot(act, wd_ref[...].astype(BF16), preferred_element_type=F32)

    @pl.when(j == 0)
    def _():
        def write_xn(rows, y):
            xn_ref[rows, :] = y.astype(BF16)

        _rms_strips(n_rows, lambda rows: h_ref[rows, :], write_xn, pre_g_ref[...])
        o_ref[...] = down_projection()

    @pl.when(j > 0)
    def _():
        o_ref[...] += down_projection()

    @pl.when(j == pl.num_programs(2) - 1)
    def _():
        def write_out(rows, y):
            o_ref[rows, :] = h_ref[rows, :] + y

        _rms_strips(n_rows, lambda rows: o_ref[rows, :], write_out, half_post_g_ref[...])


def _ffn(h, pre_g, w_gu, w_down, half_post_g, layer, *, tm, tf, rows=None):
    T, D = h.shape
    FF = w_down.shape[1]
    nj = FF // tf
    period, start, count = rows or (T, 0, T)
    n_seq, per = T // period, count // tm
    if rows is None:
        h_spec = pl.BlockSpec((tm, D), lambda s, i, j: (i, 0))
    else:
        assert period % SUBLANES == 0 and start % SUBLANES == 0 and tm % SUBLANES == 0
        h_spec = pl.BlockSpec((pl.Element(tm), pl.Element(D)),
                              lambda s, i, j: (pl.multiple_of(s * period + start + i * tm, SUBLANES), 0))
    return pl.pallas_call(
        _ffn_kernel,
        out_shape=jax.ShapeDtypeStruct((n_seq * count, D), F32),
        grid=(n_seq, per, nj),
        in_specs=[
            h_spec,
            _layer_vec(layer, D),
            pl.BlockSpec((None, D, tf), lambda s, i, j: (layer, 0, j)),
            pl.BlockSpec((None, D, tf), lambda s, i, j: (layer, 0, j + nj)),
            pl.BlockSpec((None, tf, D), lambda s, i, j: (layer, j, 0)),
            _layer_vec(layer, D),
        ],
        out_specs=pl.BlockSpec((tm, D), lambda s, i, j: (s * per + i, 0)),
        scratch_shapes=[pltpu.VMEM((tm, D), BF16)],
        compiler_params=_cparams(("parallel", "parallel", "arbitrary")),
        name="ffn",
    )(h, pre_g, w_gu, w_gu, w_down, half_post_g)


def _proj_in_kernel(h_ref, g_ref, w_ref, w_lo_ref, o_ref, o_lo_ref, xn_ref):
    j = pl.program_id(1)
    last = pl.num_programs(1) - 1

    @pl.when(j == 0)
    def _():
        def write_xn(rows, y):
            xn_ref[rows, :] = y.astype(BF16)

        _rms_strips(h_ref.shape[0], lambda rows: h_ref[rows, :], write_xn, g_ref[...])

    @pl.when(j < last)
    def _():
        o_ref[...] = jnp.dot(xn_ref[...], w_ref[...], preferred_element_type=F32)

    @pl.when(j == last)
    def _():
        o_lo_ref[...] = jnp.dot(xn_ref[...], w_lo_ref[...], preferred_element_type=F32)


def _proj_in(h, g, w_main, n_main, w_lo, layer, *, tm, tn):
    T, D = h.shape
    n_lo = w_lo.shape[2]
    n_tiles = n_main // tn
    main_col = lambda j: jnp.minimum(j, n_tiles - 1)
    return pl.pallas_call(
        _proj_in_kernel,
        out_shape=(jax.ShapeDtypeStruct((T, n_main), F32), jax.ShapeDtypeStruct((T, n_lo), F32)),
        grid=(T // tm, n_tiles + 1),
        in_specs=[
            pl.BlockSpec((tm, D), lambda i, j: (i, 0)),
            _layer_vec(layer, D),
            pl.BlockSpec((None, D, tn), lambda i, j: (layer, 0, main_col(j))),
            pl.BlockSpec((None, D, n_lo), lambda i, j: (layer, 0, 0)),
        ],
        out_specs=(pl.BlockSpec((tm, tn), lambda i, j: (i, main_col(j))),
                   pl.BlockSpec((tm, n_lo), lambda i, j: (i, 0))),
        scratch_shapes=[pltpu.VMEM((tm, D), BF16)],
        compiler_params=_cparams(("parallel", "arbitrary")),
        name="proj_in",
    )(h, g, w_main, w_lo)


(PV_W0, PV_A0, PV_V0, PV_KK, PV_KA, PV_RK, PV_LNG, PV_LNB, PV_MUR, PV_MUK, PV_MUV, PV_MULO,
 PV_CONVG, PV_CONVW) = range(14)
CONV_K = 3
PV_ROWS = 16

_DIMS = {"nn": ((1,), (0,)), "nt": ((1,), (1,)), "tn": ((0,), (0,))}


def _split(x):
    hi = x.astype(BF16)
    return hi, (x - hi.astype(F32)).astype(BF16)


def _mm(x, w, form="nn", x_hp=False):
    dot = lambda p, q: lax.dot_general(p, q, (_DIMS[form], ((), ())), preferred_element_type=F32)
    w = w.astype(BF16)
    if not x_hp:
        return dot(x.astype(BF16), w)
    hi, lo = _split(x)
    return dot(hi, w) + dot(lo, w)


def _mixer_kernel(has_vres, *refs):
    if has_vres:
        (gb_ref, gc_ref, u_ref, r_ref, k_ref, v_ref, lo_ref, vf_ref, pv_ref, wa2_ref, g2_ref, v2_ref,
         yc_ref, y_ref, state_ref, carry_ref, conv_carry_ref) = refs
    else:
        (gb_ref, gc_ref, u_ref, r_ref, k_ref, v_ref, lo_ref, pv_ref, wa2_ref, g2_ref,
         yc_ref, y_ref, vf_out_ref, state_ref, carry_ref, conv_carry_ref) = refs
    NB, C, W = r_ref.shape
    R = NB * C
    G = LANES
    n_groups = W // G

    @pl.when(pl.program_id(1) == 0)
    def _():
        state_ref[...] = jnp.zeros_like(state_ref)
        carry_ref[...] = jnp.zeros_like(carry_ref)
        conv_carry_ref[...] = jnp.zeros_like(conv_carry_ref)

    def pv(i):
        return pv_ref[i:i + 1, :]

    trow = lax.broadcasted_iota(jnp.int32, (C, W), 0)
    for b in range(NB):
        uc = gc_ref[b] * u_ref[b]
        base = b * SUBLANES
        last2 = conv_carry_ref[base + 1:base + 2, :]
        last1 = conv_carry_ref[base + 2:base + 3, :]
        back1 = jnp.where(trow == 0, last1, pltpu.roll(uc, 1, 0))
        back2 = jnp.where(trow == 0, last2, jnp.where(trow == 1, last1, pltpu.roll(uc, 2, 0)))
        conv_carry_ref[base + 1:base + CONV_K, :] = uc[C - CONV_K + 1:C, :]
        conv = pv(PV_CONVW) * back2 + pv(PV_CONVW + 1) * back1 + pv(PV_CONVW + 2) * uc
        yc_ref[b] = _rms(gb_ref[b] * conv, pv(PV_CONVG)).astype(BF16)

    def token_shift(x_ref, slot, mu):
        width = x_ref.shape[-1]
        first = lax.broadcasted_iota(jnp.int32, (C, width), 0) == 0
        parts = []
        for b in range(NB):
            x = x_ref[b]
            crow = b * SUBLANES + slot
            prev = jnp.where(first, carry_ref[crow:crow + 1, :width], pltpu.roll(x, 1, 0))
            carry_ref[crow:crow + 1, :width] = x[C - 1:C, :]
            parts.append(x + mu[:, :width] * (prev - x))
        return jnp.concatenate(parts, axis=0)

    r = token_shift(r_ref, 0, pv(PV_MUR))
    k = token_shift(k_ref, 1, pv(PV_MUK))
    v = token_shift(v_ref, 2, pv(PV_MUV))
    lo = token_shift(lo_ref, 3, pv(PV_MULO))

    wa = lo[:, LO_WD:LO_WD + LANES]
    wa = jnp.where(lax.broadcasted_iota(jnp.int32, wa.shape, 1) < LO_AD, jnp.tanh(wa), wa)
    gd = lo[:, LO_GD:LO_GD + LO_GD_W]
    if has_vres:
        vd = lo[:, LO_VD:LO_VD + LO_VD_W]
        mix = jax.nn.sigmoid(pv(PV_V0) + _mm(vd, v2_ref[...]))
        v = v + (vf_ref[...].reshape(R, W) - v) * mix
    else:
        vf_out_ref[...] = v.reshape(NB, C, W)

    za = _mm(wa, wa2_ref[...])
    logw = -math.exp(-0.5) * jax.nn.sigmoid(pv(PV_W0) + za[:, :W])
    a = jax.nn.sigmoid(pv(PV_A0) + za[:, W:])
    g = _mm(jax.nn.sigmoid(gd), g2_ref[...])
    kk_raw = k * pv(PV_KK)
    k2 = k * (1.0 + (a - 1.0) * pv(PV_KA))

    gi = lax.broadcasted_iota(jnp.int32, (G, G), 0)
    gj = lax.broadcasted_iota(jnp.int32, (G, G), 1)
    same_head = (gi // HEAD_DIM) == (gj // HEAD_DIM)
    bd_mask = same_head.astype(F32).astype(BF16)
    ti = lax.broadcasted_iota(jnp.int32, (C, G), 0)
    si = lax.broadcasted_iota(jnp.int32, (C, G), 1) % HEAD_DIM
    incl_p, strict_p = ti >= si, ti > si
    eye_p = (ti == si).astype(F32)
    half_pair = []
    blk = 2
    while blk <= C:
        half_pair.append(((ti // blk) == (si // blk)) & ((ti // (blk // 2)) != (si // (blk // 2))))
        blk *= 2

    def bd(y):
        return jnp.concatenate([y] * GROUP_HEADS, axis=0) * bd_mask

    def group_cols(x):
        return [x[:, i * G:(i + 1) * G] for i in range(n_groups)]

    def head_sum(x, x_hp):
        s = _mm(jnp.concatenate(group_cols(x), axis=0), bd_mask, x_hp=x_hp)
        return jnp.concatenate([s[i * R:(i + 1) * R] for i in range(n_groups)], axis=1)

    kk = kk_raw * jnp.minimum(lax.rsqrt(head_sum(kk_raw * kk_raw, False)), 1e12)
    bonus = head_sum(r * k2 * pv(PV_RK), True) * v

    tc = lax.broadcasted_iota(jnp.int32, (C, C), 0)
    sc = lax.broadcasted_iota(jnp.int32, (C, C), 1)
    tri = (tc >= sc).astype(F32).astype(BF16)
    tri3 = jnp.concatenate([tri, tri, tri], axis=1)
    l_hi = logw.astype(BF16)
    l_rest = logw - l_hi.astype(F32)
    l_mid = l_rest.astype(BF16)
    l_lo = (l_rest - l_mid.astype(F32)).astype(BF16)
    seq_rows = lambda x, b: x[b * C:(b + 1) * C]
    cs_b = [jnp.dot(tri3, jnp.concatenate([seq_rows(l_hi, b), seq_rows(l_mid, b), seq_rows(l_lo, b)],
                                          axis=0), preferred_element_type=F32) for b in range(NB)]
    cs = jnp.concatenate(cs_b, axis=0)
    cs_last = [c[C - 1:C, :] for c in cs_b]
    cs_end = jnp.concatenate([jnp.broadcast_to(c, (C, W)) for c in cs_last], axis=0)
    e_neg = jnp.exp(-cs)
    e_tail = jnp.exp(cs_end - cs)
    p_last = [jnp.exp(c) for c in cs_last]

    b_raw = kk * a
    r_t = (r * jnp.exp(cs)).astype(BF16)
    a_t = (-kk * jnp.exp(cs - logw)).astype(BF16)
    b_t = (b_raw * e_neg).astype(BF16)
    k_t = (k2 * e_neg).astype(BF16)
    b_p = (b_raw * e_tail).astype(BF16)
    k_p = (k2 * e_tail).astype(BF16)
    v_b = v.astype(BF16)

    def units_of(x):
        return [x[b * C:(b + 1) * C, i * G:(i + 1) * G] for b in range(NB) for i in range(n_groups)]

    rg, ag, bg, kg, vg, bpg, kpg = map(units_of, (r_t, a_t, b_t, k_t, v_b, b_p, k_p))
    units = range(NB * n_groups)
    stack = lambda p, q: jnp.concatenate([p, q], axis=0)
    lhs = [stack(ag[i], rg[i]) for i in units]
    sc = [_mm(lhs[i], stack(bd(bg[i]), bd(kg[i])), "nt") for i in units]
    a_ab = [jnp.where(strict_p, sc[i][:C, :G], 0.0) for i in units]
    a_rb = [jnp.where(incl_p, sc[i][C:, :G], 0.0) for i in units]
    a_ak = [jnp.where(strict_p, sc[i][:C, G:], 0.0) for i in units]
    a_rk = [jnp.where(incl_p, sc[i][C:, G:], 0.0) for i in units]

    inv = [eye_p + jnp.where(half_pair[0], a_ab[i], 0.0) for i in units]
    inv_b = [x.astype(BF16) for x in inv]
    for lvl in range(1, len(half_pair)):
        n_inv = [_mm(jnp.where(half_pair[lvl], a_ab[i], 0.0), bd(inv_b[i])) for i in units]
        inv = [inv[i] + _mm(inv_b[i], bd(n_inv[i].astype(BF16))) for i in units]
        inv_b = [x.astype(BF16) for x in inv]

    av_yv = [_mm(stack(a_ak[i], a_rk[i]), bd(vg[i])) for i in units]
    hat = [_mm(inv_b[i], jnp.concatenate([bd(ag[i]), bd(av_yv[i][:C].astype(BF16))], axis=1))
           for i in units]

    s0 = [state_ref[i] for i in units]
    res = [_mm(stack(hat[i][:, :G].astype(BF16), rg[i]), s0[i], "nt") for i in units]
    u = [(res[i][:C] + hat[i][:, G:]).astype(BF16) for i in units]
    upd = [_mm(stack(u[i], vg[i]), stack(bpg[i], kpg[i]), "tn") for i in units]
    for i in units:
        b, gidx = divmod(i, n_groups)
        state_ref[i] = (s0[i] * p_last[b][:, gidx * G:(gidx + 1) * G]
                        + jnp.where(same_head, upd[i], 0.0))
    y_units = [res[i][C:] + _mm(a_rb[i], bd(u[i])) + av_yv[i][C:] for i in units]

    y = jnp.concatenate([jnp.concatenate(y_units[b * n_groups:(b + 1) * n_groups], axis=1)
                         for b in range(NB)], axis=0)
    yc = y - head_sum(y, True) * (1.0 / HEAD_DIM)
    var = head_sum(yc * yc, False) * (1.0 / HEAD_DIM)
    yn = yc * lax.rsqrt(var + LNX_EPS) * pv(PV_LNG) + pv(PV_LNB)
    y_ref[...] = ((yn + bonus) * g).astype(BF16).reshape(NB, C, W)


def _mixer(proj, proj_lo, v_first, pvec, wa2, g2, v2, layer, *, batch, nb):
    Lp = proj.shape[1]
    W = pvec.shape[2]
    C = WKV_CHUNK
    has_vres = v_first is not None
    tok = lambda col: pl.BlockSpec((nb, C, W), lambda bp, c: (bp, c, col))
    par = lambda arr: pl.BlockSpec((None,) + arr.shape[1:], lambda bp, c: (layer, 0, 0))
    in_specs = [tok(col) for col in range(6)] + [pl.BlockSpec((nb, C, LO_W), lambda bp, c: (bp, c, 0))]
    args = [proj] * 6 + [proj_lo]
    if has_vres:
        in_specs.append(tok(0))
        args.append(v_first)
    in_specs += [par(pvec), par(wa2), par(g2)]
    args += [pvec, wa2, g2]
    token_bf16 = jax.ShapeDtypeStruct((batch, Lp, W), BF16)
    out_shape, out_specs = [token_bf16, token_bf16], [tok(0), tok(0)]
    if has_vres:
        in_specs.append(par(v2))
        args.append(v2)
    else:
        out_shape.append(jax.ShapeDtypeStruct((batch, Lp, W), F32))
        out_specs.append(tok(0))
    return pl.pallas_call(
        functools.partial(_mixer_kernel, has_vres),
        out_shape=tuple(out_shape),
        grid=(batch // nb, Lp // C),
        in_specs=in_specs,
        out_specs=tuple(out_specs),
        scratch_shapes=[
            pltpu.VMEM((nb * (W // LANES), LANES, LANES), F32),
            pltpu.VMEM((nb * SUBLANES, W), F32),
            pltpu.VMEM((nb * SUBLANES, W), F32),
        ],
        compiler_params=_cparams(("parallel", "arbitrary")),
        name="mixer",
    )(*args)


def _proj_out_kernel(yc_ref, yr_ref, w_ref, g_ref, h_ref, o_ref):
    half = yc_ref.shape[1]
    y = (jnp.dot(yc_ref[...], w_ref[0:half, :], preferred_element_type=F32)
         + jnp.dot(yr_ref[...], w_ref[half:2 * half, :], preferred_element_type=F32))

    def write_out(rows, y_normed):
        o_ref[rows, :] = h_ref[rows, :] + y_normed

    _rms_strips(y.shape[0], lambda rows: y[rows, :], write_out, g_ref[...])


def _proj_out(y_conv, y_rwkv, w_out, g, h, layer, *, tm):
    T, D = h.shape
    half = y_conv.shape[1]
    return pl.pallas_call(
        _proj_out_kernel,
        out_shape=jax.ShapeDtypeStruct((T, D), F32),
        grid=(T // tm,),
        in_specs=[
            pl.BlockSpec((tm, half), lambda i: (i, 0)),
            pl.BlockSpec((tm, half), lambda i: (i, 0)),
            pl.BlockSpec((None,) + w_out.shape[1:], lambda i: (layer, 0, 0)),
            _layer_vec(layer, D),
            pl.BlockSpec((tm, D), lambda i: (i, 0)),
        ],
        out_specs=pl.BlockSpec((tm, D), lambda i: (i, 0)),
        compiler_params=_cparams(("parallel",)),
        name="proj_out",
    )(y_conv, y_rwkv, w_out, g, h)


def _pad_axis(w, axis, size):
    pad = [(0, 0)] * w.ndim
    pad[axis] = (0, size - w.shape[axis])
    return jnp.pad(w, pad)


def _lora_cols(wd, ad, gd, vd):
    assert LO_AD == wd.shape[-1] and LO_GD == LO_AD + ad.shape[-1]
    return jnp.concatenate([wd, ad, _pad_axis(gd, -1, LO_GD_W), _pad_axis(vd, -1, LO_VD_W)], axis=-1)


def _first_layer_zero(w):
    return jnp.concatenate([jnp.zeros_like(w[:1]), w], axis=0)


def kernel(x, meta_tokens, ffn1_pre_g, ffn1_w_gu, ffn1_w_down, ffn1_post_g, mix_pre_g, w_in, w_in_vres, mu_rwkv, mu_vres, conv_w, conv_norm_g, decay_w0, decay_w2, iclr_a0, iclr_a2, vres_v0, vres_v2, gate_g2, k_k, k_a, r_k, lnx_g, lnx_b, w_out, mix_post_g, ffn2_pre_g, ffn2_w_gu, ffn2_w_down, ffn2_post_g):
    B, S, D = x.shape
    depth = w_in.shape[0]
    conv_width = conv_w.shape[2]
    rw = decay_w0.shape[1]
    L = N_META + S
    Lp = -(-L // WKV_CHUNK) * WKV_CHUNK
    T = B * Lp
    pick = lambda n, cands: next(c for c in cands if n % c == 0)
    tm = pick(T, (640, 512, 256, WKV_CHUNK))
    tm_ffn = pick(T, (1040, 640, 512, 256, WKV_CHUNK))
    tm_proj = pick(T, (1280, 1040, 832, 640, 512, 256, WKV_CHUNK))
    tf = pick(ffn1_w_down.shape[1], (256, 128))
    nb = pick(B, (4, 2, 1))
    o = 3 * conv_width + 3 * rw

    h = jnp.pad(x, ((0, 0), (N_META, Lp - L), (0, 0)))
    meta = jnp.broadcast_to(meta_tokens.astype(x.dtype)[None], (B, N_META, D))
    h = lax.dynamic_update_slice(h, meta, (0, 0, 0)).reshape(T, D)

    vec = lambda a: a.reshape(depth, 1, -1)
    ffn1 = (vec(ffn1_pre_g), ffn1_w_gu, ffn1_w_down, vec(0.5 * ffn1_post_g))
    ffn2 = (vec(ffn2_pre_g), ffn2_w_gu, ffn2_w_down, vec(0.5 * ffn2_post_g))
    w_main = w_in.astype(BF16)
    w_lo = _lora_cols(w_in[..., o:o + R_DECAY], w_in[..., o + R_DECAY:o + R_DECAY + R_ICLR],
                      w_in[..., o + R_DECAY + R_ICLR:], _first_layer_zero(w_in_vres)).astype(BF16)
    m = 3 * rw
    mu_lo = _lora_cols(mu_rwkv[:, m:m + R_DECAY], mu_rwkv[:, m + R_DECAY:m + R_DECAY + R_ICLR],
                       mu_rwkv[:, m + R_DECAY + R_ICLR:], _first_layer_zero(mu_vres))
    pvec = jnp.stack([decay_w0, iclr_a0, _first_layer_zero(vres_v0), k_k, k_a, r_k.reshape(depth, rw),
                      lnx_g, lnx_b, mu_rwkv[:, :rw], mu_rwkv[:, rw:2 * rw], mu_rwkv[:, 2 * rw:m],
                      _pad_axis(mu_lo, 1, rw), conv_norm_g]
                     + [conv_w[:, tap] for tap in range(CONV_K)], axis=1)
    assert pvec.shape[1:] == (PV_ROWS, rw) and conv_w.shape[1] == CONV_K and conv_width == rw
    wa2 = jnp.concatenate([_pad_axis(decay_w2, 2, 2 * rw),
                           jnp.pad(iclr_a2, ((0, 0), (0, 0), (rw, 0)))], axis=1).astype(BF16)
    g2 = _pad_axis(gate_g2, 1, LO_GD_W).astype(BF16)
    v2 = _pad_axis(_first_layer_zero(vres_v2), 1, LO_VD_W).astype(BF16)
    w_out_b = w_out.astype(BF16)
    mix_pre, mix_post = vec(mix_pre_g), vec(mix_post_g)

    v_first = None
    for i in range(depth):
        h = _ffn(h, *ffn1, i, tm=tm_ffn, tf=tf)
        proj, proj_lo = _proj_in(h, mix_pre, w_main, o, w_lo, i, tm=tm_proj, tn=rw)
        proj3, proj_lo3 = proj.reshape(B, Lp, o), proj_lo.reshape(B, Lp, LO_W)
        if i == 0:
            y_conv, y_rwkv, v_first = _mixer(proj3, proj_lo3, None, pvec, wa2, g2, None, i, batch=B, nb=nb)
        else:
            y_conv, y_rwkv = _mixer(proj3, proj_lo3, v_first, pvec, wa2, g2, v2, i, batch=B, nb=nb)
        h = _proj_out(y_conv.reshape(T, rw), y_rwkv.reshape(T, rw), w_out_b, mix_post, h, i, tm=tm)
        if i < depth - 1:
            h = _ffn(h, *ffn2, i, tm=tm_ffn, tf=tf)
    out = _ffn(h, *ffn2, depth - 1, tm=pick(S, (1024, 512, 256, WKV_CHUNK)), tf=tf, rows=(Lp, N_META, S))
    return out.reshape(B, S, D)
```

```python
import functools
import math

import jax
import jax.numpy as jnp
from jax import lax
from jax.experimental import pallas as pl
from jax.experimental.pallas import tpu as pltpu

F32 = jnp.float32
BF16 = jnp.bfloat16

NORM_EPS = 1e-6
LNX_EPS = 64e-5
N_META = 16
HEAD_DIM = 64
WKV_CHUNK = 64
R_DECAY, R_ICLR, R_GATE, R_VRES = 64, 64, 160, 32
LO_WD, LO_AD, LO_GD, LO_VD = 0, 64, 128, 384
LO_GD_W = 256
LO_VD_W = 128
LO_W = 512
LANES = 128
GROUP_HEADS = LANES // HEAD_DIM
SUBLANES = 8

VMEM_LIMIT_BYTES = 60 * 1024 * 1024


def _cparams(sem):
    return pltpu.CompilerParams(dimension_semantics=sem, vmem_limit_bytes=VMEM_LIMIT_BYTES)


def _rms(x, g):
    ms = jnp.mean(x * x, axis=-1, keepdims=True)
    return x * lax.rsqrt(ms + NORM_EPS) * g


NORM_STRIP = 16


def _rms_strips(n_rows, read, write, g):
    pending = None
    for start in range(0, n_rows + NORM_STRIP, NORM_STRIP):
        if start < n_rows:
            rows = slice(start, start + NORM_STRIP)
            x = read(rows)
            inv_rms = lax.rsqrt(jnp.mean(x * x, axis=-1, keepdims=True) + NORM_EPS)
            upcoming = (rows, x, inv_rms)
        else:
            upcoming = None
        if pending is not None:
            rows_p, x_p, inv_rms_p = pending
            write(rows_p, x_p * inv_rms_p * g)
        pending = upcoming


def _layer_vec(layer, width):
    return pl.BlockSpec((None, 1, width), lambda *_: (layer, 0, 0))


def _ffn_kernel(h_ref, pre_g_ref, wg_ref, wu_ref, wd_ref, half_post_g_ref, o_ref, xn_ref):
    j = pl.program_id(2)

    n_rows = h_ref.shape[0]

    def down_projection():
        xn = xn_ref[...]
        gate = jnp.dot(xn, wg_ref[...], preferred_element_type=F32)
        up = jnp.dot(xn, wu_ref[...], preferred_element_type=F32)
        act = (gate * jax.nn.sigmoid(gate) * up).astype(BF16)
        return jnp.dot(act, wd_ref[...], preferred_element_type=F32)

    @pl.when(j == 0)
    def _():
        def write_xn(rows, y):
            xn_ref[rows, :] = y.astype(BF16)

        _rms_strips(n_rows, lambda rows: h_ref[rows, :], write_xn, pre_g_ref[...])
        o_ref[...] = down_projection()

    @pl.when(j > 0)
    def _():
        o_ref[...] += down_projection()

    @pl.when(j == pl.num_programs(2) - 1)
    def _():
        def write_out(rows, y):
            o_ref[rows, :] = h_ref[rows, :] + y

        _rms_strips(n_rows, lambda rows: o_ref[rows, :], write_out, half_post_g_ref[...])


def _ffn(h, pre_g, w_gu, w_down, half_post_g, layer, *, tm, tf, rows=None):
    T, D = h.shape
    FF = w_down.shape[1]
    nj = FF // tf
    period, start, count = rows or (T, 0, T)
    n_seq, per = T // period, count // tm
    if rows is None:
        h_spec = pl.BlockSpec((tm, D), lambda s, i, j: (i, 0))
    else:
        assert period % SUBLANES == 0 and start % SUBLANES == 0 and tm % SUBLANES == 0
        h_spec = pl.BlockSpec((pl.Element(tm), pl.Element(D)),
                              lambda s, i, j: (pl.multiple_of(s * period + start + i * tm, SUBLANES), 0))
    return pl.pallas_call(
        _ffn_kernel,
        out_shape=jax.ShapeDtypeStruct((n_seq * count, D), F32),
        grid=(n_seq, per, nj),
        in_specs=[
            h_spec,
            _layer_vec(layer, D),
            pl.BlockSpec((None, D, tf), lambda s, i, j: (layer, 0, j)),
            pl.BlockSpec((None, D, tf), lambda s, i, j: (layer, 0, j + nj)),
            pl.BlockSpec((None, tf, D), lambda s, i, j: (layer, j, 0)),
            _layer_vec(layer, D),
        ],
        out_specs=pl.BlockSpec((tm, D), lambda s, i, j: (s * per + i, 0)),
        scratch_shapes=[pltpu.VMEM((tm, D), BF16)],
        compiler_params=_cparams(("parallel", "parallel", "arbitrary")),
        name="ffn",
    )(h, pre_g, w_gu, w_gu, w_down, half_post_g)


def _proj_in_kernel(h_ref, g_ref, w_ref, w_lo_ref, o_ref, o_lo_ref, xn_ref):
    j = pl.program_id(1)
    last = pl.num_programs(1) - 1

    @pl.when(j == 0)
    def _():
        def write_xn(rows, y):
            xn_ref[rows, :] = y.astype(BF16)

        _rms_strips(h_ref.shape[0], lambda rows: h_ref[rows, :], write_xn, g_ref[...])

    @pl.when(j < last)
    def _():
        o_ref[...] = jnp.dot(xn_ref[...], w_ref[...], preferred_element_type=F32)

    @pl.when(j == last)
    def _():
        o_lo_ref[...] = jnp.dot(xn_ref[...], w_lo_ref[...], preferred_element_type=F32)


def _proj_in(h, g, w_main, n_main, w_lo, layer, *, tm, tn):
    T, D = h.shape
    n_lo = w_lo.shape[2]
    n_tiles = n_main // tn
    main_col = lambda j: jnp.minimum(j, n_tiles - 1)
    return pl.pallas_call(
        _proj_in_kernel,
        out_shape=(jax.ShapeDtypeStruct((T, n_main), F32), jax.ShapeDtypeStruct((T, n_lo), F32)),
        grid=(T // tm, n_tiles + 1),
        in_specs=[
            pl.BlockSpec((tm, D), lambda i, j: (i, 0)),
            _layer_vec(layer, D),
            pl.BlockSpec((None, D, tn), lambda i, j: (layer, 0, main_col(j))),
            pl.BlockSpec((None, D, n_lo), lambda i, j: (layer, 0, 0)),
        ],
        out_specs=(pl.BlockSpec((tm, tn), lambda i, j: (i, main_col(j))),
                   pl.BlockSpec((tm, n_lo), lambda i, j: (i, 0))),
        scratch_shapes=[pltpu.VMEM((tm, D), BF16)],
        compiler_params=_cparams(("parallel", "arbitrary")),
        name="proj_in",
    )(h, g, w_main, w_lo)


(PV_W0, PV_A0, PV_V0, PV_KK, PV_KA, PV_RK, PV_LNG, PV_LNB, PV_MUR, PV_MUK, PV_MUV, PV_MULO,
 PV_CONVG, PV_CONVW) = range(14)
CONV_K = 3
PV_ROWS = 16

_DIMS = {"nn": ((1,), (0,)), "nt": ((1,), (1,)), "tn": ((0,), (0,))}


def _split(x):
    hi = x.astype(BF16)
    return hi, (x - hi.astype(F32)).astype(BF16)


def _mm(x, w, form="nn", x_hp=False):
    dot = lambda p, q: lax.dot_general(p, q, (_DIMS[form], ((), ())), preferred_element_type=F32)
    w = w.astype(BF16)
    if not x_hp:
        return dot(x.astype(BF16), w)
    hi, lo = _split(x)
    return dot(hi, w) + dot(lo, w)


def _mixer_kernel(has_vres, *refs):
    if has_vres:
        (gb_ref, gc_ref, u_ref, r_ref, k_ref, v_ref, lo_ref, vf_ref, pv_ref, wa2_ref, g2_ref, v2_ref,
         y_ref, state_ref, carry_ref, conv_carry_ref) = refs
    else:
        (gb_ref, gc_ref, u_ref, r_ref, k_ref, v_ref, lo_ref, pv_ref, wa2_ref, g2_ref,
         y_ref, vf_out_ref, state_ref, carry_ref, conv_carry_ref) = refs
    NB, C, W = r_ref.shape
    R = NB * C
    G = LANES
    n_groups = W // G

    @pl.when(pl.program_id(1) == 0)
    def _():
        state_ref[...] = jnp.zeros_like(state_ref)
        carry_ref[...] = jnp.zeros_like(carry_ref)
        conv_carry_ref[...] = jnp.zeros_like(conv_carry_ref)

    def pv(i):
        return pv_ref[i:i + 1, :]

    trow = lax.broadcasted_iota(jnp.int32, (C, W), 0)
    for b in range(NB):
        uc = gc_ref[b] * u_ref[b]
        base = b * SUBLANES
        last2 = conv_carry_ref[base + 1:base + 2, :]
        last1 = conv_carry_ref[base + 2:base + 3, :]
        back1 = jnp.where(trow == 0, last1, pltpu.roll(uc, 1, 0))
        back2 = jnp.where(trow == 0, last2, jnp.where(trow == 1, last1, pltpu.roll(uc, 2, 0)))
        conv_carry_ref[base + 1:base + CONV_K, :] = uc[C - CONV_K + 1:C, :]
        conv = pv(PV_CONVW) * back2 + pv(PV_CONVW + 1) * back1 + pv(PV_CONVW + 2) * uc
        y_ref[b, :, 0:W] = _rms(gb_ref[b] * conv, pv(PV_CONVG)).astype(BF16)

    def token_shift(x_ref, slot, mu):
        width = x_ref.shape[-1]
        first = lax.broadcasted_iota(jnp.int32, (C, width), 0) == 0
        parts = []
        for b in range(NB):
            x = x_ref[b]
            crow = b * SUBLANES + slot
            prev = jnp.where(first, carry_ref[crow:crow + 1, :width], pltpu.roll(x, 1, 0))
            carry_ref[crow:crow + 1, :width] = x[C - 1:C, :]
            parts.append(x + mu[:, :width] * (prev - x))
        return jnp.concatenate(parts, axis=0)

    r = token_shift(r_ref, 0, pv(PV_MUR))
    k = token_shift(k_ref, 1, pv(PV_MUK))
    v = token_shift(v_ref, 2, pv(PV_MUV))
    lo = token_shift(lo_ref, 3, pv(PV_MULO))

    wa = lo[:, LO_WD:LO_WD + LANES]
    wa = jnp.where(lax.broadcasted_iota(jnp.int32, wa.shape, 1) < LO_AD, jnp.tanh(wa), wa)
    gd = lo[:, LO_GD:LO_GD + LO_GD_W]
    if has_vres:
        vd = lo[:, LO_VD:LO_VD + LO_VD_W]
        mix = jax.nn.sigmoid(pv(PV_V0) + _mm(vd, v2_ref[...]))
        v = v + (vf_ref[...].reshape(R, W) - v) * mix
    else:
        vf_out_ref[...] = v.reshape(NB, C, W)

    za = _mm(wa, wa2_ref[...])
    logw = -math.exp(-0.5) * jax.nn.sigmoid(pv(PV_W0) + za[:, :W])
    a = jax.nn.sigmoid(pv(PV_A0) + za[:, W:])
    g = _mm(jax.nn.sigmoid(gd), g2_ref[...])
    kk_raw = k * pv(PV_KK)
    k2 = k * (1.0 + (a - 1.0) * pv(PV_KA))

    gi = lax.broadcasted_iota(jnp.int32, (G, G), 0)
    gj = lax.broadcasted_iota(jnp.int32, (G, G), 1)
    same_head = (gi // HEAD_DIM) == (gj // HEAD_DIM)
    bd_mask = same_head.astype(F32).astype(BF16)
    ti = lax.broadcasted_iota(jnp.int32, (C, G), 0)
    si = lax.broadcasted_iota(jnp.int32, (C, G), 1) % HEAD_DIM
    incl_p, strict_p = ti >= si, ti > si
    eye_p = (ti == si).astype(F32)
    half_pair = []
    blk = 2
    while blk <= C:
        half_pair.append(((ti // blk) == (si // blk)) & ((ti // (blk // 2)) != (si // (blk // 2))))
        blk *= 2

    def bd(y):
        return jnp.concatenate([y] * GROUP_HEADS, axis=0) * bd_mask

    def group_cols(x):
        return [x[:, i * G:(i + 1) * G] for i in range(n_groups)]

    def head_sum(x, x_hp):
        s = _mm(jnp.concatenate(group_cols(x), axis=0), bd_mask, x_hp=x_hp)
        return jnp.concatenate([s[i * R:(i + 1) * R] for i in range(n_groups)], axis=1)

    kk = kk_raw * jnp.minimum(lax.rsqrt(head_sum(kk_raw * kk_raw, False)), 1e12)
    bonus = head_sum(r * k2 * pv(PV_RK), True) * v

    tc = lax.broadcasted_iota(jnp.int32, (C, C), 0)
    sc = lax.broadcasted_iota(jnp.int32, (C, C), 1)
    tri = (tc >= sc).astype(F32).astype(BF16)
    tri3 = jnp.concatenate([tri, tri, tri], axis=1)
    l_hi = logw.astype(BF16)
    l_rest = logw - l_hi.astype(F32)
    l_mid = l_rest.astype(BF16)
    l_lo = (l_rest - l_mid.astype(F32)).astype(BF16)
    seq_rows = lambda x, b: x[b * C:(b + 1) * C]
    cs_b = [jnp.dot(tri3, jnp.concatenate([seq_rows(l_hi, b), seq_rows(l_mid, b), seq_rows(l_lo, b)],
                                          axis=0), preferred_element_type=F32) for b in range(NB)]
    cs = jnp.concatenate(cs_b, axis=0)
    cs_last = [c[C - 1:C, :] for c in cs_b]
    cs_end = jnp.concatenate([jnp.broadcast_to(c, (C, W)) for c in cs_last], axis=0)
    e_neg = jnp.exp(-cs)
    e_tail = jnp.exp(cs_end - cs)
    p_last = [jnp.exp(c) for c in cs_last]

    b_raw = kk * a
    r_t = (r * jnp.exp(cs)).astype(BF16)
    a_t = (-kk * jnp.exp(cs - logw)).astype(BF16)
    b_t = (b_raw * e_neg).astype(BF16)
    k_t = (k2 * e_neg).astype(BF16)
    b_p = (b_raw * e_tail).astype(BF16)
    k_p = (k2 * e_tail).astype(BF16)
    v_b = v.astype(BF16)

    def units_of(x):
        return [x[b * C:(b + 1) * C, i * G:(i + 1) * G] for b in range(NB) for i in range(n_groups)]

    rg, ag, bg, kg, vg, bpg, kpg = map(units_of, (r_t, a_t, b_t, k_t, v_b, b_p, k_p))
    units = range(NB * n_groups)
    stack = lambda p, q: jnp.concatenate([p, q], axis=0)
    lhs = [stack(ag[i], rg[i]) for i in units]
    sc = [_mm(lhs[i], stack(bd(bg[i]), bd(kg[i])), "nt") for i in units]
    a_ab = [jnp.where(strict_p, sc[i][:C, :G], 0.0) for i in units]
    a_rb = [jnp.where(incl_p, sc[i][C:, :G], 0.0) for i in units]
    a_ak = [jnp.where(strict_p, sc[i][:C, G:], 0.0) for i in units]
    a_rk = [jnp.where(incl_p, sc[i][C:, G:], 0.0) for i in units]

    inv = [eye_p + jnp.where(half_pair[0], a_ab[i], 0.0) for i in units]
    inv_b = [x.astype(BF16) for x in inv]
    for lvl in range(1, len(half_pair)):
        n_inv = [_mm(jnp.where(half_pair[lvl], a_ab[i], 0.0), bd(inv_b[i])) for i in units]
        inv = [inv[i] + _mm(inv_b[i], bd(n_inv[i].astype(BF16))) for i in units]
        inv_b = [x.astype(BF16) for x in inv]

    av_yv = [_mm(stack(a_ak[i], a_rk[i]), bd(vg[i])) for i in units]
    hat = [_mm(inv_b[i], jnp.concatenate([bd(ag[i]), bd(av_yv[i][:C].astype(BF16))], axis=1))
           for i in units]

    s0 = [state_ref[i] for i in units]
    res = [_mm(stack(hat[i][:, :G].astype(BF16), rg[i]), s0[i], "nt") for i in units]
    u = [(res[i][:C] + hat[i][:, G:]).astype(BF16) for i in units]
    upd = [_mm(stack(u[i], vg[i]), stack(bpg[i], kpg[i]), "tn") for i in units]
    for i in units:
        b, gidx = divmod(i, n_groups)
        state_ref[i] = (s0[i] * p_last[b][:, gidx * G:(gidx + 1) * G]
                        + jnp.where(same_head, upd[i], 0.0))
    y_units = [res[i][C:] + _mm(a_rb[i], bd(u[i])) + av_yv[i][C:] for i in units]

    y = jnp.concatenate([jnp.concatenate(y_units[b * n_groups:(b + 1) * n_groups], axis=1)
                         for b in range(NB)], axis=0)
    yc = y - head_sum(y, True) * (1.0 / HEAD_DIM)
    var = head_sum(yc * yc, False) * (1.0 / HEAD_DIM)
    yn = yc * lax.rsqrt(var + LNX_EPS) * pv(PV_LNG) + pv(PV_LNB)
    y_ref[:, :, W:2 * W] = ((yn + bonus) * g).astype(BF16).reshape(NB, C, W)


def _mixer(proj, proj_lo, v_first, pvec, wa2, g2, v2, layer, *, batch, nb):
    Lp = proj.shape[1]
    W = pvec.shape[2]
    C = WKV_CHUNK
    has_vres = v_first is not None
    tok = lambda col: pl.BlockSpec((nb, C, W), lambda bp, c: (bp, c, col))
    par = lambda arr: pl.BlockSpec((None,) + arr.shape[1:], lambda bp, c: (layer, 0, 0))
    in_specs = [tok(col) for col in range(6)] + [pl.BlockSpec((nb, C, LO_W), lambda bp, c: (bp, c, 0))]
    args = [proj] * 6 + [proj_lo]
    if has_vres:
        in_specs.append(tok(0))
        args.append(v_first)
    in_specs += [par(pvec), par(wa2), par(g2)]
    args += [pvec, wa2, g2]
    out_shape = [jax.ShapeDtypeStruct((batch, Lp, 2 * W), BF16)]
    out_specs = [pl.BlockSpec((nb, C, 2 * W), lambda bp, c: (bp, c, 0))]
    if has_vres:
        in_specs.append(par(v2))
        args.append(v2)
    else:
        out_shape.append(jax.ShapeDtypeStruct((batch, Lp, W), F32))
        out_specs.append(tok(0))
    return pl.pallas_call(
        functools.partial(_mixer_kernel, has_vres),
        out_shape=tuple(out_shape),
        grid=(batch // nb, Lp // C),
        in_specs=in_specs,
        out_specs=tuple(out_specs),
        scratch_shapes=[
            pltpu.VMEM((nb * (W // LANES), LANES, LANES), F32),
            pltpu.VMEM((nb * SUBLANES, W), F32),
            pltpu.VMEM((nb * SUBLANES, W), F32),
        ],
        compiler_params=_cparams(("parallel", "arbitrary")),
        name="mixer",
    )(*args)


def _proj_out_kernel(y_ref, w_ref, g_ref, h_ref, o_ref):
    y = jnp.dot(y_ref[...], w_ref[...], preferred_element_type=F32)

    def write_out(rows, y_normed):
        o_ref[rows, :] = h_ref[rows, :] + y_normed

    _rms_strips(y.shape[0], lambda rows: y[rows, :], write_out, g_ref[...])


def _proj_out(y_mix, w_out, g, h, layer, *, tm):
    T, D = h.shape
    return pl.pallas_call(
        _proj_out_kernel,
        out_shape=jax.ShapeDtypeStruct((T, D), F32),
        grid=(T // tm,),
        in_specs=[
            pl.BlockSpec((tm, y_mix.shape[1]), lambda i: (i, 0)),
            pl.BlockSpec((None,) + w_out.shape[1:], lambda i: (layer, 0, 0)),
            _layer_vec(layer, D),
            pl.BlockSpec((tm, D), lambda i: (i, 0)),
        ],
        out_specs=pl.BlockSpec((tm, D), lambda i: (i, 0)),
        compiler_params=_cparams(("parallel",)),
        name="proj_out",
    )(y_mix, w_out, g, h)


def _pad_axis(w, axis, size):
    pad = [(0, 0)] * w.ndim
    pad[axis] = (0, size - w.shape[axis])
    return jnp.pad(w, pad)


def _lora_cols(wd, ad, gd, vd):
    assert LO_AD == wd.shape[-1] and LO_GD == LO_AD + ad.shape[-1]
    return jnp.concatenate([wd, ad, _pad_axis(gd, -1, LO_GD_W), _pad_axis(vd, -1, LO_VD_W)], axis=-1)


def _first_layer_zero(w):
    return jnp.concatenate([jnp.zeros_like(w[:1]), w], axis=0)


def kernel(x, meta_tokens, ffn1_pre_g, ffn1_w_gu, ffn1_w_down, ffn1_post_g, mix_pre_g, w_in, w_in_vres, mu_rwkv, mu_vres, conv_w, conv_norm_g, decay_w0, decay_w2, iclr_a0, iclr_a2, vres_v0, vres_v2, gate_g2, k_k, k_a, r_k, lnx_g, lnx_b, w_out, mix_post_g, ffn2_pre_g, ffn2_w_gu, ffn2_w_down, ffn2_post_g):
    B, S, D = x.shape
    depth = w_in.shape[0]
    conv_width = conv_w.shape[2]
    rw = decay_w0.shape[1]
    L = N_META + S
    Lp = -(-L // WKV_CHUNK) * WKV_CHUNK
    T = B * Lp
    pick = lambda n, cands: next(c for c in cands if n % c == 0)
    tm = pick(T, (640, 512, 256, WKV_CHUNK))
    tm_ffn = pick(T, (1040, 640, 512, 256, WKV_CHUNK))
    tm_proj = pick(T, (1280, 1040, 832, 640, 512, 256, WKV_CHUNK))
    tf = pick(ffn1_w_down.shape[1], (512, 256, 128))
    nb = pick(B, (4, 2, 1))
    o = 3 * conv_width + 3 * rw

    h = jnp.pad(x, ((0, 0), (N_META, Lp - L), (0, 0)))
    meta = jnp.broadcast_to(meta_tokens.astype(x.dtype)[None], (B, N_META, D))
    h = lax.dynamic_update_slice(h, meta, (0, 0, 0)).reshape(T, D)

    vec = lambda a: a.reshape(depth, 1, -1)
    ffn1 = (vec(ffn1_pre_g), ffn1_w_gu.astype(BF16), ffn1_w_down.astype(BF16), vec(0.5 * ffn1_post_g))
    ffn2 = (vec(ffn2_pre_g), ffn2_w_gu.astype(BF16), ffn2_w_down.astype(BF16), vec(0.5 * ffn2_post_g))
    w_main = w_in.astype(BF16)
    w_lo = _lora_cols(w_in[..., o:o + R_DECAY], w_in[..., o + R_DECAY:o + R_DECAY + R_ICLR],
                      w_in[..., o + R_DECAY + R_ICLR:], _first_layer_zero(w_in_vres)).astype(BF16)
    m = 3 * rw
    mu_lo = _lora_cols(mu_rwkv[:, m:m + R_DECAY], mu_rwkv[:, m + R_DECAY:m + R_DECAY + R_ICLR],
                       mu_rwkv[:, m + R_DECAY + R_ICLR:], _first_layer_zero(mu_vres))
    pvec = jnp.stack([decay_w0, iclr_a0, _first_layer_zero(vres_v0), k_k, k_a, r_k.reshape(depth, rw),
                      lnx_g, lnx_b, mu_rwkv[:, :rw], mu_rwkv[:, rw:2 * rw], mu_rwkv[:, 2 * rw:m],
                      _pad_axis(mu_lo, 1, rw), conv_norm_g]
                     + [conv_w[:, tap] for tap in range(CONV_K)], axis=1)
    assert pvec.shape[1:] == (PV_ROWS, rw) and conv_w.shape[1] == CONV_K and conv_width == rw
    wa2 = jnp.concatenate([_pad_axis(decay_w2, 2, 2 * rw),
                           jnp.pad(iclr_a2, ((0, 0), (0, 0), (rw, 0)))], axis=1).astype(BF16)
    g2 = _pad_axis(gate_g2, 1, LO_GD_W).astype(BF16)
    v2 = _pad_axis(_first_layer_zero(vres_v2), 1, LO_VD_W).astype(BF16)
    w_out_b = w_out.astype(BF16)
    mix_pre, mix_post = vec(mix_pre_g), vec(mix_post_g)

    v_first = None
    for i in range(depth):
        h = _ffn(h, *ffn1, i, tm=tm_ffn, tf=tf)
        proj, proj_lo = _proj_in(h, mix_pre, w_main, o, w_lo, i, tm=tm_proj, tn=rw)
        proj3, proj_lo3 = proj.reshape(B, Lp, o), proj_lo.reshape(B, Lp, LO_W)
        if i == 0:
            y_mix, v_first = _mixer(proj3, proj_lo3, None, pvec, wa2, g2, None, i, batch=B, nb=nb)
        else:
            (y_mix,) = _mixer(proj3, proj_lo3, v_first, pvec, wa2, g2, v2, i, batch=B, nb=nb)
        h = _proj_out(y_mix.reshape(T, 2 * rw), w_out_b, mix_post, h, i, tm=tm)
        if i < depth - 1:
            h = _ffn(h, *ffn2, i, tm=tm_ffn, tf=tf)
    out = _ffn(h, *ffn2, depth - 1, tm=pick(S, (1024, 512, 256, WKV_CHUNK)), tf=tf, rows=(Lp, N_META, S))
    return out.reshape(B, S, D)
```

```python
import functools
import math

import jax
import jax.numpy as jnp
from jax import lax
from jax.experimental import pallas as pl
from jax.experimental.pallas import tpu as pltpu

F32 = jnp.float32
BF16 = jnp.bfloat16

NORM_EPS = 1e-6
LNX_EPS = 64e-5
N_META = 16
HEAD_DIM = 64
WKV_CHUNK = 64
R_DECAY, R_ICLR, R_GATE, R_VRES = 64, 64, 160, 32
LO_WD, LO_AD, LO_GD, LO_VD = 0, 64, 128, 384
LO_GD_W = 256
LO_VD_W = 128
LO_W = 512
LANES = 128
GROUP_HEADS = LANES // HEAD_DIM
SUBLANES = 8

VMEM_LIMIT_BYTES = 60 * 1024 * 1024


def _cparams(sem, fuse_inputs=None):
    return pltpu.CompilerParams(dimension_semantics=sem, vmem_limit_bytes=VMEM_LIMIT_BYTES,
                                allow_input_fusion=fuse_inputs)


def _rms(x, g):
    ms = jnp.mean(x * x, axis=-1, keepdims=True)
    return x * lax.rsqrt(ms + NORM_EPS) * g


NORM_STRIP = 16


def _rms_strips(n_rows, read, write, g):
    pending = None
    for start in range(0, n_rows + NORM_STRIP, NORM_STRIP):
        if start < n_rows:
            rows = slice(start, start + NORM_STRIP)
            x = read(rows)
            inv_rms = lax.rsqrt(jnp.mean(x * x, axis=-1, keepdims=True) + NORM_EPS)
            upcoming = (rows, x, inv_rms)
        else:
            upcoming = None
        if pending is not None:
            rows_p, x_p, inv_rms_p = pending
            write(rows_p, x_p * inv_rms_p * g)
        pending = upcoming


def _layer_vec(layer, width):
    return pl.BlockSpec((None, 1, width), lambda *_: (layer, 0, 0))


def _ffn_kernel(h_ref, pre_g_ref, wg_ref, wu_ref, wd_ref, half_post_g_ref, o_ref, xn_ref):
    j = pl.program_id(2)

    n_rows = h_ref.shape[0]

    def down_projection():
        xn = xn_ref[...]
        gate = jnp.dot(xn, wg_ref[...], preferred_element_type=F32)
        up = jnp.dot(xn, wu_ref[...], preferred_element_type=F32)
        act = (gate * jax.nn.sigmoid(gate) * up).astype(BF16)
        return jnp.dot(act, wd_ref[...], preferred_element_type=F32)

    @pl.when(j == 0)
    def _():
        def write_xn(rows, y):
            xn_ref[rows, :] = y.astype(BF16)

        _rms_strips(n_rows, lambda rows: h_ref[rows, :], write_xn, pre_g_ref[...])
        o_ref[...] = down_projection()

    @pl.when(j > 0)
    def _():
        o_ref[...] += down_projection()

    @pl.when(j == pl.num_programs(2) - 1)
    def _():
        def write_out(rows, y):
            o_ref[rows, :] = h_ref[rows, :] + y

        _rms_strips(n_rows, lambda rows: o_ref[rows, :], write_out, half_post_g_ref[...])


def _ffn(h, pre_g, w_gu, w_down, half_post_g, layer, *, tm, tf, rows=None):
    T, D = h.shape
    FF = w_down.shape[1]
    nj = FF // tf
    period, start, count = rows or (T, 0, T)
    n_seq, per = T // period, count // tm
    if rows is None:
        h_spec = pl.BlockSpec((tm, D), lambda s, i, j: (i, 0))
    else:
        assert period % SUBLANES == 0 and start % SUBLANES == 0 and tm % SUBLANES == 0
        h_spec = pl.BlockSpec((pl.Element(tm), pl.Element(D)),
                              lambda s, i, j: (pl.multiple_of(s * period + start + i * tm, SUBLANES), 0))
    return pl.pallas_call(
        _ffn_kernel,
        out_shape=jax.ShapeDtypeStruct((n_seq * count, D), F32),
        grid=(n_seq, per, nj),
        in_specs=[
            h_spec,
            _layer_vec(layer, D),
            pl.BlockSpec((None, D, tf), lambda s, i, j: (layer, 0, j)),
            pl.BlockSpec((None, D, tf), lambda s, i, j: (layer, 0, j + nj)),
            pl.BlockSpec((None, tf, D), lambda s, i, j: (layer, j, 0)),
            _layer_vec(layer, D),
        ],
        out_specs=pl.BlockSpec((tm, D), lambda s, i, j: (s * per + i, 0)),
        scratch_shapes=[pltpu.VMEM((tm, D), BF16)],
        compiler_params=_cparams(("parallel", "parallel", "arbitrary"),
                                 fuse_inputs=[False, False, True, True, True, False]),
        name="ffn",
    )(h, pre_g, w_gu, w_gu, w_down, half_post_g)


def _proj_in_kernel(h_ref, g_ref, w_ref, w_lo_ref, o_ref, o_lo_ref, xn_ref):
    j = pl.program_id(1)
    last = pl.num_programs(1) - 1

    @pl.when(j == 0)
    def _():
        def write_xn(rows, y):
            xn_ref[rows, :] = y.astype(BF16)

        _rms_strips(h_ref.shape[0], lambda rows: h_ref[rows, :], write_xn, g_ref[...])

    @pl.when(j < last)
    def _():
        o_ref[...] = jnp.dot(xn_ref[...], w_ref[...], preferred_element_type=F32)

    @pl.when(j == last)
    def _():
        o_lo_ref[...] = jnp.dot(xn_ref[...], w_lo_ref[...], preferred_element_type=F32)


def _proj_in(h, g, w_main, n_main, w_lo, layer, *, tm, tn):
    T, D = h.shape
    n_lo = w_lo.shape[2]
    n_tiles = n_main // tn
    main_col = lambda j: jnp.minimum(j, n_tiles - 1)
    return pl.pallas_call(
        _proj_in_kernel,
        out_shape=(jax.ShapeDtypeStruct((T, n_main), F32), jax.ShapeDtypeStruct((T, n_lo), F32)),
        grid=(T // tm, n_tiles + 1),
        in_specs=[
            pl.BlockSpec((tm, D), lambda i, j: (i, 0)),
            _layer_vec(layer, D),
            pl.BlockSpec((None, D, tn), lambda i, j: (layer, 0, main_col(j))),
            pl.BlockSpec((None, D, n_lo), lambda i, j: (layer, 0, 0)),
        ],
        out_specs=(pl.BlockSpec((tm, tn), lambda i, j: (i, main_col(j))),
                   pl.BlockSpec((tm, n_lo), lambda i, j: (i, 0))),
        scratch_shapes=[pltpu.VMEM((tm, D), BF16)],
        compiler_params=_cparams(("parallel", "arbitrary")),
        name="proj_in",
    )(h, g, w_main, w_lo)


(PV_W0, PV_A0, PV_V0, PV_KK, PV_KA, PV_RK, PV_LNG, PV_LNB, PV_MUR, PV_MUK, PV_MUV, PV_MULO,
 PV_CONVG, PV_CONVW) = range(14)
CONV_K = 3
PV_ROWS = 16

_DIMS = {"nn": ((1,), (0,)), "nt": ((1,), (1,)), "tn": ((0,), (0,))}


def _split(x):
    hi = x.astype(BF16)
    return hi, (x - hi.astype(F32)).astype(BF16)


def _mm(x, w, form="nn", x_hp=False):
    dot = lambda p, q: lax.dot_general(p, q, (_DIMS[form], ((), ())), preferred_element_type=F32)
    w = w.astype(BF16)
    if not x_hp:
        return dot(x.astype(BF16), w)
    hi, lo = _split(x)
    return dot(hi, w) + dot(lo, w)


def _mixer_kernel(has_vres, *refs):
    if has_vres:
        (gb_ref, gc_ref, u_ref, r_ref, k_ref, v_ref, lo_ref, vf_ref, pv_ref, wa2_ref, g2_ref, v2_ref,
         y_ref, state_ref, carry_ref, conv_carry_ref) = refs
    else:
        (gb_ref, gc_ref, u_ref, r_ref, k_ref, v_ref, lo_ref, pv_ref, wa2_ref, g2_ref,
         y_ref, vf_out_ref, state_ref, carry_ref, conv_carry_ref) = refs
    NB, C, W = r_ref.shape
    R = NB * C
    G = LANES
    n_groups = W // G

    @pl.when(pl.program_id(1) == 0)
    def _():
        state_ref[...] = jnp.zeros_like(state_ref)
        carry_ref[...] = jnp.zeros_like(carry_ref)
        conv_carry_ref[...] = jnp.zeros_like(conv_carry_ref)

    def pv(i):
        return pv_ref[i:i + 1, :]

    trow = lax.broadcasted_iota(jnp.int32, (C, W), 0)
    for b in range(NB):
        uc = gc_ref[b] * u_ref[b]
        base = b * SUBLANES
        last2 = conv_carry_ref[base + 1:base + 2, :]
        last1 = conv_carry_ref[base + 2:base + 3, :]
        back1 = jnp.where(trow == 0, last1, pltpu.roll(uc, 1, 0))
        back2 = jnp.where(trow == 0, last2, jnp.where(trow == 1, last1, pltpu.roll(uc, 2, 0)))
        conv_carry_ref[base + 1:base + CONV_K, :] = uc[C - CONV_K + 1:C, :]
        conv = pv(PV_CONVW) * back2 + pv(PV_CONVW + 1) * back1 + pv(PV_CONVW + 2) * uc
        y_ref[b, :, 0:W] = _rms(gb_ref[b] * conv, pv(PV_CONVG)).astype(BF16)

    def token_shift(x_ref, slot, mu):
        width = x_ref.shape[-1]
        first = lax.broadcasted_iota(jnp.int32, (C, width), 0) == 0
        parts = []
        for b in range(NB):
            x = x_ref[b]
            crow = b * SUBLANES + slot
            prev = jnp.where(first, carry_ref[crow:crow + 1, :width], pltpu.roll(x, 1, 0))
            carry_ref[crow:crow + 1, :width] = x[C - 1:C, :]
            parts.append(x + mu[:, :width] * (prev - x))
        return jnp.concatenate(parts, axis=0)

    r = token_shift(r_ref, 0, pv(PV_MUR))
    k = token_shift(k_ref, 1, pv(PV_MUK))
    v = token_shift(v_ref, 2, pv(PV_MUV))
    lo = token_shift(lo_ref, 3, pv(PV_MULO))

    wa = lo[:, LO_WD:LO_WD + LANES]
    wa = jnp.where(lax.broadcasted_iota(jnp.int32, wa.shape, 1) < LO_AD, jnp.tanh(wa), wa)
    gd = lo[:, LO_GD:LO_GD + LO_GD_W]
    if has_vres:
        vd = lo[:, LO_VD:LO_VD + LO_VD_W]
        mix = jax.nn.sigmoid(pv(PV_V0) + _mm(vd, v2_ref[...]))
        v = v + (vf_ref[...].reshape(R, W) - v) * mix
    else:
        vf_out_ref[...] = v.reshape(NB, C, W)

    za = _mm(wa, wa2_ref[...])
    logw = -math.exp(-0.5) * jax.nn.sigmoid(pv(PV_W0) + za[:, :W])
    a = jax.nn.sigmoid(pv(PV_A0) + za[:, W:])
    g = _mm(jax.nn.sigmoid(gd), g2_ref[...])
    kk_raw = k * pv(PV_KK)
    k2 = k * (1.0 + (a - 1.0) * pv(PV_KA))

    gi = lax.broadcasted_iota(jnp.int32, (G, G), 0)
    gj = lax.broadcasted_iota(jnp.int32, (G, G), 1)
    same_head = (gi // HEAD_DIM) == (gj // HEAD_DIM)
    bd_mask = same_head.astype(F32).astype(BF16)
    ti = lax.broadcasted_iota(jnp.int32, (C, G), 0)
    si = lax.broadcasted_iota(jnp.int32, (C, G), 1) % HEAD_DIM
    incl_p, strict_p = ti >= si, ti > si
    eye_p = (ti == si).astype(F32)
    half_pair = []
    blk = 2
    while blk <= C:
        half_pair.append(((ti // blk) == (si // blk)) & ((ti // (blk // 2)) != (si // (blk // 2))))
        blk *= 2

    def bd(y):
        return jnp.concatenate([y] * GROUP_HEADS, axis=0) * bd_mask

    def group_cols(x):
        return [x[:, i * G:(i + 1) * G] for i in range(n_groups)]

    def head_sum(x, x_hp):
        s = _mm(jnp.concatenate(group_cols(x), axis=0), bd_mask, x_hp=x_hp)
        return jnp.concatenate([s[i * R:(i + 1) * R] for i in range(n_groups)], axis=1)

    kk = kk_raw * jnp.minimum(lax.rsqrt(head_sum(kk_raw * kk_raw, False)), 1e12)
    bonus = head_sum(r * k2 * pv(PV_RK), True) * v

    tc = lax.broadcasted_iota(jnp.int32, (C, C), 0)
    sc = lax.broadcasted_iota(jnp.int32, (C, C), 1)
    tri = (tc >= sc).astype(F32).astype(BF16)
    tri3 = jnp.concatenate([tri, tri, tri], axis=1)
    l_hi = logw.astype(BF16)
    l_rest = logw - l_hi.astype(F32)
    l_mid = l_rest.astype(BF16)
    l_lo = (l_rest - l_mid.astype(F32)).astype(BF16)
    seq_rows = lambda x, b: x[b * C:(b + 1) * C]
    cs_b = [jnp.dot(tri3, jnp.concatenate([seq_rows(l_hi, b), seq_rows(l_mid, b), seq_rows(l_lo, b)],
                                          axis=0), preferred_element_type=F32) for b in range(NB)]
    cs = jnp.concatenate(cs_b, axis=0)
    cs_last = [c[C - 1:C, :] for c in cs_b]
    cs_end = jnp.concatenate([jnp.broadcast_to(c, (C, W)) for c in cs_last], axis=0)
    e_neg = jnp.exp(-cs)
    e_tail = jnp.exp(cs_end - cs)
    p_last = [jnp.exp(c) for c in cs_last]

    b_raw = kk * a
    r_t = (r * jnp.exp(cs)).astype(BF16)
    a_t = (-kk * jnp.exp(cs - logw)).astype(BF16)
    b_t = (b_raw * e_neg).astype(BF16)
    k_t = (k2 * e_neg).astype(BF16)
    b_p = (b_raw * e_tail).astype(BF16)
    k_p = (k2 * e_tail).astype(BF16)
    v_b = v.astype(BF16)

    def units_of(x):
        return [x[b * C:(b + 1) * C, i * G:(i + 1) * G] for b in range(NB) for i in range(n_groups)]

    rg, ag, bg, kg, vg, bpg, kpg = map(units_of, (r_t, a_t, b_t, k_t, v_b, b_p, k_p))
    units = range(NB * n_groups)
    stack = lambda p, q: jnp.concatenate([p, q], axis=0)
    lhs = [stack(ag[i], rg[i]) for i in units]
    sc = [_mm(lhs[i], stack(bd(bg[i]), bd(kg[i])), "nt") for i in units]
    a_ab = [jnp.where(strict_p, sc[i][:C, :G], 0.0) for i in units]
    a_rb = [jnp.where(incl_p, sc[i][C:, :G], 0.0) for i in units]
    a_ak = [jnp.where(strict_p, sc[i][:C, G:], 0.0) for i in units]
    a_rk = [jnp.where(incl_p, sc[i][C:, G:], 0.0) for i in units]

    inv = [eye_p + jnp.where(half_pair[0], a_ab[i], 0.0) for i in units]
    inv_b = [x.astype(BF16) for x in inv]
    for lvl in range(1, len(half_pair)):
        n_inv = [_mm(jnp.where(half_pair[lvl], a_ab[i], 0.0), bd(inv_b[i])) for i in units]
        inv = [inv[i] + _mm(inv_b[i], bd(n_inv[i].astype(BF16))) for i in units]
        inv_b = [x.astype(BF16) for x in inv]

    av_yv = [_mm(stack(a_ak[i], a_rk[i]), bd(vg[i])) for i in units]
    hat = [_mm(inv_b[i], jnp.concatenate([bd(ag[i]), bd(av_yv[i][:C].astype(BF16))], axis=1))
           for i in units]

    s0 = [state_ref[i] for i in units]
    res = [_mm(stack(hat[i][:, :G].astype(BF16), rg[i]), s0[i], "nt") for i in units]
    u = [(res[i][:C] + hat[i][:, G:]).astype(BF16) for i in units]
    upd = [_mm(stack(u[i], vg[i]), stack(bpg[i], kpg[i]), "tn") for i in units]
    for i in units:
        b, gidx = divmod(i, n_groups)
        state_ref[i] = (s0[i] * p_last[b][:, gidx * G:(gidx + 1) * G]
                        + jnp.where(same_head, upd[i], 0.0))
    y_units = [res[i][C:] + _mm(a_rb[i], bd(u[i])) + av_yv[i][C:] for i in units]

    y = jnp.concatenate([jnp.concatenate(y_units[b * n_groups:(b + 1) * n_groups], axis=1)
                         for b in range(NB)], axis=0)
    yc = y - head_sum(y, True) * (1.0 / HEAD_DIM)
    var = head_sum(yc * yc, False) * (1.0 / HEAD_DIM)
    yn = yc * lax.rsqrt(var + LNX_EPS) * pv(PV_LNG) + pv(PV_LNB)
    y_ref[:, :, W:2 * W] = ((yn + bonus) * g).astype(BF16).reshape(NB, C, W)


def _mixer(proj, proj_lo, v_first, pvec, wa2, g2, v2, layer, *, batch, nb):
    Lp = proj.shape[1]
    W = pvec.shape[2]
    C = WKV_CHUNK
    has_vres = v_first is not None
    tok = lambda col: pl.BlockSpec((nb, C, W), lambda bp, c: (bp, c, col))
    par = lambda arr: pl.BlockSpec((None,) + arr.shape[1:], lambda bp, c: (layer, 0, 0))
    in_specs = [tok(col) for col in range(6)] + [pl.BlockSpec((nb, C, LO_W), lambda bp, c: (bp, c, 0))]
    args = [proj] * 6 + [proj_lo]
    if has_vres:
        in_specs.append(tok(0))
        args.append(v_first)
    in_specs += [par(pvec), par(wa2), par(g2)]
    args += [pvec, wa2, g2]
    out_shape = [jax.ShapeDtypeStruct((batch, Lp, 2 * W), BF16)]
    out_specs = [pl.BlockSpec((nb, C, 2 * W), lambda bp, c: (bp, c, 0))]
    if has_vres:
        in_specs.append(par(v2))
        args.append(v2)
    else:
        out_shape.append(jax.ShapeDtypeStruct((batch, Lp, W), F32))
        out_specs.append(tok(0))
    return pl.pallas_call(
        functools.partial(_mixer_kernel, has_vres),
        out_shape=tuple(out_shape),
        grid=(batch // nb, Lp // C),
        in_specs=in_specs,
        out_specs=tuple(out_specs),
        scratch_shapes=[
            pltpu.VMEM((nb * (W // LANES), LANES, LANES), F32),
            pltpu.VMEM((nb * SUBLANES, W), F32),
            pltpu.VMEM((nb * SUBLANES, W), F32),
        ],
        compiler_params=_cparams(("parallel", "arbitrary")),
        name="mixer",
    )(*args)


def _proj_out_kernel(y_ref, w_ref, g_ref, h_ref, o_ref):
    y = jnp.dot(y_ref[...], w_ref[...], preferred_element_type=F32)

    def write_out(rows, y_normed):
        o_ref[rows, :] = h_ref[rows, :] + y_normed

    _rms_strips(y.shape[0], lambda rows: y[rows, :], write_out, g_ref[...])


def _proj_out(y_mix, w_out, g, h, layer, *, tm):
    T, D = h.shape
    return pl.pallas_call(
        _proj_out_kernel,
        out_shape=jax.ShapeDtypeStruct((T, D), F32),
        grid=(T // tm,),
        in_specs=[
            pl.BlockSpec((tm, y_mix.shape[1]), lambda i: (i, 0)),
            pl.BlockSpec((None,) + w_out.shape[1:], lambda i: (layer, 0, 0)),
            _layer_vec(layer, D),
            pl.BlockSpec((tm, D), lambda i: (i, 0)),
        ],
        out_specs=pl.BlockSpec((tm, D), lambda i: (i, 0)),
        compiler_params=_cparams(("parallel",)),
        name="proj_out",
    )(y_mix, w_out, g, h)


def _pad_axis(w, axis, size):
    pad = [(0, 0)] * w.ndim
    pad[axis] = (0, size - w.shape[axis])
    return jnp.pad(w, pad)


def _lora_cols(wd, ad, gd, vd):
    assert LO_AD == wd.shape[-1] and LO_GD == LO_AD + ad.shape[-1]
    return jnp.concatenate([wd, ad, _pad_axis(gd, -1, LO_GD_W), _pad_axis(vd, -1, LO_VD_W)], axis=-1)


def _first_layer_zero(w):
    return jnp.concatenate([jnp.zeros_like(w[:1]), w], axis=0)


def kernel(x, meta_tokens, ffn1_pre_g, ffn1_w_gu, ffn1_w_down, ffn1_post_g, mix_pre_g, w_in, w_in_vres, mu_rwkv, mu_vres, conv_w, conv_norm_g, decay_w0, decay_w2, iclr_a0, iclr_a2, vres_v0, vres_v2, gate_g2, k_k, k_a, r_k, lnx_g, lnx_b, w_out, mix_post_g, ffn2_pre_g, ffn2_w_gu, ffn2_w_down, ffn2_post_g):
    B, S, D = x.shape
    depth = w_in.shape[0]
    conv_width = conv_w.shape[2]
    rw = decay_w0.shape[1]
    L = N_META + S
    Lp = -(-L // WKV_CHUNK) * WKV_CHUNK
    T = B * Lp
    pick = lambda n, cands: next(c for c in cands if n % c == 0)
    tm = pick(T, (640, 512, 256, WKV_CHUNK))
    tm_ffn = pick(T, (1040, 640, 512, 256, WKV_CHUNK))
    tm_proj = pick(T, (1280, 1040, 832, 640, 512, 256, WKV_CHUNK))
    tf = pick(ffn1_w_down.shape[1], (512, 256, 128))
    nb = pick(B, (4, 2, 1))
    o = 3 * conv_width + 3 * rw

    h = jnp.pad(x, ((0, 0), (N_META, Lp - L), (0, 0)))
    meta = jnp.broadcast_to(meta_tokens.astype(x.dtype)[None], (B, N_META, D))
    h = lax.dynamic_update_slice(h, meta, (0, 0, 0)).reshape(T, D)

    vec = lambda a: a.reshape(depth, 1, -1)
    ffn1 = (vec(ffn1_pre_g), ffn1_w_gu.astype(BF16), ffn1_w_down.astype(BF16), vec(0.5 * ffn1_post_g))
    ffn2 = (vec(ffn2_pre_g), ffn2_w_gu.astype(BF16), ffn2_w_down.astype(BF16), vec(0.5 * ffn2_post_g))
    w_main = w_in.astype(BF16)
    w_lo = _lora_cols(w_in[..., o:o + R_DECAY], w_in[..., o + R_DECAY:o + R_DECAY + R_ICLR],
                      w_in[..., o + R_DECAY + R_ICLR:], _first_layer_zero(w_in_vres)).astype(BF16)
    m = 3 * rw
    mu_lo = _lora_cols(mu_rwkv[:, m:m + R_DECAY], mu_rwkv[:, m + R_DECAY:m + R_DECAY + R_ICLR],
                       mu_rwkv[:, m + R_DECAY + R_ICLR:], _first_layer_zero(mu_vres))
    pvec = jnp.stack([decay_w0, iclr_a0, _first_layer_zero(vres_v0), k_k, k_a, r_k.reshape(depth, rw),
                      lnx_g, lnx_b, mu_rwkv[:, :rw], mu_rwkv[:, rw:2 * rw], mu_rwkv[:, 2 * rw:m],
                      _pad_axis(mu_lo, 1, rw), conv_norm_g]
                     + [conv_w[:, tap] for tap in range(CONV_K)], axis=1)
    assert pvec.shape[1:] == (PV_ROWS, rw) and conv_w.shape[1] == CONV_K and conv_width == rw
    wa2 = jnp.concatenate([_pad_axis(decay_w2, 2, 2 * rw),
                           jnp.pad(iclr_a2, ((0, 0), (0, 0), (rw, 0)))], axis=1).astype(BF16)
    g2 = _pad_axis(gate_g2, 1, LO_GD_W).astype(BF16)
    v2 = _pad_axis(_first_layer_zero(vres_v2), 1, LO_VD_W).astype(BF16)
    w_out_b = w_out.astype(BF16)
    mix_pre, mix_post = vec(mix_pre_g), vec(mix_post_g)

    v_first = None
    for i in range(depth):
        h = _ffn(h, *ffn1, i, tm=tm_ffn, tf=tf)
        proj, proj_lo = _proj_in(h, mix_pre, w_main, o, w_lo, i, tm=tm_proj, tn=rw)
        proj3, proj_lo3 = proj.reshape(B, Lp, o), proj_lo.reshape(B, Lp, LO_W)
        if i == 0:
            y_mix, v_first = _mixer(proj3, proj_lo3, None, pvec, wa2, g2, None, i, batch=B, nb=nb)
        else:
            (y_mix,) = _mixer(proj3, proj_lo3, v_first, pvec, wa2, g2, v2, i, batch=B, nb=nb)
        h = _proj_out(y_mix.reshape(T, 2 * rw), w_out_b, mix_post, h, i, tm=tm)
        if i < depth - 1:
            h = _ffn(h, *ffn2, i, tm=tm_ffn, tf=tf)
    out = _ffn(h, *ffn2, depth - 1, tm=pick(S, (1024, 512, 256, WKV_CHUNK)), tf=tf, rows=(Lp, N_META, S))
    return out.reshape(B, S, D)
```
